```python
import jax, jax.numpy as jnp
from jax import lax
import numpy as np

D_MODEL = 2048
BATCH = 8
SEQ = 4096
DEPTH = 1
DEC_BATCH = 32
DEC_SEQ = 16
PAST_LEN = 4096

CHUNK = 64
N_META = 16
RMS_EPS = 1e-6
D_FF = 5504
RW_WIDTH = D_MODEL // 2
RW_HEAD = 64
RW_HEADS = RW_WIDTH // RW_HEAD
RW_DECAY_LORA = 64
RW_A_LORA = 64
RW_GATE_LORA = 160
RW_COLS = 3 * RW_WIDTH + RW_DECAY_LORA + RW_A_LORA + RW_GATE_LORA
RW_GN_EPS = 64e-5
ML_WIDTH = D_MODEL // 2
ML_HEADS = 4
ML_HEAD = ML_WIDTH // ML_HEADS
ML_CONV = 4
ML_LN_EPS = 1e-5
ML_COLS = 4 * ML_WIDTH + 2 * ML_HEADS
GATE_COLS = 2 * D_MODEL
IN_COLS = RW_COLS + ML_COLS + GATE_COLS

kernel_name = "rwkv7_mlstm_gated_hybrid_stream_step"


def _rmsnorm(x, g):
    xf = x.astype(jnp.float32)
    y = xf * lax.rsqrt(jnp.mean(xf * xf, -1, keepdims=True) + RMS_EPS)
    return (y * g.astype(jnp.float32)).astype(x.dtype)


def _swiglu(x, w_gate, w_up, w_down):
    return (jax.nn.silu(x @ w_gate) * (x @ w_up)) @ w_down


def _norm_last(y, eps):
    mu = jnp.mean(y, -1, keepdims=True)
    var = jnp.mean(jnp.square(y - mu), -1, keepdims=True)
    return (y - mu) * lax.rsqrt(var + eps)


def _causal_conv(u, buf, w, b):
    T = u.shape[1]
    padded = jnp.concatenate([buf.astype(u.dtype), u], 1)
    out = b.astype(u.dtype)
    for j in range(ML_CONV):
        out = out + w[j].astype(u.dtype) * padded[:, j:j + T]
    return out, padded[:, T:]


def _rwkv7_scan(r, w, k, v, a_vec, b_vec, S0):
    def step(S, inp):
        r_t, w_t, k_t, v_t, a_t, b_t = inp
        sa = jnp.einsum('bhvk,bhk->bhv', S, a_t)
        S = S * w_t[:, :, None, :] + sa[..., None] * b_t[:, :, None, :] + v_t[..., None] * k_t[:, :, None, :]
        return S, jnp.einsum('bhvk,bhk->bhv', S, r_t)
    xs = tuple(jnp.moveaxis(t, 1, 0) for t in (r, w, k, v, a_vec, b_vec))
    S, ys = lax.scan(step, S0, xs)
    return jnp.moveaxis(ys, 0, 1), S


def _mlstm_chunkwise(q, k, v, i_pre, logf, C0, n0, m0, L):
    B, H, T, d = q.shape
    nc = T // L

    def to_chunks(t):
        t = t.reshape(B, H, nc, L, *t.shape[3:])
        return jnp.moveaxis(t, 2, 0)

    causal = jnp.tril(jnp.ones((L, L), bool))

    def step(carry, inp):
        C, n, m = carry
        qc, kc, vc, ic, fc = inp
        b = jnp.cumsum(fc, -1)
        log_inter = b + m[..., None]
        D = b[..., :, None] - b[..., None, :] + ic[..., None, :]
        D = jnp.where(causal, D, -jnp.inf)
        m_q = jnp.maximum(log_inter, jnp.max(D, -1))
        w_inter = jnp.exp(log_inter - m_q)
        s = jnp.einsum('bhtd,bhsd->bhts', qc, kc) * jnp.exp(D - m_q[..., None])
        num = w_inter[..., None] * jnp.einsum('bhtk,bhkv->bhtv', qc, C) + jnp.einsum('bhts,bhsv->bhtv', s, vc)
        den = w_inter * jnp.einsum('bhtk,bhk->bht', qc, n) + jnp.sum(s, -1)
        h = num / jnp.maximum(jnp.abs(den), jnp.exp(-m_q))[..., None]
        g = b[..., -1:] - b + ic
        m_new = jnp.maximum(b[..., -1] + m, jnp.max(g, -1))
        a_st = jnp.exp(b[..., -1] + m - m_new)
        wk = jnp.exp(g - m_new[..., None])
        C = a_st[..., None, None] * C + jnp.einsum('bhs,bhsk,bhsv->bhkv', wk, kc, vc)
        n = a_st[..., None] * n + jnp.einsum('bhs,bhsk->bhk', wk, kc)
        return (C, n, m_new), h

    (C, n, m), hs = lax.scan(step, (C0, n0, m0), tuple(map(to_chunks, (q, k, v, i_pre, logf))))
    h = jnp.moveaxis(hs, 0, 2).reshape(B, H, T, d)
    return h, C, n, m


def _mixer(h, states, prm, l, n_lead):
    f32 = jnp.float32
    rw_shift0, rw_S0, ml_conv0, ml_C0, ml_n0, ml_m0 = states
    B, T, _ = h.shape
    proj = (h @ prm['w_in'][l]).astype(f32)
    o1 = RW_COLS
    o2 = o1 + 2 * ML_WIDTH
    o3 = o2 + ML_WIDTH
    o4 = o3 + ML_WIDTH
    o5 = o4 + ML_HEADS
    o6 = o5 + ML_HEADS
    p_rw, p_qk, p_v, p_o = proj[..., :o1], proj[..., o1:o2], proj[..., o2:o3], proj[..., o3:o4]
    p_i, p_f, p_gate = proj[..., o4:o5], proj[..., o5:o6], proj[..., o6:]

    prev = jnp.concatenate([rw_shift0[:, None].astype(f32), p_rw[:, :-1]], 1)
    u = p_rw + prm['rw_mu'][l].astype(f32) * (prev - p_rw)
    new_rw_shift = p_rw[:, -1]
    W = RW_WIDTH
    r, k, v = u[..., :W], u[..., W:2 * W], u[..., 2 * W:3 * W]
    lw = u[..., 3 * W:3 * W + RW_DECAY_LORA]
    la = u[..., 3 * W + RW_DECAY_LORA:3 * W + RW_DECAY_LORA + RW_A_LORA]
    lg = u[..., 3 * W + RW_DECAY_LORA + RW_A_LORA:]
    wlog = -jax.nn.softplus(-(prm['rw_w0'][l] + jnp.tanh(lw) @ prm['rw_w2'][l])) - 0.5
    decay = jnp.exp(-jnp.exp(wlog.astype(f32)))
    a = jax.nn.sigmoid((prm['rw_a0'][l] + la @ prm['rw_a2'][l]).astype(f32))
    g = (jax.nn.sigmoid(lg) @ prm['rw_g2'][l]).astype(f32)
    heads = lambda t: t.reshape(B, T, RW_HEADS, RW_HEAD)
    kk = heads(k * prm['rw_kk'][l])
    kk = (kk / jnp.maximum(jnp.sqrt(jnp.sum(kk * kk, -1, keepdims=True)), 1e-12)).astype(f32)
    k = (k * (1.0 + (a - 1.0) * prm['rw_ka'][l])).astype(f32)
    rh, kh, vh, ah, wh = heads(r), heads(k), heads(v), heads(a), heads(decay)
    y, new_rw_S = _rwkv7_scan(rh, wh, kh, vh, -kk, kk * ah, rw_S0.astype(f32))
    y = _norm_last(y, RW_GN_EPS).reshape(B, T, W) * prm['rw_ln_w'][l] + prm['rw_ln_b'][l]
    bonus = jnp.sum(rh * kh * prm['rw_rk'][l].astype(f32), -1, keepdims=True) * vh
    y_rw = ((y + bonus.reshape(B, T, W)) * g).astype(f32)

    qk, new_ml_conv = _causal_conv(p_qk, ml_conv0, prm['ml_conv_w'][l], prm['ml_conv_b'][l])
    qk = jax.nn.silu(qk.astype(f32))
    mh = lambda t: t.reshape(B, T, ML_HEADS, ML_HEAD).transpose(0, 2, 1, 3)
    q = mh(qk[..., :ML_WIDTH])
    km = mh(qk[..., ML_WIDTH:] * (ML_HEAD ** -0.5))
    vm = mh(p_v)
    i_pre = (p_i + prm['ml_i_b'][l]).astype(f32).transpose(0, 2, 1)
    logf = jax.nn.log_sigmoid((p_f + prm['ml_f_b'][l]).astype(f32)).transpose(0, 2, 1)
    C, n, m = ml_C0.astype(f32), ml_n0.astype(f32), ml_m0.astype(f32)
    segments = [(0, n_lead, n_lead), (n_lead, T, min(CHUNK, T - n_lead))] if n_lead > 0 else [(0, T, min(CHUNK, T))]
    hs = []
    for (s0, s1, L) in segments:
        hseg, C, n, m = _mlstm_chunkwise(q[:, :, s0:s1], km[:, :, s0:s1], vm[:, :, s0:s1],
                                         i_pre[..., s0:s1], logf[..., s0:s1], C, n, m, L)
        hs.append(hseg)
    hm = jnp.concatenate(hs, 2)
    hm = _norm_last(hm, ML_LN_EPS).transpose(0, 2, 1, 3).reshape(B, T, ML_WIDTH)
    y_ml = hm * prm['ml_norm_w'][l] * jax.nn.sigmoid(p_o)

    gates = jax.nn.sigmoid(p_gate)
    merged = gates[..., :D_MODEL] * (y_rw @ prm['w_br_rw'][l]) + gates[..., D_MODEL:] * (y_ml @ prm['w_br_ml'][l])
    out = (merged @ prm['w_out'][l]).astype(h.dtype)
    return out, (new_rw_shift, new_rw_S, new_ml_conv, C, n, m)


def _trunk(x, states, prm, n_lead):
    collected = []
    for l in range(DEPTH):
        x = x + 0.5 * _swiglu(_rmsnorm(x, prm['ffn1_norm'][l]), prm['ffn1_w_gate'][l], prm['ffn1_w_up'][l], prm['ffn1_w_down'][l])
        mix, ns = _mixer(_rmsnorm(x, prm['mix_norm'][l]), tuple(s[l] for s in states), prm, l, n_lead)
        x = x + mix
        x = x + 0.5 * _swiglu(_rmsnorm(x, prm['ffn2_norm'][l]), prm['ffn2_w_gate'][l], prm['ffn2_w_up'][l], prm['ffn2_w_down'][l])
        collected.append(ns)
    new_states = tuple(jnp.stack([ns[i] for ns in collected]) for i in range(len(collected[0])))
    return _rmsnorm(x, prm['final_norm']), new_states


def setup_inputs(seed: int = 0) -> dict:
    key = jax.random.key(seed)
    ks = iter(jax.random.split(key, 48))
    nrm = lambda shape, s: jax.random.normal(next(ks), shape, jnp.float32) * s
    uni = lambda shape, lo, hi: jax.random.uniform(next(ks), shape, jnp.float32, lo, hi)
    Dp = DEPTH
    f_bias = jnp.broadcast_to(jnp.linspace(3.0, 6.0, ML_HEADS, dtype=jnp.float32), (Dp, ML_HEADS))
    return {
        "x_prompt": nrm((BATCH, SEQ, D_MODEL), 1.0),
        "x_sample": nrm((DEC_BATCH, DEC_SEQ, D_MODEL), 1.0),
        "state_rwkv_shift": nrm((Dp, DEC_BATCH, RW_COLS), 1.0),
        "state_rwkv_wkv": nrm((Dp, DEC_BATCH, RW_HEADS, RW_HEAD, RW_HEAD), 0.1),
        "state_mlstm_conv": nrm((Dp, DEC_BATCH, ML_CONV - 1, 2 * ML_WIDTH), 1.0),
        "state_mlstm_C": nrm((Dp, DEC_BATCH, ML_HEADS, ML_HEAD, ML_HEAD), 0.05),
        "state_mlstm_n": nrm((Dp, DEC_BATCH, ML_HEADS, ML_HEAD), 0.5),
        "state_mlstm_m": nrm((Dp, DEC_BATCH, ML_HEADS), 1.0),
        "meta_tokens": nrm((N_META, D_MODEL), 1.0),
        "ffn1_norm": 1.0 + nrm((Dp, D_MODEL), 0.02),
        "ffn1_w_gate": nrm((Dp, D_MODEL, D_FF), D_MODEL ** -0.5),
        "ffn1_w_up": nrm((Dp, D_MODEL, D_FF), D_MODEL ** -0.5),
        "ffn1_w_down": nrm((Dp, D_FF, D_MODEL), D_FF ** -0.5),
        "mix_norm": 1.0 + nrm((Dp, D_MODEL), 0.02),
        "w_in": nrm((Dp, D_MODEL, IN_COLS), D_MODEL ** -0.5),
        "rw_mu": uni((Dp, RW_COLS), 0.0, 1.0),
        "rw_w0": uni((Dp, RW_WIDTH), -6.0, 0.0),
        "rw_w2": nrm((Dp, RW_DECAY_LORA, RW_WIDTH), 0.1),
        "rw_a0": nrm((Dp, RW_WIDTH), 0.5),
        "rw_a2": nrm((Dp, RW_A_LORA, RW_WIDTH), 0.5 * RW_A_LORA ** -0.5),
        "rw_g2": nrm((Dp, RW_GATE_LORA, RW_WIDTH), RW_GATE_LORA ** -0.5),
        "rw_kk": 1.0 + nrm((Dp, RW_WIDTH), 0.1),
        "rw_ka": 1.0 + nrm((Dp, RW_WIDTH), 0.1),
        "rw_rk": nrm((Dp, RW_HEADS, RW_HEAD), 0.1),
        "rw_ln_w": 1.0 + nrm((Dp, RW_WIDTH), 0.02),
        "rw_ln_b": nrm((Dp, RW_WIDTH), 0.02),
        "ml_conv_w": nrm((Dp, ML_CONV, 2 * ML_WIDTH), 0.5),
        "ml_conv_b": nrm((Dp, 2 * ML_WIDTH), 0.02),
        "ml_i_b": nrm((Dp, ML_HEADS), 0.1),
        "ml_f_b": f_bias + nrm((Dp, ML_HEADS), 0.1),
        "ml_norm_w": 1.0 + nrm((Dp, ML_WIDTH), 0.02),
        "w_br_rw": nrm((Dp, RW_WIDTH, D_MODEL), RW_WIDTH ** -0.5),
        "w_br_ml": nrm((Dp, ML_WIDTH, D_MODEL), ML_WIDTH ** -0.5),
        "w_out": nrm((Dp, D_MODEL, D_MODEL), D_MODEL ** -0.5),
        "ffn2_norm": 1.0 + nrm((Dp, D_MODEL), 0.02),
        "ffn2_w_gate": nrm((Dp, D_MODEL, D_FF), D_MODEL ** -0.5),
        "ffn2_w_up": nrm((Dp, D_MODEL, D_FF), D_MODEL ** -0.5),
        "ffn2_w_down": nrm((Dp, D_FF, D_MODEL), D_FF ** -0.5),
        "final_norm": 1.0 + nrm((D_MODEL,), 0.02),
    }


def reference(x_prompt, x_sample, state_rwkv_shift, state_rwkv_wkv, state_mlstm_conv, state_mlstm_C,
              state_mlstm_n, state_mlstm_m, meta_tokens, ffn1_norm, ffn1_w_gate, ffn1_w_up, ffn1_w_down,
              mix_norm, w_in, rw_mu, rw_w0, rw_w2, rw_a0, rw_a2, rw_g2, rw_kk, rw_ka, rw_rk, rw_ln_w, rw_ln_b,
              ml_conv_w, ml_conv_b, ml_i_b, ml_f_b, ml_norm_w, w_br_rw, w_br_ml, w_out,
              ffn2_norm, ffn2_w_gate, ffn2_w_up, ffn2_w_down, final_norm):
    f32 = jnp.float32
    prm = dict(ffn1_norm=ffn1_norm, ffn1_w_gate=ffn1_w_gate, ffn1_w_up=ffn1_w_up, ffn1_w_down=ffn1_w_down,
               mix_norm=mix_norm, w_in=w_in, rw_mu=rw_mu, rw_w0=rw_w0, rw_w2=rw_w2, rw_a0=rw_a0, rw_a2=rw_a2,
               rw_g2=rw_g2, rw_kk=rw_kk, rw_ka=rw_ka, rw_rk=rw_rk, rw_ln_w=rw_ln_w, rw_ln_b=rw_ln_b,
               ml_conv_w=ml_conv_w, ml_conv_b=ml_conv_b, ml_i_b=ml_i_b, ml_f_b=ml_f_b, ml_norm_w=ml_norm_w,
               w_br_rw=w_br_rw, w_br_ml=w_br_ml, w_out=w_out, ffn2_norm=ffn2_norm, ffn2_w_gate=ffn2_w_gate,
               ffn2_w_up=ffn2_w_up, ffn2_w_down=ffn2_w_down, final_norm=final_norm)

    B = x_prompt.shape[0]
    meta = jnp.broadcast_to(meta_tokens.astype(x_prompt.dtype)[None], (B, N_META, D_MODEL))
    xp = jnp.concatenate([meta, x_prompt], 1)
    zero_states = (jnp.zeros((DEPTH, B, RW_COLS), f32),
                   jnp.zeros((DEPTH, B, RW_HEADS, RW_HEAD, RW_HEAD), f32),
                   jnp.zeros((DEPTH, B, ML_CONV - 1, 2 * ML_WIDTH), f32),
                   jnp.zeros((DEPTH, B, ML_HEADS, ML_HEAD, ML_HEAD), f32),
                   jnp.zeros((DEPTH, B, ML_HEADS, ML_HEAD), f32),
                   jnp.zeros((DEPTH, B, ML_HEADS), f32))
    yp, (p_rw_shift, p_rw_wkv, p_ml_conv, p_ml_C, p_ml_n, p_ml_m) = _trunk(xp, zero_states, prm, N_META)
    y_prompt = yp[:, N_META:]

    s_states = (state_rwkv_shift, state_rwkv_wkv, state_mlstm_conv, state_mlstm_C, state_mlstm_n, state_mlstm_m)
    y_sample, (s_rw_shift, s_rw_wkv, s_ml_conv, s_ml_C, s_ml_n, s_ml_m) = _trunk(x_sample, s_states, prm, 0)

    return (y_prompt, y_sample,
            p_rw_shift, p_rw_wkv, p_ml_conv, p_ml_C, p_ml_n, p_ml_m,
            s_rw_shift, s_rw_wkv, s_ml_conv, s_ml_C, s_ml_n, s_ml_m)
```

```python
import functools

import jax
import jax.numpy as jnp
from jax import lax
from jax.experimental import pallas as pl
from jax.experimental.pallas import tpu as pltpu

F32 = jnp.float32
BF16 = jnp.bfloat16

LANE = 128
SUBLANE = 8
VMEM_LIMIT_BYTES = 56 * 1024 * 1024
CHUNK = 64
ROW_TILE = 512
FF_TILE = 512
COL_TILE = 512
MISC_COLS = 512
MISC_IF = 384
RMS_EPS = 1e-6
RW_GN_EPS = 64e-5
ML_LN_EPS = 1e-5
NEG = -1e30


def _dot(a, b):
    return jnp.dot(a, b, preferred_element_type=F32)


def _dot_nt(a, b):
    return lax.dot_general(a, b, (((1,), (1,)), ((), ())), preferred_element_type=F32)


def _dot_tn(a, b):
    return lax.dot_general(a, b, (((0,), (0,)), ((), ())), preferred_element_type=F32)


def _split3(x):
    h1 = x.astype(BF16)
    r1 = x - h1.astype(F32)
    h2 = r1.astype(BF16)
    r2 = r1 - h2.astype(F32)
    return h1, h2, r2.astype(BF16)


def _cumsum_rows(tri, x):
    n = x.shape[1]
    y = _dot(tri, jnp.concatenate(_split3(x), axis=1))
    return y[:, :n] + y[:, n:2 * n] + y[:, 2 * n:]


def _segsum(x, seg):
    rows = x.shape[0]
    y = _dot(jnp.concatenate(_split3(x), axis=0), seg)
    return y[:rows] + y[rows:2 * rows] + y[2 * rows:]


def _sigmoid(z):
    return jax.nn.sigmoid(z)


def _softplus(z):
    return jnp.maximum(z, 0.0) + jnp.log1p(jnp.exp(-jnp.abs(z)))


def _rmsnorm(x, g):
    return x * lax.rsqrt(jnp.mean(x * x, axis=-1, keepdims=True) + RMS_EPS) * g


def _iota(shape, dim):
    return lax.broadcasted_iota(jnp.int32, shape, dim)


def _params(*semantics):
    return pltpu.CompilerParams(dimension_semantics=semantics, vmem_limit_bytes=VMEM_LIMIT_BYTES)


def _ffn_kernel(x_ref, g_ref, wgu_ref, wd_ref, fg_ref, o_ref, xn_ref, *, tf, n_f, final_norm):
    j = pl.program_id(1)

    @pl.when(j == 0)
    def _():
        xn_ref[...] = _rmsnorm(x_ref[...], g_ref[...]).astype(BF16)
        o_ref[...] = jnp.zeros_like(o_ref)

    gu = _dot(xn_ref[...], wgu_ref[...])
    gate = gu[:, :tf]
    h = (gate * _sigmoid(gate) * gu[:, tf:]).astype(BF16)
    o_ref[...] += _dot(h, wd_ref[...])

    @pl.when(j == n_f - 1)
    def _():
        y = x_ref[...] + 0.5 * o_ref[...]
        if final_norm:
            y = _rmsnorm(y, fg_ref[...])
        o_ref[...] = y


def _ffn(x, norm_g, wgu, wd, final_g, *, final_norm):
    n, d = x.shape
    n_f = wd.shape[0] // FF_TILE
    kern = functools.partial(_ffn_kernel, tf=FF_TILE, n_f=n_f, final_norm=final_norm)
    return pl.pallas_call(
        kern,
        grid=(n // ROW_TILE, n_f),
        in_specs=[
            pl.BlockSpec((ROW_TILE, d), lambda i, j: (i, 0)),
            pl.BlockSpec((1, d), lambda i, j: (0, 0)),
            pl.BlockSpec((d, 2 * FF_TILE), lambda i, j: (0, j)),
            pl.BlockSpec((FF_TILE, d), lambda i, j: (j, 0)),
            pl.BlockSpec((1, d), lambda i, j: (0, 0)),
        ],
        out_specs=pl.BlockSpec((ROW_TILE, d), lambda i, j: (i, 0)),
        out_shape=jax.ShapeDtypeStruct((n, d), F32),
        scratch_shapes=[pltpu.VMEM((ROW_TILE, d), BF16)],
        compiler_params=_params("parallel", "arbitrary"),
        name="ffn_final" if final_norm else "ffn",
    )(x, norm_g, wgu, wd, final_g)


def _proj_kernel(x_ref, g_ref, w_ref, o_ref, xn_ref):
    @pl.when(pl.program_id(1) == 0)
    def _():
        xn_ref[...] = _rmsnorm(x_ref[...], g_ref[...]).astype(BF16)

    o_ref[...] = _dot(xn_ref[...], w_ref[...])


def _proj(x, norm_g, w):
    n, d = x.shape
    cols = w.shape[1]
    return pl.pallas_call(
        _proj_kernel,
        grid=(n // ROW_TILE, cols // COL_TILE),
        in_specs=[
            pl.BlockSpec((ROW_TILE, d), lambda i, j: (i, 0)),
            pl.BlockSpec((1, d), lambda i, j: (0, 0)),
            pl.BlockSpec((d, COL_TILE), lambda i, j: (0, j)),
        ],
        out_specs=pl.BlockSpec((ROW_TILE, COL_TILE), lambda i, j: (i, j)),
        out_shape=jax.ShapeDtypeStruct((n, cols), F32),
        scratch_shapes=[pltpu.VMEM((ROW_TILE, d), BF16)],
        compiler_params=_params("parallel", "arbitrary"),
        name="in_proj",
    )(x, norm_g, w)


def _merge_kernel(x_ref, yrw_ref, yml_ref, ga_ref, gb_ref, wrw_ref, wml_ref, wo_ref, o_ref, *, n_j):
    j = pl.program_id(1)

    @pl.when(j == 0)
    def _():
        o_ref[...] = jnp.zeros_like(o_ref)

    merged = (_sigmoid(ga_ref[...]) * _dot(yrw_ref[...], wrw_ref[...])
              + _sigmoid(gb_ref[...]) * _dot(yml_ref[...], wml_ref[...]))
    o_ref[...] += _dot(merged.astype(BF16), wo_ref[...])

    @pl.when(j == n_j - 1)
    def _():
        o_ref[...] = x_ref[...] + o_ref[...]


def _merge(x, y_rw, y_ml, proj, w_rw, w_ml, w_out):
    n, d = x.shape
    w = y_rw.shape[1]
    n_j = d // COL_TILE
    kern = functools.partial(_merge_kernel, n_j=n_j)
    return pl.pallas_call(
        kern,
        grid=(n // ROW_TILE, n_j),
        in_specs=[
            pl.BlockSpec((ROW_TILE, d), lambda i, j: (i, 0)),
            pl.BlockSpec((ROW_TILE, w), lambda i, j: (i, 0)),
            pl.BlockSpec((ROW_TILE, w), lambda i, j: (i, 0)),
            pl.BlockSpec((ROW_TILE, COL_TILE), lambda i, j: (i, j)),
            pl.BlockSpec((ROW_TILE, COL_TILE), lambda i, j: (i, n_j + j)),
            pl.BlockSpec((w, COL_TILE), lambda i, j: (0, j)),
            pl.BlockSpec((w, COL_TILE), lambda i, j: (0, j)),
            pl.BlockSpec((COL_TILE, d), lambda i, j: (j, 0)),
        ],
        out_specs=pl.BlockSpec((ROW_TILE, d), lambda i, j: (i, 0)),
        out_shape=jax.ShapeDtypeStruct((n, d), F32),
        compiler_params=_params("parallel", "arbitrary"),
        name="merge",
    )(x, y_rw, y_ml, proj, proj, w_rw, w_ml, w_out)


def _rwkv_pair(r, k, v, a, lw, g, s0, kkw, kaw, rkw, lnw, lnb, cst):
    L = r.shape[0]
    tri, seg, head0, strict, incl, col_head0, eye2, blockdiag = cst

    kk_ = k * kkw
    kk = kk_ / jnp.maximum(jnp.sqrt(_segsum(kk_ * kk_, seg)), 1e-12)
    k2 = k * (1.0 + (a - 1.0) * kaw)
    kka = kk * a

    cum = _cumsum_rows(tri, lw)
    c_end = cum[L - 1:L, :]
    e_inv = jnp.exp(-cum)
    e_rem = jnp.exp(c_end - cum)
    at = (-kk * jnp.exp(cum - lw)).astype(BF16)
    rt = (r * jnp.exp(cum)).astype(BF16)
    bt = (kka * e_inv).astype(BF16)
    kt = (k2 * e_inv).astype(BF16)
    vb = v.astype(BF16)
    zero = jnp.zeros_like(vb)

    def by_head(x):
        return jnp.concatenate([jnp.where(head0, x, zero), jnp.where(head0, zero, x)], axis=0)

    lhs = jnp.concatenate([at, rt], axis=0)
    ab = _dot_nt(lhs, by_head(bt))
    ak = _dot_nt(lhs, by_head(kt))
    n_cat = jnp.where(strict, ab[:L], 0.0)
    a_rb = jnp.where(incl, ab[L:], 0.0)
    a_ak = jnp.where(strict, ak[:L], 0.0)
    a_rk = jnp.where(incl, ak[L:], 0.0)

    n_bd = jnp.concatenate([jnp.where(col_head0, n_cat, 0.0), jnp.where(col_head0, 0.0, n_cat)], axis=0)
    t_inv = eye2 + n_bd
    p = n_bd
    for _ in range(L.bit_length() - 2):
        pb = p.astype(BF16)
        p = _dot(pb, pb)
        t_inv = t_inv + _dot(t_inv.astype(BF16), p.astype(BF16))

    sb = s0.astype(BF16)
    vs = by_head(vb)
    x = _dot_nt(at, sb) + _dot(a_ak.astype(BF16), vs)
    us = _dot(t_inv.astype(BF16), by_head(x.astype(BF16)))
    usb = us.astype(BF16)
    y = _dot_nt(rt, sb) + _dot(jnp.concatenate([a_rb, a_rk], axis=1).astype(BF16),
                               jnp.concatenate([usb, vs], axis=0))
    u_all = (us[:L] + us[L:]).astype(BF16)
    upd = _dot_tn(jnp.concatenate([u_all, vb], axis=0),
                  jnp.concatenate([(kka * e_rem).astype(BF16), (k2 * e_rem).astype(BF16)], axis=0))
    s_new = s0 * jnp.exp(c_end) + jnp.where(blockdiag, upd, 0.0)

    hd = float(LANE // 2)
    mean = _segsum(y, seg) / hd
    yc = y - mean
    var = _segsum(yc * yc, seg) / hd
    yn = yc * lax.rsqrt(var + RW_GN_EPS) * lnw + lnb
    bonus = _segsum(r * k2 * rkw, seg) * v
    return (yn + bonus) * g, s_new


def _rwkv_kernel(rkv_ref, misc_ref, sh_rkv_ref, sh_misc_ref, s0_ref,
                 mu_rkv_ref, mu_misc_ref, w0_ref, w2_ref, a0_ref, a2_ref, g2_ref,
                 kk_ref, ka_ref, rk_ref, lnw_ref, lnb_ref,
                 y_ref, s_ref,
                 carry_rkv, carry_misc, r_s, k_s, v_s, a_s, lw_s, g_s,
                 *, rows, lead, width):
    L = CHUNK
    c = pl.program_id(1)

    @pl.when(c == 0)
    def _():
        carry_rkv[...] = sh_rkv_ref[0]
        carry_misc[...] = sh_misc_ref[0]
        s_ref[...] = s0_ref[...]

    p = rkv_ref[...]
    pm = misc_ref[:, :3 * LANE]
    if rows < L:
        p = jnp.concatenate([jnp.zeros((L - rows, p.shape[1]), F32), p], axis=0)
        pm = jnp.concatenate([jnp.zeros((L - rows, pm.shape[1]), F32), pm], axis=0)
    row = _iota((L, 1), 0)
    first = jnp.where(c == 0, lead, 0)
    valid = row >= first
    p = jnp.where(valid, p, 0.0)
    pm = jnp.where(valid, pm, 0.0)

    def shift_mix(cur, carry_ref, mu):
        prev = jnp.where(row == first, carry_ref[...], pltpu.roll(cur, 1, axis=0))
        carry_ref[...] = cur[L - 1:L, :]
        return jnp.where(valid, cur + mu * (prev - cur), 0.0)

    u = shift_mix(p, carry_rkv, mu_rkv_ref[...])
    um = shift_mix(pm, carry_misc, mu_misc_ref[...])

    lora = um[:, :LANE]
    wl = w0_ref[...] + _dot(jnp.tanh(lora).astype(BF16), w2_ref[...])
    wlog = -_softplus(-wl) - 0.5
    lw_s[...] = jnp.where(valid, -jnp.exp(wlog), 0.0)
    a_s[...] = _sigmoid(a0_ref[...] + _dot(lora.astype(BF16), a2_ref[...]))
    g_s[...] = _dot(_sigmoid(um[:, LANE:]).astype(BF16), g2_ref[...])
    r_s[...] = u[:, :width]
    k_s[...] = u[:, width:2 * width]
    v_s[...] = u[:, 2 * width:]

    lane = _iota((1, LANE), 1)
    half = LANE // 2
    col2 = _iota((L, 2 * L), 1)
    t_i = _iota((L, 2 * L), 0)
    s_i = col2 & (L - 1)
    cst = (
        (_iota((L, L), 1) <= _iota((L, L), 0)).astype(BF16),
        ((_iota((LANE, LANE), 0) < half) == (_iota((LANE, LANE), 1) < half)).astype(BF16),
        lane < half,
        s_i < t_i, s_i <= t_i,
        col2 < L,
        (_iota((2 * L, 2 * L), 0) == _iota((2 * L, 2 * L), 1)).astype(F32),
        (_iota((LANE, LANE), 0) < half) == (_iota((LANE, LANE), 1) < half),
    )
    for j in range(width // LANE):
        sl = slice(j * LANE, (j + 1) * LANE)
        out, s_new = _rwkv_pair(r_s[:, sl], k_s[:, sl], v_s[:, sl], a_s[:, sl], lw_s[:, sl], g_s[:, sl],
                                s_ref[0, j], kk_ref[:, sl], ka_ref[:, sl], rk_ref[:, sl],
                                lnw_ref[:, sl], lnb_ref[:, sl], cst)
        s_ref[0, j] = s_new
        y_ref[:, sl] = out[L - rows:, :].astype(y_ref.dtype)


def _rwkv(proj, shift_rkv, shift_misc, s0, prm, *, batch, n_chunks, rows, row0, lead, width, d_model):
    n = proj.shape[0]
    n_pairs = width // LANE
    rkv_blk = (2 * d_model + 2 * width) // (3 * width)
    misc_blk = (proj.shape[1] - MISC_COLS) // MISC_COLS
    blk0 = row0 // rows
    kern = functools.partial(_rwkv_kernel, rows=rows, lead=lead, width=width)
    full = lambda shape: pl.BlockSpec(shape, lambda b, c: (0,) * len(shape))
    seq_row = lambda b, c: blk0 + b * n_chunks + c
    in_specs = [
        pl.BlockSpec((rows, 3 * width), lambda b, c: (seq_row(b, c), rkv_blk)),
        pl.BlockSpec((rows, MISC_COLS), lambda b, c: (seq_row(b, c), misc_blk)),
        pl.BlockSpec((1, 1, 3 * width), lambda b, c: (b, 0, 0)),
        pl.BlockSpec((1, 1, 3 * LANE), lambda b, c: (b, 0, 0)),
        pl.BlockSpec((1, n_pairs, LANE, LANE), lambda b, c: (b, 0, 0, 0)),
        full((1, 3 * width)), full((1, 3 * LANE)),
        full((1, width)), full((LANE, width)), full((1, width)), full((LANE, width)), full((2 * LANE, width)),
        full((1, width)), full((1, width)), full((1, width)), full((1, width)), full((1, width)),
    ]
    out_specs = [
        pl.BlockSpec((rows, width), lambda b, c: (b * n_chunks + c, 0)),
        pl.BlockSpec((1, n_pairs, LANE, LANE), lambda b, c: (b, 0, 0, 0)),
    ]
    out_shape = [
        jax.ShapeDtypeStruct((batch * n_chunks * rows, width), BF16),
        jax.ShapeDtypeStruct((batch, n_pairs, LANE, LANE), F32),
    ]
    scratch = [pltpu.VMEM((1, 3 * width), F32), pltpu.VMEM((1, 3 * LANE), F32)]
    scratch += [pltpu.VMEM((CHUNK, width), F32) for _ in range(6)]
    return pl.pallas_call(
        kern, grid=(batch, n_chunks), in_specs=in_specs, out_specs=out_specs, out_shape=out_shape,
        scratch_shapes=scratch, compiler_params=_params("parallel", "arbitrary"), name="rwkv7_chunk",
    )(proj, proj, shift_rkv, shift_misc, s0, *prm)


def _mlstm_kernel(qk_ref, v_ref, o_ref, misc_ref, conv0_ref, c0_ref, n0_ref, m0_ref,
                  cw_ref, cb_ref, ifb_ref, nw_ref,
                  y_ref, c_ref, n_ref, m_ref,
                  ext, *, rows, lead, heads, hd):
    L = CHUNK
    c = pl.program_id(1)
    width = heads * hd

    @pl.when(c == 0)
    def _():
        c_ref[...] = c0_ref[...]
        n_ref[...] = n0_ref[...]
        m_ref[...] = m0_ref[...]
        ext[0:SUBLANE, :] = jnp.zeros((SUBLANE, 2 * width), F32)

    def chunk_rows(x):
        if rows < L:
            return jnp.concatenate([jnp.zeros((L - rows, x.shape[1]), F32), x], axis=0)
        return x

    row = _iota((L, 1), 0)
    valid = row >= jnp.where(c == 0, lead, 0)

    ext[SUBLANE:SUBLANE + L, :] = jnp.where(valid, chunk_rows(qk_ref[...]), 0.0)

    @pl.when(c == 0)
    def _():
        ext[lead:lead + SUBLANE, :] = conv0_ref[0]

    conv = cb_ref[...]
    for j in range(4):
        conv = conv + cw_ref[j:j + 1, :] * ext[SUBLANE - 3 + j:SUBLANE - 3 + j + L, :]
    ext[0:SUBLANE, :] = ext[L:L + SUBLANE, :]
    qk = jnp.where(valid, conv * _sigmoid(conv), 0.0)
    v = jnp.where(valid, chunk_rows(v_ref[...]), 0.0)
    og = _sigmoid(chunk_rows(o_ref[...]))

    lane = _iota((1, LANE), 1)
    is_i = lane < heads
    z = chunk_rows(misc_ref[:, MISC_IF:MISC_IF + LANE]) + ifb_ref[...]
    logsig = jnp.minimum(z, 0.0) - jnp.log1p(jnp.exp(-jnp.abs(z)))
    gates = jnp.where(valid, jnp.where(is_i, z, logsig), jnp.where(is_i, NEG, 0.0))
    tri = (_iota((L, L), 1) <= _iota((L, L), 0)).astype(BF16)
    cum = _cumsum_rows(tri, jnp.where(is_i, 0.0, gates))
    pad = jnp.zeros((LANE - L, LANE), F32)
    gates_t = jnp.concatenate([gates, pad], axis=0).T
    cum_t = jnp.concatenate([cum, pad], axis=0).T
    causal = _iota((L, L), 1) <= _iota((L, L), 0)
    m_all = m_ref[0]

    for h in range(heads):
        sl = slice(h * hd, (h + 1) * hd)
        q = qk[:, sl]
        kx = qk[:, width + h * hd:width + (h + 1) * hd] * (hd ** -0.5)
        vh = v[:, sl]
        b_col = cum[:, heads + h:heads + h + 1]
        i_col = gates[:, h:h + 1]
        b_row = cum_t[heads + h:heads + h + 1, :L]
        i_row = gates_t[h:h + 1, :L]
        b_end = b_col[L - 1:L, :]
        m_prev = jnp.sum(jnp.where(lane == h, m_all, 0.0), axis=1, keepdims=True)

        log_inter = b_col + m_prev
        dmat = jnp.where(causal, b_col - b_row + i_row, NEG)
        m_q = jnp.maximum(log_inter, jnp.max(dmat, axis=-1, keepdims=True))
        w_inter = jnp.exp(log_inter - m_q)
        qb = q.astype(BF16)
        s = _dot_nt(qb, kx.astype(BF16)) * jnp.exp(dmat - m_q)
        c_h = c_ref[0, h]
        n_h = n_ref[0, h:h + 1, :]
        num = w_inter * _dot(qb, c_h.astype(BF16)) + _dot(s.astype(BF16), vh.astype(BF16))
        den = w_inter * jnp.sum(q * n_h, axis=-1, keepdims=True) + jnp.sum(s, axis=-1, keepdims=True)
        hcell = num / jnp.maximum(jnp.abs(den), jnp.exp(-m_q))

        g_col = b_end - b_col + i_col
        m_new = jnp.maximum(b_end + m_prev, jnp.max(g_col, axis=0, keepdims=True))
        a_st = jnp.exp(b_end + m_prev - m_new)
        wkk = jnp.exp(g_col - m_new) * kx
        c_ref[0, h] = a_st * c_h + _dot_tn(wkk.astype(BF16), vh.astype(BF16))
        n_ref[0, h:h + 1, :] = a_st * n_h + jnp.sum(wkk, axis=0, keepdims=True)
        m_all = jnp.where(lane == h, m_new, m_all)

        mu = jnp.mean(hcell, axis=-1, keepdims=True)
        hc = hcell - mu
        var = jnp.mean(hc * hc, axis=-1, keepdims=True)
        yh = hc * lax.rsqrt(var + ML_LN_EPS) * nw_ref[:, sl] * og[:, sl]
        y_ref[:, sl] = yh[L - rows:, :].astype(y_ref.dtype)

    m_ref[0] = m_all


def _mlstm(proj, conv0, c0, n0, m0, prm, *, batch, n_chunks, rows, row0, lead, heads, hd, d_model):
    n = proj.shape[0]
    width = heads * hd
    blk0 = row0 // rows
    seq_row = lambda b, c: blk0 + b * n_chunks + c
    qk_blk = (2 * d_model) // (2 * width)
    v_blk = (2 * d_model + 2 * width + 3 * (d_model - width)) // width
    misc_blk = (proj.shape[1] - MISC_COLS) // MISC_COLS
    kern = functools.partial(_mlstm_kernel, rows=rows, lead=lead, heads=heads, hd=hd)
    full = lambda shape: pl.BlockSpec(shape, lambda b, c: (0,) * len(shape))
    in_specs = [
        pl.BlockSpec((rows, 2 * width), lambda b, c: (seq_row(b, c), qk_blk)),
        pl.BlockSpec((rows, width), lambda b, c: (seq_row(b, c), v_blk)),
        pl.BlockSpec((rows, width), lambda b, c: (seq_row(b, c), v_blk + 1)),
        pl.BlockSpec((rows, MISC_COLS), lambda b, c: (seq_row(b, c), misc_blk)),
        pl.BlockSpec((1, SUBLANE, 2 * width), lambda b, c: (b, 0, 0)),
        pl.BlockSpec((1, heads, hd, hd), lambda b, c: (b, 0, 0, 0)),
        pl.BlockSpec((1, heads, hd), lambda b, c: (b, 0, 0)),
        pl.BlockSpec((1, 1, LANE), lambda b, c: (b, 0, 0)),
        full((4, 2 * width)), full((1, 2 * width)), full((1, LANE)), full((1, width)),
    ]
    out_specs = [
        pl.BlockSpec((rows, width), lambda b, c: (b * n_chunks + c, 0)),
        pl.BlockSpec((1, heads, hd, hd), lambda b, c: (b, 0, 0, 0)),
        pl.BlockSpec((1, heads, hd), lambda b, c: (b, 0, 0)),
        pl.BlockSpec((1, 1, LANE), lambda b, c: (b, 0, 0)),
    ]
    out_shape = [
        jax.ShapeDtypeStruct((batch * n_chunks * rows, width), BF16),
        jax.ShapeDtypeStruct((batch, heads, hd, hd), F32),
        jax.ShapeDtypeStruct((batch, heads, hd), F32),
        jax.ShapeDtypeStruct((batch, 1, LANE), F32),
    ]
    return pl.pallas_call(
        kern, grid=(batch, n_chunks), in_specs=in_specs, out_specs=out_specs, out_shape=out_shape,
        scratch_shapes=[pltpu.VMEM((CHUNK + 2 * SUBLANE, 2 * width), F32)],
        compiler_params=_params("parallel", "arbitrary"), name="mlstm_chunk",
    )(proj, proj, proj, proj, conv0, c0, n0, m0, *prm)


def _pad_rows(w, rows):
    return jnp.pad(w, ((0, rows - w.shape[0]), (0, 0)))


def _swiglu_weights(w_gate, w_up, w_down):
    d, f = w_gate.shape
    fp = -(-f // FF_TILE) * FF_TILE
    wg = jnp.pad(w_gate, ((0, 0), (0, fp - f))).astype(BF16).reshape(d, fp // FF_TILE, 1, FF_TILE)
    wu = jnp.pad(w_up, ((0, 0), (0, fp - f))).astype(BF16).reshape(d, fp // FF_TILE, 1, FF_TILE)
    wgu = jnp.concatenate([wg, wu], axis=2).reshape(d, 2 * fp)
    return wgu, _pad_rows(w_down, fp).astype(BF16)


def kernel(x_prompt, x_sample, state_rwkv_shift, state_rwkv_wkv, state_mlstm_conv, state_mlstm_C,
           state_mlstm_n, state_mlstm_m, meta_tokens, ffn1_norm, ffn1_w_gate, ffn1_w_up, ffn1_w_down,
           mix_norm, w_in, rw_mu, rw_w0, rw_w2, rw_a0, rw_a2, rw_g2, rw_kk, rw_ka, rw_rk, rw_ln_w, rw_ln_b,
           ml_conv_w, ml_conv_b, ml_i_b, ml_f_b, ml_norm_w, w_br_rw, w_br_ml, w_out,
           ffn2_norm, ffn2_w_gate, ffn2_w_up, ffn2_w_down, final_norm):
    assert ffn1_norm.shape[0] == 1, "single-layer trunk"
    B, T, D = x_prompt.shape
    Bs, Ts, _ = x_sample.shape
    n_meta = meta_tokens.shape[0]
    W = rw_w0.shape[-1]
    dl, al, gl = rw_w2.shape[1], rw_a2.shape[1], rw_g2.shape[1]
    rw_heads, rw_hd = rw_rk.shape[1], rw_rk.shape[2]
    Wm = ml_norm_w.shape[-1]
    H = ml_i_b.shape[-1]
    hd = Wm // H
    K = ml_conv_w.shape[1]
    L = CHUNK
    assert rw_hd == LANE // 2 and dl == LANE // 2 and al == LANE // 2 and gl <= 2 * LANE
    assert K == 4 and hd % LANE == 0 and 2 * H <= LANE and W == Wm and 2 * W == D
    assert Ts <= L and Ts % (2 * SUBLANE) == 0 and D % COL_TILE == 0 and (11 * D // 2) % MISC_COLS == 0

    lead = (-n_meta) % L
    Tp = lead + n_meta + T
    n_chunks = Tp // L
    lead_s = L - Ts
    xp = jnp.concatenate([jnp.zeros((B, lead, D), F32),
                          jnp.broadcast_to(meta_tokens.astype(F32)[None], (B, n_meta, D)),
                          x_prompt], axis=1).reshape(B * Tp, D)
    n_real = B * Tp + Bs * Ts
    N = -(-n_real // ROW_TILE) * ROW_TILE
    x = jnp.concatenate([xp, x_sample.reshape(Bs * Ts, D), jnp.zeros((N - n_real, D), F32)], axis=0)

    wgu1, wd1 = _swiglu_weights(ffn1_w_gate[0], ffn1_w_up[0], ffn1_w_down[0])
    wgu2, wd2 = _swiglu_weights(ffn2_w_gate[0], ffn2_w_up[0], ffn2_w_down[0])
    wi = w_in[0]
    o_lora = 3 * W
    o_qk = o_lora + dl + al + gl
    o_v = o_qk + 2 * Wm
    o_o = o_v + Wm
    o_i = o_o + Wm
    o_gate = o_i + 2 * H
    zc = lambda n: jnp.zeros((D, n), wi.dtype)
    misc_w = jnp.concatenate([wi[:, o_lora:o_qk], zc(MISC_IF - (dl + al + gl)),
                              wi[:, o_i:o_gate], zc(MISC_COLS - MISC_IF - 2 * H)], axis=1)
    w_proj = jnp.concatenate([wi[:, o_gate:], wi[:, o_qk:o_v], wi[:, :o_lora], wi[:, o_v:o_i], misc_w],
                             axis=1).astype(BF16)

    def misc_vec(v_lora, fill=0.0):
        return jnp.pad(v_lora, [(0, 0)] * (v_lora.ndim - 1) + [(0, 3 * LANE - v_lora.shape[-1])],
                       constant_values=fill)

    row = lambda v: v.reshape(1, -1).astype(F32)
    rw_prm = (
        row(rw_mu[0, :o_lora]), misc_vec(row(rw_mu[0, o_lora:])),
        row(rw_w0[0]), _pad_rows(rw_w2[0], LANE).astype(BF16),
        row(rw_a0[0]), jnp.concatenate([jnp.zeros((dl, W), F32), rw_a2[0]], axis=0).astype(BF16),
        _pad_rows(rw_g2[0], 2 * LANE).astype(BF16),
        row(rw_kk[0]), row(rw_ka[0]), row(rw_rk[0]), row(rw_ln_w[0]), row(rw_ln_b[0]),
    )
    ifb = jnp.pad(jnp.concatenate([ml_i_b[0], ml_f_b[0]]).reshape(1, 2 * H), ((0, 0), (0, LANE - 2 * H)))
    ml_prm = (ml_conv_w[0].astype(F32), row(ml_conv_b[0]), ifb.astype(F32), row(ml_norm_w[0]))

    def rw_state_in(shift, wkv):
        b = shift.shape[0]
        s = wkv.reshape(b, rw_heads // 2, 2, rw_hd, rw_hd)
        z = jnp.zeros_like(s[:, :, 0])
        bd = jnp.concatenate([jnp.concatenate([s[:, :, 0], z], axis=-1),
                              jnp.concatenate([z, s[:, :, 1]], axis=-1)], axis=-2)
        return shift[:, None, :o_lora], misc_vec(shift[:, None, o_lora:]), bd

    def rw_state_out(bd):
        h = rw_hd
        return jnp.stack([bd[:, :, :h, :h], bd[:, :, h:, h:]], axis=2).reshape(bd.shape[0], rw_heads, h, h)

    def conv_in(buf):
        return jnp.pad(buf, ((0, 0), (SUBLANE - (K - 1), 0), (0, 0)))

    def m_in(m):
        return jnp.pad(m, ((0, 0), (0, LANE - H)))[:, None, :]

    x1 = _ffn(x, row(ffn1_norm[0]), wgu1, wd1, row(final_norm), final_norm=False)
    proj = _proj(x1, row(mix_norm[0]), w_proj)

    zeros = lambda *s: jnp.zeros(s, F32)
    seqs = (
        dict(batch=B, n_chunks=n_chunks, rows=L, row0=0, lead=lead),
        dict(batch=Bs, n_chunks=1, rows=Ts, row0=B * Tp, lead=lead_s),
    )
    rw_states = (
        rw_state_in(zeros(B, o_qk), zeros(B, rw_heads, rw_hd, rw_hd)),
        rw_state_in(state_rwkv_shift[0], state_rwkv_wkv[0]),
    )
    ml_states = (
        (conv_in(zeros(B, K - 1, 2 * Wm)), zeros(B, H, hd, hd), zeros(B, H, hd), m_in(zeros(B, H))),
        (conv_in(state_mlstm_conv[0]), state_mlstm_C[0], state_mlstm_n[0], m_in(state_mlstm_m[0])),
    )
    y_rw_parts, y_ml_parts, rw_out, ml_out = [], [], [], []
    for seq, rws, mls in zip(seqs, rw_states, ml_states):
        y_rw, s_end = _rwkv(proj, *rws, rw_prm, width=W, d_model=D, **seq)
        y_ml, c_end, n_end, m_end = _mlstm(proj, *mls, ml_prm, heads=H, hd=hd, d_model=D, **seq)
        y_rw_parts.append(y_rw)
        y_ml_parts.append(y_ml)
        rw_out.append(rw_state_out(s_end))
        ml_out.append((c_end, n_end, m_end[:, 0, :H]))

    split = B * Tp
    row_pad = [jnp.zeros((N - n_real, W), BF16)]
    y_rw = jnp.concatenate(y_rw_parts + row_pad, axis=0)
    y_ml = jnp.concatenate(y_ml_parts + row_pad, axis=0)

    x2 = _merge(x1, y_rw, y_ml, proj, w_br_rw[0].astype(BF16), w_br_ml[0].astype(BF16), w_out[0].astype(BF16))
    y = _ffn(x2, row(ffn2_norm[0]), wgu2, wd2, row(final_norm), final_norm=True)

    c_rkv = 2 * D + 2 * Wm
    c_misc = proj.shape[1] - MISC_COLS

    def seq_states(pseq):
        shift = jnp.concatenate([pseq[:, -1, c_rkv:c_rkv + 3 * W],
                                 pseq[:, -1, c_misc:c_misc + dl + al + gl]], axis=-1)
        return shift, pseq[:, -(K - 1):, 2 * D:2 * D + 2 * Wm]

    p_shift, p_conv = seq_states(proj[:split].reshape(B, Tp, -1))
    s_shift, s_conv = seq_states(proj[split:n_real].reshape(Bs, Ts, -1))
    y_prompt = y[:split].reshape(B, Tp, D)[:, lead + n_meta:]
    y_sample = y[split:n_real].reshape(Bs, Ts, D)
    d1 = lambda a: a[None]
    return (y_prompt, y_sample,
            d1(p_shift), d1(rw_out[0]), d1(p_conv), d1(ml_out[0][0]), d1(ml_out[0][1]), d1(ml_out[0][2]),
            d1(s_shift), d1(rw_out[1]), d1(s_conv), d1(ml_out[1][0]), d1(ml_out[1][1]), d1(ml_out[1][2]))
```

```python
import functools

import jax
import jax.numpy as jnp
from jax import lax
from jax.experimental import pallas as pl
from jax.experimental.pallas import tpu as pltpu

F32 = jnp.float32
BF16 = jnp.bfloat16

LANE = 128
SUBLANE = 8
VMEM_LIMIT_BYTES = 56 * 1024 * 1024
CHUNK = 64
ROW_TILE = 512
PROJ_ROW_TILE = 1024
FF_TILE = 512
COL_TILE = 512
MISC_COLS = 512
MISC_IF = 384
RMS_EPS = 1e-6
RW_GN_EPS = 64e-5
ML_LN_EPS = 1e-5
NEG = -1e30


def _dot(a, b):
    return jnp.dot(a, b, preferred_element_type=F32)


def _dot_nt(a, b):
    return lax.dot_general(a, b, (((1,), (1,)), ((), ())), preferred_element_type=F32)


def _dot_tn(a, b):
    return lax.dot_general(a, b, (((0,), (0,)), ((), ())), preferred_element_type=F32)


def _split3(x):
    h1 = x.astype(BF16)
    r1 = x - h1.astype(F32)
    h2 = r1.astype(BF16)
    r2 = r1 - h2.astype(F32)
    return h1, h2, r2.astype(BF16)


def _cumsum_rows(tri, x):
    n = x.shape[1]
    y = _dot(tri, jnp.concatenate(_split3(x), axis=1))
    return y[:, :n] + y[:, n:2 * n] + y[:, 2 * n:]


def _segsum(x, seg):
    rows = x.shape[0]
    y = _dot(jnp.concatenate(_split3(x), axis=0), seg)
    return y[:rows] + y[rows:2 * rows] + y[2 * rows:]


def _sigmoid(z):
    return jax.nn.sigmoid(z)


def _softplus(z):
    return jnp.maximum(z, 0.0) + jnp.log1p(jnp.exp(-jnp.abs(z)))


def _rmsnorm(x, g):
    return x * lax.rsqrt(jnp.mean(x * x, axis=-1, keepdims=True) + RMS_EPS) * g


def _iota(shape, dim):
    return lax.broadcasted_iota(jnp.int32, shape, dim)


def _params(*semantics):
    return pltpu.CompilerParams(dimension_semantics=semantics, vmem_limit_bytes=VMEM_LIMIT_BYTES)


def _ffn_kernel(x_ref, g_ref, wgu_ref, wd_ref, fg_ref, o_ref, xn_ref, *, tf, n_f, final_norm):
    j = pl.program_id(1)

    @pl.when(j == 0)
    def _():
        xn_ref[...] = _rmsnorm(x_ref[...], g_ref[...]).astype(BF16)
        o_ref[...] = jnp.zeros_like(o_ref)

    gu = _dot(xn_ref[...], wgu_ref[...])
    gate = gu[:, :tf]
    h = (gate * _sigmoid(gate) * gu[:, tf:]).astype(BF16)
    o_ref[...] += _dot(h, wd_ref[...])

    @pl.when(j == n_f - 1)
    def _():
        y = x_ref[...] + 0.5 * o_ref[...]
        if final_norm:
            y = _rmsnorm(y, fg_ref[...])
        o_ref[...] = y


def _ffn(x, norm_g, wgu, wd, final_g, *, final_norm):
    n, d = x.shape
    n_f = wd.shape[0] // FF_TILE
    kern = functools.partial(_ffn_kernel, tf=FF_TILE, n_f=n_f, final_norm=final_norm)
    return pl.pallas_call(
        kern,
        grid=(n // ROW_TILE, n_f),
        in_specs=[
            pl.BlockSpec((ROW_TILE, d), lambda i, j: (i, 0)),
            pl.BlockSpec((1, d), lambda i, j: (0, 0)),
            pl.BlockSpec((d, 2 * FF_TILE), lambda i, j: (0, j)),
            pl.BlockSpec((FF_TILE, d), lambda i, j: (j, 0)),
            pl.BlockSpec((1, d), lambda i, j: (0, 0)),
        ],
        out_specs=pl.BlockSpec((ROW_TILE, d), lambda i, j: (i, 0)),
        out_shape=jax.ShapeDtypeStruct((n, d), F32),
        scratch_shapes=[pltpu.VMEM((ROW_TILE, d), BF16)],
        compiler_params=_params("parallel", "arbitrary"),
        name="ffn_final" if final_norm else "ffn",
    )(x, norm_g, wgu, wd, final_g)


def _proj_kernel(x_ref, g_ref, w_ref, o_ref, xn_ref):
    @pl.when(pl.program_id(1) == 0)
    def _():
        xn_ref[...] = _rmsnorm(x_ref[...], g_ref[...]).astype(BF16)

    o_ref[...] = _dot(xn_ref[...], w_ref[...])


def _proj(x, norm_g, w):
    n, d = x.shape
    cols = w.shape[1]
    tm = PROJ_ROW_TILE
    return pl.pallas_call(
        _proj_kernel,
        grid=(n // tm, cols // COL_TILE),
        in_specs=[
            pl.BlockSpec((tm, d), lambda i, j: (i, 0)),
            pl.BlockSpec((1, d), lambda i, j: (0, 0)),
            pl.BlockSpec((d, COL_TILE), lambda i, j: (0, j)),
        ],
        out_specs=pl.BlockSpec((tm, COL_TILE), lambda i, j: (i, j)),
        out_shape=jax.ShapeDtypeStruct((n, cols), F32),
        scratch_shapes=[pltpu.VMEM((tm, d), BF16)],
        compiler_params=_params("parallel", "arbitrary"),
        name="in_proj",
    )(x, norm_g, w)


def _merge_kernel(x_ref, yrw_ref, yml_ref, ga_ref, gb_ref, wrw_ref, wml_ref, wo_ref, o_ref, *, n_j):
    j = pl.program_id(1)

    @pl.when(j == 0)
    def _():
        o_ref[...] = jnp.zeros_like(o_ref)

    merged = (_sigmoid(ga_ref[...]) * _dot(yrw_ref[...], wrw_ref[...])
              + _sigmoid(gb_ref[...]) * _dot(yml_ref[...], wml_ref[...]))
    o_ref[...] += _dot(merged.astype(BF16), wo_ref[...])

    @pl.when(j == n_j - 1)
    def _():
        o_ref[...] = x_ref[...] + o_ref[...]


def _merge(x, y_rw, y_ml, proj, w_rw, w_ml, w_out):
    n, d = x.shape
    w = y_rw.shape[1]
    n_j = d // COL_TILE
    kern = functools.partial(_merge_kernel, n_j=n_j)
    return pl.pallas_call(
        kern,
        grid=(n // ROW_TILE, n_j),
        in_specs=[
            pl.BlockSpec((ROW_TILE, d), lambda i, j: (i, 0)),
            pl.BlockSpec((ROW_TILE, w), lambda i, j: (i, 0)),
            pl.BlockSpec((ROW_TILE, w), lambda i, j: (i, 0)),
            pl.BlockSpec((ROW_TILE, COL_TILE), lambda i, j: (i, j)),
            pl.BlockSpec((ROW_TILE, COL_TILE), lambda i, j: (i, n_j + j)),
            pl.BlockSpec((w, COL_TILE), lambda i, j: (0, j)),
            pl.BlockSpec((w, COL_TILE), lambda i, j: (0, j)),
            pl.BlockSpec((COL_TILE, d), lambda i, j: (j, 0)),
        ],
        out_specs=pl.BlockSpec((ROW_TILE, d), lambda i, j: (i, 0)),
        out_shape=jax.ShapeDtypeStruct((n, d), F32),
        compiler_params=_params("parallel", "arbitrary"),
        name="merge",
    )(x, y_rw, y_ml, proj, proj, w_rw, w_ml, w_out)


def _each(f, *lists):
    return [f(*xs) for xs in zip(*lists)]


def _rwkv_pairs(r, k, v, a, lw, g, s0, kkw, kaw, rkw, lnw, lnb, cst):
    L = r[0].shape[0]
    tri, seg, head0, strict, incl, col_head0, eye2, blockdiag = cst
    bf = lambda x: x.astype(BF16)

    kk_ = _each(lambda k_, w_: k_ * w_, k, kkw)
    kn = _each(lambda x: _segsum(x * x, seg), kk_)
    cum = _each(lambda x: _cumsum_rows(tri, x), lw)
    kk = _each(lambda x, n_: x / jnp.maximum(jnp.sqrt(n_), 1e-12), kk_, kn)
    k2 = _each(lambda k_, a_, w_: k_ * (1.0 + (a_ - 1.0) * w_), k, a, kaw)
    kka = _each(lambda x, a_: x * a_, kk, a)
    c_end = [c[L - 1:L, :] for c in cum]
    e_inv = [jnp.exp(-c) for c in cum]
    e_rem = _each(lambda c, ce: jnp.exp(ce - c), cum, c_end)
    at = _each(lambda x, c, l_: bf(-x * jnp.exp(c - l_)), kk, cum, lw)
    rt = _each(lambda x, c: bf(x * jnp.exp(c)), r, cum)
    bt = _each(lambda x, e: bf(x * e), kka, e_inv)
    kt = _each(lambda x, e: bf(x * e), k2, e_inv)
    vb = [bf(x) for x in v]
    zero = jnp.zeros_like(vb[0])

    def by_head(x):
        return jnp.concatenate([jnp.where(head0, x, zero), jnp.where(head0, zero, x)], axis=0)

    lhs = _each(lambda x, y_: jnp.concatenate([x, y_], axis=0), at, rt)
    ab = _each(lambda l_, x: _dot_nt(l_, by_head(x)), lhs, bt)
    ak = _each(lambda l_, x: _dot_nt(l_, by_head(x)), lhs, kt)
    a_rb = [jnp.where(incl, x[L:], 0.0) for x in ab]
    a_ak = [jnp.where(strict, x[:L], 0.0) for x in ak]
    a_rk = [jnp.where(incl, x[L:], 0.0) for x in ak]

    def blockdiag2(x):
        n_cat = jnp.where(strict, x[:L], 0.0)
        return jnp.concatenate([jnp.where(col_head0, n_cat, 0.0), jnp.where(col_head0, 0.0, n_cat)], axis=0)

    p = [blockdiag2(x) for x in ab]
    t_inv = [eye2 + x for x in p]
    for _ in range(L.bit_length() - 2):
        p = [_dot(bf(x), bf(x)) for x in p]
        t_inv = _each(lambda t_, x: t_ + _dot(bf(t_), bf(x)), t_inv, p)

    sb = [bf(x) for x in s0]
    vs = [by_head(x) for x in vb]
    x0 = _each(lambda a_, s_, m_, v_: _dot_nt(a_, s_) + _dot(bf(m_), v_), at, sb, a_ak, vs)
    us = _each(lambda t_, x: _dot(bf(t_), by_head(bf(x))), t_inv, x0)
    y = _each(lambda r_, s_, m1, m2, u_, v_: _dot_nt(r_, s_) + _dot(
        bf(jnp.concatenate([m1, m2], axis=1)), jnp.concatenate([bf(u_), v_], axis=0)),
        rt, sb, a_rb, a_rk, us, vs)
    upd = _each(lambda u_, v_, x1, x2, e: _dot_tn(
        jnp.concatenate([bf(u_[:L] + u_[L:]), v_], axis=0),
        jnp.concatenate([bf(x1 * e), bf(x2 * e)], axis=0)), us, vb, kka, k2, e_rem)
    s_new = _each(lambda s_, ce, u_: s_ * jnp.exp(ce) + jnp.where(blockdiag, u_, 0.0), s0, c_end, upd)

    hd = float(LANE // 2)
    mean = [_segsum(x, seg) / hd for x in y]
    bsum = _each(lambda r_, k_, w_: _segsum(r_ * k_ * w_, seg), r, k2, rkw)
    yc = _each(lambda x, m_: x - m_, y, mean)
    var = [_segsum(x * x, seg) / hd for x in yc]
    out = _each(lambda x, v_, w_, b_, bs, vv, g_: (x * lax.rsqrt(v_ + RW_GN_EPS) * w_ + b_ + bs * vv) * g_,
                yc, var, lnw, lnb, bsum, v, g)
    return out, s_new


def _rwkv_kernel(rkv_ref, misc_ref, sh_rkv_ref, sh_misc_ref, s0_ref,
                 mu_rkv_ref, mu_misc_ref, w0_ref, w2_ref, a0_ref, a2_ref, g2_ref,
                 kk_ref, ka_ref, rk_ref, lnw_ref, lnb_ref,
                 y_ref, s_ref,
                 carry_rkv, carry_misc, r_s, k_s, v_s, a_s, lw_s, g_s,
                 *, rows, lead, width):
    L = CHUNK
    c = pl.program_id(1)

    @pl.when(c == 0)
    def _():
        carry_rkv[...] = sh_rkv_ref[0]
        carry_misc[...] = sh_misc_ref[0]
        s_ref[...] = s0_ref[...]

    p = rkv_ref[...]
    pm = misc_ref[:, :3 * LANE]
    if rows < L:
        p = jnp.concatenate([jnp.zeros((L - rows, p.shape[1]), F32), p], axis=0)
        pm = jnp.concatenate([jnp.zeros((L - rows, pm.shape[1]), F32), pm], axis=0)
    row = _iota((L, 1), 0)
    first = jnp.where(c == 0, lead, 0)
    valid = row >= first
    p = jnp.where(valid, p, 0.0)
    pm = jnp.where(valid, pm, 0.0)

    def shift_mix(cur, carry_ref, mu):
        prev = jnp.where(row == first, carry_ref[...], pltpu.roll(cur, 1, axis=0))
        carry_ref[...] = cur[L - 1:L, :]
        return jnp.where(valid, cur + mu * (prev - cur), 0.0)

    u = shift_mix(p, carry_rkv, mu_rkv_ref[...])
    um = shift_mix(pm, carry_misc, mu_misc_ref[...])

    lora = um[:, :LANE]
    wl = w0_ref[...] + _dot(jnp.tanh(lora).astype(BF16), w2_ref[...])
    wlog = -_softplus(-wl) - 0.5
    lw_s[...] = jnp.where(valid, -jnp.exp(wlog), 0.0)
    a_s[...] = _sigmoid(a0_ref[...] + _dot(lora.astype(BF16), a2_ref[...]))
    g_s[...] = _dot(_sigmoid(um[:, LANE:]).astype(BF16), g2_ref[...])
    r_s[...] = u[:, :width]
    k_s[...] = u[:, width:2 * width]
    v_s[...] = u[:, 2 * width:]

    lane = _iota((1, LANE), 1)
    half = LANE // 2
    col2 = _iota((L, 2 * L), 1)
    t_i = _iota((L, 2 * L), 0)
    s_i = col2 & (L - 1)
    cst = (
        (_iota((L, L), 1) <= _iota((L, L), 0)).astype(BF16),
        ((_iota((LANE, LANE), 0) < half) == (_iota((LANE, LANE), 1) < half)).astype(BF16),
        lane < half,
        s_i < t_i, s_i <= t_i,
        col2 < L,
        (_iota((2 * L, 2 * L), 0) == _iota((2 * L, 2 * L), 1)).astype(F32),
        (_iota((LANE, LANE), 0) < half) == (_iota((LANE, LANE), 1) < half),
    )
    sls = [slice(j * LANE, (j + 1) * LANE) for j in range(width // LANE)]
    pick = lambda ref: [ref[:, sl] for sl in sls]
    outs, s_new = _rwkv_pairs(pick(r_s), pick(k_s), pick(v_s), pick(a_s), pick(lw_s), pick(g_s),
                              [s_ref[0, j] for j in range(len(sls))],
                              pick(kk_ref), pick(ka_ref), pick(rk_ref), pick(lnw_ref), pick(lnb_ref), cst)
    for j, sl in enumerate(sls):
        s_ref[0, j] = s_new[j]
        y_ref[:, sl] = outs[j][L - rows:, :].astype(y_ref.dtype)


def _rwkv(proj, shift_rkv, shift_misc, s0, prm, *, batch, n_chunks, rows, row0, lead, width, d_model):
    n = proj.shape[0]
    n_pairs = width // LANE
    rkv_blk = (2 * d_model + 2 * width) // (3 * width)
    misc_blk = (proj.shape[1] - MISC_COLS) // MISC_COLS
    blk0 = row0 // rows
    kern = functools.partial(_rwkv_kernel, rows=rows, lead=lead, width=width)
    full = lambda shape: pl.BlockSpec(shape, lambda b, c: (0,) * len(shape))
    seq_row = lambda b, c: blk0 + b * n_chunks + c
    in_specs = [
        pl.BlockSpec((rows, 3 * width), lambda b, c: (seq_row(b, c), rkv_blk)),
        pl.BlockSpec((rows, MISC_COLS), lambda b, c: (seq_row(b, c), misc_blk)),
        pl.BlockSpec((1, 1, 3 * width), lambda b, c: (b, 0, 0)),
        pl.BlockSpec((1, 1, 3 * LANE), lambda b, c: (b, 0, 0)),
        pl.BlockSpec((1, n_pairs, LANE, LANE), lambda b, c: (b, 0, 0, 0)),
        full((1, 3 * width)), full((1, 3 * LANE)),
        full((1, width)), full((LANE, width)), full((1, width)), full((LANE, width)), full((2 * LANE, width)),
        full((1, width)), full((1, width)), full((1, width)), full((1, width)), full((1, width)),
    ]
    out_specs = [
        pl.BlockSpec((rows, width), lambda b, c: (b * n_chunks + c, 0)),
        pl.BlockSpec((1, n_pairs, LANE, LANE), lambda b, c: (b, 0, 0, 0)),
    ]
    out_shape = [
        jax.ShapeDtypeStruct((batch * n_chunks * rows, width), BF16),
        jax.ShapeDtypeStruct((batch, n_pairs, LANE, LANE), F32),
    ]
    scratch = [pltpu.VMEM((1, 3 * width), F32), pltpu.VMEM((1, 3 * LANE), F32)]
    scratch += [pltpu.VMEM((CHUNK, width), F32) for _ in range(6)]
    return pl.pallas_call(
        kern, grid=(batch, n_chunks), in_specs=in_specs, out_specs=out_specs, out_shape=out_shape,
        scratch_shapes=scratch, compiler_params=_params("parallel", "arbitrary"), name="rwkv7_chunk",
    )(proj, proj, shift_rkv, shift_misc, s0, *prm)


def _mlstm_kernel(qk_ref, v_ref, o_ref, misc_ref, conv0_ref, c0_ref, n0_ref, m0_ref,
                  cw_ref, cb_ref, ifb_ref, nw_ref,
                  y_ref, c_ref, n_ref, m_ref,
                  ext, *, rows, lead, heads, hd):
    L = CHUNK
    c = pl.program_id(1)
    width = heads * hd

    @pl.when(c == 0)
    def _():
        c_ref[...] = c0_ref[...]
        n_ref[...] = n0_ref[...]
        m_ref[...] = m0_ref[...]
        ext[0:SUBLANE, :] = jnp.zeros((SUBLANE, 2 * width), F32)

    def chunk_rows(x):
        if rows < L:
            return jnp.concatenate([jnp.zeros((L - rows, x.shape[1]), F32), x], axis=0)
        return x

    row = _iota((L, 1), 0)
    valid = row >= jnp.where(c == 0, lead, 0)

    ext[SUBLANE:SUBLANE + L, :] = jnp.where(valid, chunk_rows(qk_ref[...]), 0.0)

    @pl.when(c == 0)
    def _():
        ext[lead:lead + SUBLANE, :] = conv0_ref[0]

    conv = cb_ref[...]
    for j in range(4):
        conv = conv + cw_ref[j:j + 1, :] * ext[SUBLANE - 3 + j:SUBLANE - 3 + j + L, :]
    ext[0:SUBLANE, :] = ext[L:L + SUBLANE, :]
    qk = jnp.where(valid, conv * _sigmoid(conv), 0.0)
    v = jnp.where(valid, chunk_rows(v_ref[...]), 0.0)
    og = _sigmoid(chunk_rows(o_ref[...]))

    lane = _iota((1, LANE), 1)
    is_i = lane < heads
    z = chunk_rows(misc_ref[:, MISC_IF:MISC_IF + LANE]) + ifb_ref[...]
    logsig = jnp.minimum(z, 0.0) - jnp.log1p(jnp.exp(-jnp.abs(z)))
    gates = jnp.where(valid, jnp.where(is_i, z, logsig), jnp.where(is_i, NEG, 0.0))
    tri = (_iota((L, L), 1) <= _iota((L, L), 0)).astype(BF16)
    cum = _cumsum_rows(tri, jnp.where(is_i, 0.0, gates))
    pad = jnp.zeros((LANE - L, LANE), F32)
    gates_t = jnp.concatenate([gates, pad], axis=0).T
    cum_t = jnp.concatenate([cum, pad], axis=0).T
    causal = _iota((L, L), 1) <= _iota((L, L), 0)
    m_all = m_ref[0]

    for h in range(heads):
        sl = slice(h * hd, (h + 1) * hd)
        q = qk[:, sl]
        kx = qk[:, width + h * hd:width + (h + 1) * hd] * (hd ** -0.5)
        vh = v[:, sl]
        b_col = cum[:, heads + h:heads + h + 1]
        i_col = gates[:, h:h + 1]
        b_row = cum_t[heads + h:heads + h + 1, :L]
        i_row = gates_t[h:h + 1, :L]
        b_end = b_col[L - 1:L, :]
        m_prev = jnp.sum(jnp.where(lane == h, m_all, 0.0), axis=1, keepdims=True)

        log_inter = b_col + m_prev
        dmat = jnp.where(causal, b_col - b_row + i_row, NEG)
        m_q = jnp.maximum(log_inter, jnp.max(dmat, axis=-1, keepdims=True))
        w_inter = jnp.exp(log_inter - m_q)
        qb = q.astype(BF16)
        s = _dot_nt(qb, kx.astype(BF16)) * jnp.exp(dmat - m_q)
        c_h = c_ref[0, h]
        n_h = n_ref[0, h:h + 1, :]
        num = w_inter * _dot(qb, c_h.astype(BF16)) + _dot(s.astype(BF16), vh.astype(BF16))
        den = w_inter * jnp.sum(q * n_h, axis=-1, keepdims=True) + jnp.sum(s, axis=-1, keepdims=True)
        hcell = num / jnp.maximum(jnp.abs(den), jnp.exp(-m_q))

        g_col = b_end - b_col + i_col
        m_new = jnp.maximum(b_end + m_prev, jnp.max(g_col, axis=0, keepdims=True))
        a_st = jnp.exp(b_end + m_prev - m_new)
        wkk = jnp.exp(g_col - m_new) * kx
        c_ref[0, h] = a_st * c_h + _dot_tn(wkk.astype(BF16), vh.astype(BF16))
        n_ref[0, h:h + 1, :] = a_st * n_h + jnp.sum(wkk, axis=0, keepdims=True)
        m_all = jnp.where(lane == h, m_new, m_all)

        mu = jnp.mean(hcell, axis=-1, keepdims=True)
        hc = hcell - mu
        var = jnp.mean(hc * hc, axis=-1, keepdims=True)
        yh = hc * lax.rsqrt(var + ML_LN_EPS) * nw_ref[:, sl] * og[:, sl]
        y_ref[:, sl] = yh[L - rows:, :].astype(y_ref.dtype)

    m_ref[0] = m_all


def _mlstm(proj, conv0, c0, n0, m0, prm, *, batch, n_chunks, rows, row0, lead, heads, hd, d_model):
    n = proj.shape[0]
    width = heads * hd
    blk0 = row0 // rows
    seq_row = lambda b, c: blk0 + b * n_chunks + c
    qk_blk = (2 * d_model) // (2 * width)
    v_blk = (2 * d_model + 2 * width + 3 * (d_model - width)) // width
    misc_blk = (proj.shape[1] - MISC_COLS) // MISC_COLS
    kern = functools.partial(_mlstm_kernel, rows=rows, lead=lead, heads=heads, hd=hd)
    full = lambda shape: pl.BlockSpec(shape, lambda b, c: (0,) * len(shape))
    in_specs = [
        pl.BlockSpec((rows, 2 * width), lambda b, c: (seq_row(b, c), qk_blk)),
        pl.BlockSpec((rows, width), lambda b, c: (seq_row(b, c), v_blk)),
        pl.BlockSpec((rows, width), lambda b, c: (seq_row(b, c), v_blk + 1)),
        pl.BlockSpec((rows, MISC_COLS), lambda b, c: (seq_row(b, c), misc_blk)),
        pl.BlockSpec((1, SUBLANE, 2 * width), lambda b, c: (b, 0, 0)),
        pl.BlockSpec((1, heads, hd, hd), lambda b, c: (b, 0, 0, 0)),
        pl.BlockSpec((1, heads, hd), lambda b, c: (b, 0, 0)),
        pl.BlockSpec((1, 1, LANE), lambda b, c: (b, 0, 0)),
        full((4, 2 * width)), full((1, 2 * width)), full((1, LANE)), full((1, width)),
    ]
    out_specs = [
        pl.BlockSpec((rows, width), lambda b, c: (b * n_chunks + c, 0)),
        pl.BlockSpec((1, heads, hd, hd), lambda b, c: (b, 0, 0, 0)),
        pl.BlockSpec((1, heads, hd), lambda b, c: (b, 0, 0)),
        pl.BlockSpec((1, 1, LANE), lambda b, c: (b, 0, 0)),
    ]
    out_shape = [
        jax.ShapeDtypeStruct((batch * n_chunks * rows, width), BF16),
        jax.ShapeDtypeStruct((batch, heads, hd, hd), F32),
        jax.ShapeDtypeStruct((batch, heads, hd), F32),
        jax.ShapeDtypeStruct((batch, 1, LANE), F32),
    ]
    return pl.pallas_call(
        kern, grid=(batch, n_chunks), in_specs=in_specs, out_specs=out_specs, out_shape=out_shape,
        scratch_shapes=[pltpu.VMEM((CHUNK + 2 * SUBLANE, 2 * width), F32)],
        compiler_params=_params("parallel", "arbitrary"), name="mlstm_chunk",
    )(proj, proj, proj, proj, conv0, c0, n0, m0, *prm)


def _pad_rows(w, rows):
    return jnp.pad(w, ((0, rows - w.shape[0]), (0, 0)))


def _swiglu_weights(w_gate, w_up, w_down):
    d, f = w_gate.shape
    fp = -(-f // FF_TILE) * FF_TILE
    wg = jnp.pad(w_gate, ((0, 0), (0, fp - f))).astype(BF16).reshape(d, fp // FF_TILE, 1, FF_TILE)
    wu = jnp.pad(w_up, ((0, 0), (0, fp - f))).astype(BF16).reshape(d, fp // FF_TILE, 1, FF_TILE)
    wgu = jnp.concatenate([wg, wu], axis=2).reshape(d, 2 * fp)
    return wgu, _pad_rows(w_down, fp).astype(BF16)


def kernel(x_prompt, x_sample, state_rwkv_shift, state_rwkv_wkv, state_mlstm_conv, state_mlstm_C,
           state_mlstm_n, state_mlstm_m, meta_tokens, ffn1_norm, ffn1_w_gate, ffn1_w_up, ffn1_w_down,
           mix_norm, w_in, rw_mu, rw_w0, rw_w2, rw_a0, rw_a2, rw_g2, rw_kk, rw_ka, rw_rk, rw_ln_w, rw_ln_b,
           ml_conv_w, ml_conv_b, ml_i_b, ml_f_b, ml_norm_w, w_br_rw, w_br_ml, w_out,
           ffn2_norm, ffn2_w_gate, ffn2_w_up, ffn2_w_down, final_norm):
    assert ffn1_norm.shape[0] == 1, "single-layer trunk"
    B, T, D = x_prompt.shape
    Bs, Ts, _ = x_sample.shape
    n_meta = meta_tokens.shape[0]
    W = rw_w0.shape[-1]
    dl, al, gl = rw_w2.shape[1], rw_a2.shape[1], rw_g2.shape[1]
    rw_heads, rw_hd = rw_rk.shape[1], rw_rk.shape[2]
    Wm = ml_norm_w.shape[-1]
    H = ml_i_b.shape[-1]
    hd = Wm // H
    K = ml_conv_w.shape[1]
    L = CHUNK
    assert rw_hd == LANE // 2 and dl == LANE // 2 and al == LANE // 2 and gl <= 2 * LANE
    assert K == 4 and hd % LANE == 0 and 2 * H <= LANE and W == Wm and 2 * W == D
    assert Ts <= L and Ts % (2 * SUBLANE) == 0 and D % COL_TILE == 0 and (11 * D // 2) % MISC_COLS == 0

    lead = (-n_meta) % L
    Tp = lead + n_meta + T
    n_chunks = Tp // L
    lead_s = L - Ts
    xp = jnp.concatenate([jnp.zeros((B, lead, D), F32),
                          jnp.broadcast_to(meta_tokens.astype(F32)[None], (B, n_meta, D)),
                          x_prompt], axis=1).reshape(B * Tp, D)
    n_real = B * Tp + Bs * Ts
    N = -(-n_real // PROJ_ROW_TILE) * PROJ_ROW_TILE
    x = jnp.concatenate([xp, x_sample.reshape(Bs * Ts, D), jnp.zeros((N - n_real, D), F32)], axis=0)

    wgu1, wd1 = _swiglu_weights(ffn1_w_gate[0], ffn1_w_up[0], ffn1_w_down[0])
    wgu2, wd2 = _swiglu_weights(ffn2_w_gate[0], ffn2_w_up[0], ffn2_w_down[0])
    wi = w_in[0]
    o_lora = 3 * W
    o_qk = o_lora + dl + al + gl
    o_v = o_qk + 2 * Wm
    o_o = o_v + Wm
    o_i = o_o + Wm
    o_gate = o_i + 2 * H
    zc = lambda n: jnp.zeros((D, n), wi.dtype)
    misc_w = jnp.concatenate([wi[:, o_lora:o_qk], zc(MISC_IF - (dl + al + gl)),
                              wi[:, o_i:o_gate], zc(MISC_COLS - MISC_IF - 2 * H)], axis=1)
    w_proj = jnp.concatenate([wi[:, o_gate:], wi[:, o_qk:o_v], wi[:, :o_lora], wi[:, o_v:o_i], misc_w],
                             axis=1).astype(BF16)

    def misc_vec(v_lora, fill=0.0):
        return jnp.pad(v_lora, [(0, 0)] * (v_lora.ndim - 1) + [(0, 3 * LANE - v_lora.shape[-1])],
                       constant_values=fill)

    row = lambda v: v.reshape(1, -1).astype(F32)
    rw_prm = (
        row(rw_mu[0, :o_lora]), misc_vec(row(rw_mu[0, o_lora:])),
        row(rw_w0[0]), _pad_rows(rw_w2[0], LANE).astype(BF16),
        row(rw_a0[0]), jnp.concatenate([jnp.zeros((dl, W), F32), rw_a2[0]], axis=0).astype(BF16),
        _pad_rows(rw_g2[0], 2 * LANE).astype(BF16),
        row(rw_kk[0]), row(rw_ka[0]), row(rw_rk[0]), row(rw_ln_w[0]), row(rw_ln_b[0]),
    )
    ifb = jnp.pad(jnp.concatenate([ml_i_b[0], ml_f_b[0]]).reshape(1, 2 * H), ((0, 0), (0, LANE - 2 * H)))
    ml_prm = (ml_conv_w[0].astype(F32), row(ml_conv_b[0]), ifb.astype(F32), row(ml_norm_w[0]))

    def rw_state_in(shift, wkv):
        b = shift.shape[0]
        s = wkv.reshape(b, rw_heads // 2, 2, rw_hd, rw_hd)
        z = jnp.zeros_like(s[:, :, 0])
        bd = jnp.concatenate([jnp.concatenate([s[:, :, 0], z], axis=-1),
                              jnp.concatenate([z, s[:, :, 1]], axis=-1)], axis=-2)
        return shift[:, None, :o_lora], misc_vec(shift[:, None, o_lora:]), bd

    def rw_state_out(bd):
        h = rw_hd
        return jnp.stack([bd[:, :, :h, :h], bd[:, :, h:, h:]], axis=2).reshape(bd.shape[0], rw_heads, h, h)

    def conv_in(buf):
        return jnp.pad(buf, ((0, 0), (SUBLANE - (K - 1), 0), (0, 0)))

    def m_in(m):
        return jnp.pad(m, ((0, 0), (0, LANE - H)))[:, None, :]

    x1 = _ffn(x, row(ffn1_norm[0]), wgu1, wd1, row(final_norm), final_norm=False)
    proj = _proj(x1, row(mix_norm[0]), w_proj)

    zeros = lambda *s: jnp.zeros(s, F32)
    seqs = (
        dict(batch=B, n_chunks=n_chunks, rows=L, row0=0, lead=lead),
        dict(batch=Bs, n_chunks=1, rows=Ts, row0=B * Tp, lead=lead_s),
    )
    rw_states = (
        rw_state_in(zeros(B, o_qk), zeros(B, rw_heads, rw_hd, rw_hd)),
        rw_state_in(state_rwkv_shift[0], state_rwkv_wkv[0]),
    )
    ml_states = (
        (conv_in(zeros(B, K - 1, 2 * Wm)), zeros(B, H, hd, hd), zeros(B, H, hd), m_in(zeros(B, H))),
        (conv_in(state_mlstm_conv[0]), state_mlstm_C[0], state_mlstm_n[0], m_in(state_mlstm_m[0])),
    )
    y_rw_parts, y_ml_parts, rw_out, ml_out = [], [], [], []
    for seq, rws, mls in zip(seqs, rw_states, ml_states):
        y_rw, s_end = _rwkv(proj, *rws, rw_prm, width=W, d_model=D, **seq)
        y_ml, c_end, n_end, m_end = _mlstm(proj, *mls, ml_prm, heads=H, hd=hd, d_model=D, **seq)
        y_rw_parts.append(y_rw)
        y_ml_parts.append(y_ml)
        rw_out.append(rw_state_out(s_end))
        ml_out.append((c_end, n_end, m_end[:, 0, :H]))

    split = B * Tp
    row_pad = [jnp.zeros((N - n_real, W), BF16)]
    y_rw = jnp.concatenate(y_rw_parts + row_pad, axis=0)
    y_ml = jnp.concatenate(y_ml_parts + row_pad, axis=0)

    x2 = _merge(x1, y_rw, y_ml, proj, w_br_rw[0].astype(BF16), w_br_ml[0].astype(BF16), w_out[0].astype(BF16))
    y = _ffn(x2, row(ffn2_norm[0]), wgu2, wd2, row(final_norm), final_norm=True)

    c_rkv = 2 * D + 2 * Wm
    c_misc = proj.shape[1] - MISC_COLS

    def seq_states(row0, batch, t_len):
        last = row0 + (jnp.arange(batch)[:, None] + 1) * t_len - (K - 1) + jnp.arange(K - 1)[None, :]
        tail = proj[last.reshape(-1)].reshape(batch, K - 1, -1)
        shift = jnp.concatenate([tail[:, -1, c_rkv:c_rkv + 3 * W],
                                 tail[:, -1, c_misc:c_misc + dl + al + gl]], axis=-1)
        return shift, tail[:, :, 2 * D:2 * D + 2 * Wm]

    p_shift, p_conv = seq_states(0, B, Tp)
    s_shift, s_conv = seq_states(split, Bs, Ts)
    y_prompt = y[:split].reshape(B, Tp, D)[:, lead + n_meta:]
    y_sample = y[split:n_real].reshape(Bs, Ts, D)
    d1 = lambda a: a[None]
    return (y_prompt, y_sample,
            d1(p_shift), d1(rw_out[0]), d1(p_conv), d1(ml_out[0][0]), d1(ml_out[0][1]), d1(ml_out[0][2]),
            d1(s_shift), d1(rw_out[1]), d1(s_conv), d1(ml_out[1][0]), d1(ml_out[1][1]), d1(ml_out[1][2]))
```

```python
import functools

import jax
import jax.numpy as jnp
from jax import lax
from jax.experimental import pallas as pl
from jax.experimental.pallas import tpu as pltpu

F32 = jnp.float32
BF16 = jnp.bfloat16

LANE = 128
SUBLANE = 8
VMEM_LIMIT_BYTES = 56 * 1024 * 1024
CHUNK = 64
ROW_TILE = 512
PROJ_ROW_TILE = 1024
FF_TILE = 512
COL_TILE = 512
MISC_COLS = 512
MISC_IF = 384
RMS_EPS = 1e-6
RW_GN_EPS = 64e-5
ML_LN_EPS = 1e-5
NEG = -1e30


def _dot(a, b):
    return jnp.dot(a, b, preferred_element_type=F32)


def _dot_nt(a, b):
    return lax.dot_general(a, b, (((1,), (1,)), ((), ())), preferred_element_type=F32)


def _dot_tn(a, b):
    return lax.dot_general(a, b, (((0,), (0,)), ((), ())), preferred_element_type=F32)


def _split3(x):
    h1 = x.astype(BF16)
    r1 = x - h1.astype(F32)
    h2 = r1.astype(BF16)
    r2 = r1 - h2.astype(F32)
    return h1, h2, r2.astype(BF16)


def _cumsum_rows(tri, x):
    n = x.shape[1]
    y = _dot(tri, jnp.concatenate(_split3(x), axis=1))
    return y[:, :n] + y[:, n:2 * n] + y[:, 2 * n:]


def _segsum(x, seg):
    rows = x.shape[0]
    y = _dot(jnp.concatenate(_split3(x), axis=0), seg)
    return y[:rows] + y[rows:2 * rows] + y[2 * rows:]


def _sigmoid(z):
    return jax.nn.sigmoid(z)


def _softplus(z):
    return jnp.maximum(z, 0.0) + jnp.log1p(jnp.exp(-jnp.abs(z)))


def _rmsnorm(x, g):
    return x * lax.rsqrt(jnp.mean(x * x, axis=-1, keepdims=True) + RMS_EPS) * g


def _iota(shape, dim):
    return lax.broadcasted_iota(jnp.int32, shape, dim)


def _params(*semantics):
    return pltpu.CompilerParams(dimension_semantics=semantics, vmem_limit_bytes=VMEM_LIMIT_BYTES)


def _ffn_kernel(*refs, tf, n_f, n_main, split_in, split_out, final_norm):
    n_x = 2 if split_in else 1
    n_o = 2 if split_out else 1
    x_refs = refs[:n_x]
    g_ref, wgu_ref, wd_ref, fg_ref = refs[n_x:n_x + 4]
    o_refs = refs[n_x + 4:n_x + 4 + n_o]
    xn_ref, acc_ref = refs[n_x + 4 + n_o:]
    i = pl.program_id(0)
    j = pl.program_id(1)

    def load_x():
        if split_in:
            return jnp.where(i < n_main, x_refs[0][...], x_refs[1][...])
        return x_refs[0][...]

    @pl.when(j == 0)
    def _():
        xn_ref[...] = _rmsnorm(load_x(), g_ref[...]).astype(BF16)
        acc_ref[...] = jnp.zeros_like(acc_ref)

    gu = _dot(xn_ref[...], wgu_ref[...])
    gate = gu[:, :tf]
    h = (gate * _sigmoid(gate) * gu[:, tf:]).astype(BF16)
    acc_ref[...] += _dot(h, wd_ref[...])

    def finish(o_ref):
        y = load_x() + 0.5 * acc_ref[...]
        if final_norm:
            y = _rmsnorm(y, fg_ref[...])
        o_ref[...] = y

    last = j == n_f - 1
    if split_out:
        pl.when(last & (i < n_main))(lambda: finish(o_refs[0]))
        pl.when(last & (i >= n_main))(lambda: finish(o_refs[1]))
    else:
        pl.when(last)(lambda: finish(o_refs[0]))


def _ffn(xs, norm_g, wgu, wd, final_g, *, n_main, split_out, final_norm):
    split_in = len(xs) == 2
    d = xs[0].shape[1]
    n = sum(x.shape[0] for x in xs)
    n_f = wd.shape[0] // FF_TILE
    nm = n_main // ROW_TILE
    kern = functools.partial(_ffn_kernel, tf=FF_TILE, n_f=n_f, n_main=nm, split_in=split_in,
                             split_out=split_out, final_norm=final_norm)
    whole = pl.BlockSpec((ROW_TILE, d), lambda i, j: (i, 0))
    main = pl.BlockSpec((ROW_TILE, d), lambda i, j: (jnp.minimum(i, nm - 1), 0))
    extra = pl.BlockSpec((ROW_TILE, d), lambda i, j: (jnp.maximum(i - nm, 0), 0))
    sds = lambda rows: jax.ShapeDtypeStruct((rows, d), F32)
    return pl.pallas_call(
        kern,
        grid=(n // ROW_TILE, n_f),
        in_specs=([main, extra] if split_in else [whole]) + [
            pl.BlockSpec((1, d), lambda i, j: (0, 0)),
            pl.BlockSpec((d, 2 * FF_TILE), lambda i, j: (0, j)),
            pl.BlockSpec((FF_TILE, d), lambda i, j: (j, 0)),
            pl.BlockSpec((1, d), lambda i, j: (0, 0)),
        ],
        out_specs=[main, extra] if split_out else whole,
        out_shape=[sds(n_main), sds(n - n_main)] if split_out else sds(n),
        scratch_shapes=[pltpu.VMEM((ROW_TILE, d), BF16), pltpu.VMEM((ROW_TILE, d), F32)],
        compiler_params=_params("arbitrary", "arbitrary"),
        name="ffn_final" if final_norm else "ffn",
    )(*xs, norm_g, wgu, wd, final_g)


def _proj_kernel(x_ref, g_ref, w_ref, o_ref, xn_ref):
    @pl.when(pl.program_id(1) == 0)
    def _():
        xn_ref[...] = _rmsnorm(x_ref[...], g_ref[...]).astype(BF16)

    o_ref[...] = _dot(xn_ref[...], w_ref[...])


def _proj(x, norm_g, w):
    n, d = x.shape
    cols = w.shape[1]
    tm = PROJ_ROW_TILE
    return pl.pallas_call(
        _proj_kernel,
        grid=(n // tm, cols // COL_TILE),
        in_specs=[
            pl.BlockSpec((tm, d), lambda i, j: (i, 0)),
            pl.BlockSpec((1, d), lambda i, j: (0, 0)),
            pl.BlockSpec((d, COL_TILE), lambda i, j: (0, j)),
        ],
        out_specs=pl.BlockSpec((tm, COL_TILE), lambda i, j: (i, j)),
        out_shape=jax.ShapeDtypeStruct((n, cols), F32),
        scratch_shapes=[pltpu.VMEM((tm, d), BF16)],
        compiler_params=_params("parallel", "arbitrary"),
        name="in_proj",
    )(x, norm_g, w)


def _merge_kernel(x_ref, yrw_ref, yml_ref, ga_ref, gb_ref, wrw_ref, wml_ref, wo_ref, o_ref, *, n_j):
    j = pl.program_id(1)

    @pl.when(j == 0)
    def _():
        o_ref[...] = jnp.zeros_like(o_ref)

    merged = (_sigmoid(ga_ref[...]) * _dot(yrw_ref[...], wrw_ref[...])
              + _sigmoid(gb_ref[...]) * _dot(yml_ref[...], wml_ref[...]))
    o_ref[...] += _dot(merged.astype(BF16), wo_ref[...])

    @pl.when(j == n_j - 1)
    def _():
        o_ref[...] = x_ref[...] + o_ref[...]


def _merge(x, y_rw, y_ml, proj, w_rw, w_ml, w_out):
    n, d = x.shape
    w = y_rw.shape[1]
    n_j = d // COL_TILE
    kern = functools.partial(_merge_kernel, n_j=n_j)
    return pl.pallas_call(
        kern,
        grid=(n // ROW_TILE, n_j),
        in_specs=[
            pl.BlockSpec((ROW_TILE, d), lambda i, j: (i, 0)),
            pl.BlockSpec((ROW_TILE, w), lambda i, j: (i, 0)),
            pl.BlockSpec((ROW_TILE, w), lambda i, j: (i, 0)),
            pl.BlockSpec((ROW_TILE, COL_TILE), lambda i, j: (i, j)),
            pl.BlockSpec((ROW_TILE, COL_TILE), lambda i, j: (i, n_j + j)),
            pl.BlockSpec((w, COL_TILE), lambda i, j: (0, j)),
            pl.BlockSpec((w, COL_TILE), lambda i, j: (0, j)),
            pl.BlockSpec((COL_TILE, d), lambda i, j: (j, 0)),
        ],
        out_specs=pl.BlockSpec((ROW_TILE, d), lambda i, j: (i, 0)),
        out_shape=jax.ShapeDtypeStruct((n, d), F32),
        compiler_params=_params("parallel", "arbitrary"),
        name="merge",
    )(x, y_rw, y_ml, proj, proj, w_rw, w_ml, w_out)


def _each(f, *lists):
    return [f(*xs) for xs in zip(*lists)]


def _rwkv_pairs(r, k, v, a, lw, g, s0, kkw, kaw, rkw, lnw, lnb, cst):
    L = r[0].shape[0]
    tri, seg, head0, strict, incl, col_head0, eye2, blockdiag = cst
    bf = lambda x: x.astype(BF16)

    kk_ = _each(lambda k_, w_: k_ * w_, k, kkw)
    kn = _each(lambda x: _segsum(x * x, seg), kk_)
    cum = _each(lambda x: _cumsum_rows(tri, x), lw)
    kk = _each(lambda x, n_: x / jnp.maximum(jnp.sqrt(n_), 1e-12), kk_, kn)
    k2 = _each(lambda k_, a_, w_: k_ * (1.0 + (a_ - 1.0) * w_), k, a, kaw)
    kka = _each(lambda x, a_: x * a_, kk, a)
    c_end = [c[L - 1:L, :] for c in cum]
    e_inv = [jnp.exp(-c) for c in cum]
    e_rem = _each(lambda c, ce: jnp.exp(ce - c), cum, c_end)
    at = _each(lambda x, c, l_: bf(-x * jnp.exp(c - l_)), kk, cum, lw)
    rt = _each(lambda x, c: bf(x * jnp.exp(c)), r, cum)
    bt = _each(lambda x, e: bf(x * e), kka, e_inv)
    kt = _each(lambda x, e: bf(x * e), k2, e_inv)
    vb = [bf(x) for x in v]
    zero = jnp.zeros_like(vb[0])

    def by_head(x):
        return jnp.concatenate([jnp.where(head0, x, zero), jnp.where(head0, zero, x)], axis=0)

    lhs = _each(lambda x, y_: jnp.concatenate([x, y_], axis=0), at, rt)
    ab = _each(lambda l_, x: _dot_nt(l_, by_head(x)), lhs, bt)
    ak = _each(lambda l_, x: _dot_nt(l_, by_head(x)), lhs, kt)
    a_rb = [jnp.where(incl, x[L:], 0.0) for x in ab]
    a_ak = [jnp.where(strict, x[:L], 0.0) for x in ak]
    a_rk = [jnp.where(incl, x[L:], 0.0) for x in ak]

    def blockdiag2(x):
        n_cat = jnp.where(strict, x[:L], 0.0)
        return jnp.concatenate([jnp.where(col_head0, n_cat, 0.0), jnp.where(col_head0, 0.0, n_cat)], axis=0)

    p = [blockdiag2(x) for x in ab]
    t_inv = [eye2 + x for x in p]
    for _ in range(L.bit_length() - 2):
        p = [_dot(bf(x), bf(x)) for x in p]
        t_inv = _each(lambda t_, x: t_ + _dot(bf(t_), bf(x)), t_inv, p)

    sb = [bf(x) for x in s0]
    vs = [by_head(x) for x in vb]
    x0 = _each(lambda a_, s_, m_, v_: _dot_nt(a_, s_) + _dot(bf(m_), v_), at, sb, a_ak, vs)
    us = _each(lambda t_, x: _dot(bf(t_), by_head(bf(x))), t_inv, x0)
    y = _each(lambda r_, s_, m1, m2, u_, v_: _dot_nt(r_, s_) + _dot(
        bf(jnp.concatenate([m1, m2], axis=1)), jnp.concatenate([bf(u_), v_], axis=0)),
        rt, sb, a_rb, a_rk, us, vs)
    upd = _each(lambda u_, v_, x1, x2, e: _dot_tn(
        jnp.concatenate([bf(u_[:L] + u_[L:]), v_], axis=0),
        jnp.concatenate([bf(x1 * e), bf(x2 * e)], axis=0)), us, vb, kka, k2, e_rem)
    s_new = _each(lambda s_, ce, u_: s_ * jnp.exp(ce) + jnp.where(blockdiag, u_, 0.0), s0, c_end, upd)

    hd = float(LANE // 2)
    mean = [_segsum(x, seg) / hd for x in y]
    bsum = _each(lambda r_, k_, w_: _segsum(r_ * k_ * w_, seg), r, k2, rkw)
    yc = _each(lambda x, m_: x - m_, y, mean)
    var = [_segsum(x * x, seg) / hd for x in yc]
    out = _each(lambda x, v_, w_, b_, bs, vv, g_: (x * lax.rsqrt(v_ + RW_GN_EPS) * w_ + b_ + bs * vv) * g_,
                yc, var, lnw, lnb, bsum, v, g)
    return out, s_new


def _rwkv_kernel(rkv_ref, misc_ref, sh_rkv_ref, sh_misc_ref, s0_ref,
                 mu_rkv_ref, mu_misc_ref, w0_ref, w2_ref, a0_ref, a2_ref, g2_ref,
                 kk_ref, ka_ref, rk_ref, lnw_ref, lnb_ref, y_prev_ref,
                 y_ref, s_ref,
                 carry_rkv, carry_misc, r_s, k_s, v_s, a_s, lw_s, g_s,
                 *, rows, lead, width):
    del y_prev_ref
    L = CHUNK
    c = pl.program_id(1)

    @pl.when(c == 0)
    def _():
        carry_rkv[...] = sh_rkv_ref[0]
        carry_misc[...] = sh_misc_ref[0]
        s_ref[...] = s0_ref[...]

    p = rkv_ref[...]
    pm = misc_ref[:, :3 * LANE]
    if rows < L:
        p = jnp.concatenate([jnp.zeros((L - rows, p.shape[1]), F32), p], axis=0)
        pm = jnp.concatenate([jnp.zeros((L - rows, pm.shape[1]), F32), pm], axis=0)
    row = _iota((L, 1), 0)
    first = jnp.where(c == 0, lead, 0)
    valid = row >= first
    p = jnp.where(valid, p, 0.0)
    pm = jnp.where(valid, pm, 0.0)

    def shift_mix(cur, carry_ref, mu):
        prev = jnp.where(row == first, carry_ref[...], pltpu.roll(cur, 1, axis=0))
        carry_ref[...] = cur[L - 1:L, :]
        return jnp.where(valid, cur + mu * (prev - cur), 0.0)

    u = shift_mix(p, carry_rkv, mu_rkv_ref[...])
    um = shift_mix(pm, carry_misc, mu_misc_ref[...])

    lora = um[:, :LANE]
    wl = w0_ref[...] + _dot(jnp.tanh(lora).astype(BF16), w2_ref[...])
    wlog = -_softplus(-wl) - 0.5
    lw_s[...] = jnp.where(valid, -jnp.exp(wlog), 0.0)
    a_s[...] = _sigmoid(a0_ref[...] + _dot(lora.astype(BF16), a2_ref[...]))
    g_s[...] = _dot(_sigmoid(um[:, LANE:]).astype(BF16), g2_ref[...])
    r_s[...] = u[:, :width]
    k_s[...] = u[:, width:2 * width]
    v_s[...] = u[:, 2 * width:]

    lane = _iota((1, LANE), 1)
    half = LANE // 2
    col2 = _iota((L, 2 * L), 1)
    t_i = _iota((L, 2 * L), 0)
    s_i = col2 & (L - 1)
    cst = (
        (_iota((L, L), 1) <= _iota((L, L), 0)).astype(BF16),
        ((_iota((LANE, LANE), 0) < half) == (_iota((LANE, LANE), 1) < half)).astype(BF16),
        lane < half,
        s_i < t_i, s_i <= t_i,
        col2 < L,
        (_iota((2 * L, 2 * L), 0) == _iota((2 * L, 2 * L), 1)).astype(F32),
        (_iota((LANE, LANE), 0) < half) == (_iota((LANE, LANE), 1) < half),
    )
    sls = [slice(j * LANE, (j + 1) * LANE) for j in range(width // LANE)]
    pick = lambda ref: [ref[:, sl] for sl in sls]
    outs, s_new = _rwkv_pairs(pick(r_s), pick(k_s), pick(v_s), pick(a_s), pick(lw_s), pick(g_s),
                              [s_ref[0, j] for j in range(len(sls))],
                              pick(kk_ref), pick(ka_ref), pick(rk_ref), pick(lnw_ref), pick(lnb_ref), cst)
    for j, sl in enumerate(sls):
        s_ref[0, j] = s_new[j]
        y_ref[:, sl] = outs[j][L - rows:, :].astype(y_ref.dtype)


def _rwkv(proj, y_prev, shift_rkv, shift_misc, s0, prm, *, batch, n_chunks, rows, seq_row, width, lead, d_model):
    n = proj.shape[0]
    n_pairs = width // LANE
    rkv_blk = (2 * d_model + 2 * width) // (3 * width)
    misc_blk = (proj.shape[1] - MISC_COLS) // MISC_COLS
    kern = functools.partial(_rwkv_kernel, rows=rows, lead=lead, width=width)
    full = lambda shape: pl.BlockSpec(shape, lambda b, c: (0,) * len(shape))
    in_specs = [
        pl.BlockSpec((rows, 3 * width), lambda b, c: (seq_row(b, c), rkv_blk)),
        pl.BlockSpec((rows, MISC_COLS), lambda b, c: (seq_row(b, c), misc_blk)),
        pl.BlockSpec((1, 1, 3 * width), lambda b, c: (b, 0, 0)),
        pl.BlockSpec((1, 1, 3 * LANE), lambda b, c: (b, 0, 0)),
        pl.BlockSpec((1, n_pairs, LANE, LANE), lambda b, c: (b, 0, 0, 0)),
        full((1, 3 * width)), full((1, 3 * LANE)),
        full((1, width)), full((LANE, width)), full((1, width)), full((LANE, width)), full((2 * LANE, width)),
        full((1, width)), full((1, width)), full((1, width)), full((1, width)), full((1, width)),
        pl.BlockSpec(memory_space=pl.ANY),
    ]
    out_specs = [
        pl.BlockSpec((rows, width), lambda b, c: (seq_row(b, c), 0)),
        pl.BlockSpec((1, n_pairs, LANE, LANE), lambda b, c: (b, 0, 0, 0)),
    ]
    out_shape = [
        jax.ShapeDtypeStruct((n, width), BF16),
        jax.ShapeDtypeStruct((batch, n_pairs, LANE, LANE), F32),
    ]
    scratch = [pltpu.VMEM((1, 3 * width), F32), pltpu.VMEM((1, 3 * LANE), F32)]
    scratch += [pltpu.VMEM((CHUNK, width), F32) for _ in range(6)]
    return pl.pallas_call(
        kern, grid=(batch, n_chunks), in_specs=in_specs, out_specs=out_specs, out_shape=out_shape,
        scratch_shapes=scratch, input_output_aliases={len(in_specs) - 1: 0},
        compiler_params=_params("arbitrary", "arbitrary"), name="rwkv7_chunk",
    )(proj, proj, shift_rkv, shift_misc, s0, *prm, y_prev)


def _mlstm_kernel(qk_ref, v_ref, o_ref, misc_ref, conv0_ref, c0_ref, n0_ref, m0_ref,
                  cw_ref, cb_ref, ifb_ref, nw_ref, y_prev_ref,
                  y_ref, c_ref, n_ref, m_ref,
                  ext, *, rows, lead, heads, hd):
    del y_prev_ref
    L = CHUNK
    c = pl.program_id(1)
    width = heads * hd

    @pl.when(c == 0)
    def _():
        c_ref[...] = c0_ref[...]
        n_ref[...] = n0_ref[...]
        m_ref[...] = m0_ref[...]
        ext[0:SUBLANE, :] = jnp.zeros((SUBLANE, 2 * width), F32)

    def chunk_rows(x):
        if rows < L:
            return jnp.concatenate([jnp.zeros((L - rows, x.shape[1]), F32), x], axis=0)
        return x

    row = _iota((L, 1), 0)
    valid = row >= jnp.where(c == 0, lead, 0)

    ext[SUBLANE:SUBLANE + L, :] = jnp.where(valid, chunk_rows(qk_ref[...]), 0.0)

    @pl.when(c == 0)
    def _():
        ext[lead:lead + SUBLANE, :] = conv0_ref[0]

    conv = cb_ref[...]
    for j in range(4):
        conv = conv + cw_ref[j:j + 1, :] * ext[SUBLANE - 3 + j:SUBLANE - 3 + j + L, :]
    ext[0:SUBLANE, :] = ext[L:L + SUBLANE, :]
    qk = jnp.where(valid, conv * _sigmoid(conv), 0.0)
    v = jnp.where(valid, chunk_rows(v_ref[...]), 0.0)
    og = _sigmoid(chunk_rows(o_ref[...]))

    lane = _iota((1, LANE), 1)
    is_i = lane < heads
    z = chunk_rows(misc_ref[:, MISC_IF:MISC_IF + LANE]) + ifb_ref[...]
    logsig = jnp.minimum(z, 0.0) - jnp.log1p(jnp.exp(-jnp.abs(z)))
    gates = jnp.where(valid, jnp.where(is_i, z, logsig), jnp.where(is_i, NEG, 0.0))
    tri = (_iota((L, L), 1) <= _iota((L, L), 0)).astype(BF16)
    cum = _cumsum_rows(tri, jnp.where(is_i, 0.0, gates))
    pad = jnp.zeros((LANE - L, LANE), F32)
    gates_t = jnp.concatenate([gates, pad], axis=0).T
    cum_t = jnp.concatenate([cum, pad], axis=0).T
    causal = _iota((L, L), 1) <= _iota((L, L), 0)
    m_all = m_ref[0]

    for h in range(heads):
        sl = slice(h * hd, (h + 1) * hd)
        q = qk[:, sl]
        kx = qk[:, width + h * hd:width + (h + 1) * hd] * (hd ** -0.5)
        vh = v[:, sl]
        b_col = cum[:, heads + h:heads + h + 1]
        i_col = gates[:, h:h + 1]
        b_row = cum_t[heads + h:heads + h + 1, :L]
        i_row = gates_t[h:h + 1, :L]
        b_end = b_col[L - 1:L, :]
        m_prev = jnp.sum(jnp.where(lane == h, m_all, 0.0), axis=1, keepdims=True)

        log_inter = b_col + m_prev
        dmat = jnp.where(causal, b_col - b_row + i_row, NEG)
        m_q = jnp.maximum(log_inter, jnp.max(dmat, axis=-1, keepdims=True))
        w_inter = jnp.exp(log_inter - m_q)
        qb = q.astype(BF16)
        s = _dot_nt(qb, kx.astype(BF16)) * jnp.exp(dmat - m_q)
        c_h = c_ref[0, h]
        n_h = n_ref[0, h:h + 1, :]
        num = w_inter * _dot(qb, c_h.astype(BF16)) + _dot(s.astype(BF16), vh.astype(BF16))
        den = w_inter * jnp.sum(q * n_h, axis=-1, keepdims=True) + jnp.sum(s, axis=-1, keepdims=True)
        hcell = num / jnp.maximum(jnp.abs(den), jnp.exp(-m_q))

        g_col = b_end - b_col + i_col
        m_new = jnp.maximum(b_end + m_prev, jnp.max(g_col, axis=0, keepdims=True))
        a_st = jnp.exp(b_end + m_prev - m_new)
        wkk = jnp.exp(g_col - m_new) * kx
        c_ref[0, h] = a_st * c_h + _dot_tn(wkk.astype(BF16), vh.astype(BF16))
        n_ref[0, h:h + 1, :] = a_st * n_h + jnp.sum(wkk, axis=0, keepdims=True)
        m_all = jnp.where(lane == h, m_new, m_all)

        mu = jnp.mean(hcell, axis=-1, keepdims=True)
        hc = hcell - mu
        var = jnp.mean(hc * hc, axis=-1, keepdims=True)
        yh = hc * lax.rsqrt(var + ML_LN_EPS) * nw_ref[:, sl] * og[:, sl]
        y_ref[:, sl] = yh[L - rows:, :].astype(y_ref.dtype)

    m_ref[0] = m_all


def _mlstm(proj, y_prev, conv0, c0, n0, m0, prm, *, batch, n_chunks, rows, seq_row, heads, hd, lead, d_model):
    n = proj.shape[0]
    width = heads * hd
    qk_blk = (2 * d_model) // (2 * width)
    v_blk = (2 * d_model + 2 * width + 3 * (d_model - width)) // width
    misc_blk = (proj.shape[1] - MISC_COLS) // MISC_COLS
    kern = functools.partial(_mlstm_kernel, rows=rows, lead=lead, heads=heads, hd=hd)
    full = lambda shape: pl.BlockSpec(shape, lambda b, c: (0,) * len(shape))
    in_specs = [
        pl.BlockSpec((rows, 2 * width), lambda b, c: (seq_row(b, c), qk_blk)),
        pl.BlockSpec((rows, width), lambda b, c: (seq_row(b, c), v_blk)),
        pl.BlockSpec((rows, width), lambda b, c: (seq_row(b, c), v_blk + 1)),
        pl.BlockSpec((rows, MISC_COLS), lambda b, c: (seq_row(b, c), misc_blk)),
        pl.BlockSpec((1, SUBLANE, 2 * width), lambda b, c: (b, 0, 0)),
        pl.BlockSpec((1, heads, hd, hd), lambda b, c: (b, 0, 0, 0)),
        pl.BlockSpec((1, heads, hd), lambda b, c: (b, 0, 0)),
        pl.BlockSpec((1, 1, LANE), lambda b, c: (b, 0, 0)),
        full((4, 2 * width)), full((1, 2 * width)), full((1, LANE)), full((1, width)),
        pl.BlockSpec(memory_space=pl.ANY),
    ]
    out_specs = [
        pl.BlockSpec((rows, width), lambda b, c: (seq_row(b, c), 0)),
        pl.BlockSpec((1, heads, hd, hd), lambda b, c: (b, 0, 0, 0)),
        pl.BlockSpec((1, heads, hd), lambda b, c: (b, 0, 0)),
        pl.BlockSpec((1, 1, LANE), lambda b, c: (b, 0, 0)),
    ]
    out_shape = [
        jax.ShapeDtypeStruct((n, width), BF16),
        jax.ShapeDtypeStruct((batch, heads, hd, hd), F32),
        jax.ShapeDtypeStruct((batch, heads, hd), F32),
        jax.ShapeDtypeStruct((batch, 1, LANE), F32),
    ]
    return pl.pallas_call(
        kern, grid=(batch, n_chunks), in_specs=in_specs, out_specs=out_specs, out_shape=out_shape,
        scratch_shapes=[pltpu.VMEM((CHUNK + SUBLANE, 2 * width), F32)],
        input_output_aliases={len(in_specs) - 1: 0},
        compiler_params=_params("arbitrary", "arbitrary"), name="mlstm_chunk",
    )(proj, proj, proj, proj, conv0, c0, n0, m0, *prm, y_prev)


def _pad_rows(w, rows):
    return jnp.pad(w, ((0, rows - w.shape[0]), (0, 0)))


def _swiglu_weights(w_gate, w_up, w_down):
    d, f = w_gate.shape
    fp = -(-f // FF_TILE) * FF_TILE
    wg = jnp.pad(w_gate, ((0, 0), (0, fp - f))).astype(BF16).reshape(d, fp // FF_TILE, 1, FF_TILE)
    wu = jnp.pad(w_up, ((0, 0), (0, fp - f))).astype(BF16).reshape(d, fp // FF_TILE, 1, FF_TILE)
    wgu = jnp.concatenate([wg, wu], axis=2).reshape(d, 2 * fp)
    return wgu, _pad_rows(w_down, fp).astype(BF16)


def kernel(x_prompt, x_sample, state_rwkv_shift, state_rwkv_wkv, state_mlstm_conv, state_mlstm_C,
           state_mlstm_n, state_mlstm_m, meta_tokens, ffn1_norm, ffn1_w_gate, ffn1_w_up, ffn1_w_down,
           mix_norm, w_in, rw_mu, rw_w0, rw_w2, rw_a0, rw_a2, rw_g2, rw_kk, rw_ka, rw_rk, rw_ln_w, rw_ln_b,
           ml_conv_w, ml_conv_b, ml_i_b, ml_f_b, ml_norm_w, w_br_rw, w_br_ml, w_out,
           ffn2_norm, ffn2_w_gate, ffn2_w_up, ffn2_w_down, final_norm):
    assert ffn1_norm.shape[0] == 1, "single-layer trunk"
    B, T, D = x_prompt.shape
    Bs, Ts, _ = x_sample.shape
    n_meta = meta_tokens.shape[0]
    W = rw_w0.shape[-1]
    dl, al, gl = rw_w2.shape[1], rw_a2.shape[1], rw_g2.shape[1]
    rw_heads, rw_hd = rw_rk.shape[1], rw_rk.shape[2]
    Wm = ml_norm_w.shape[-1]
    H = ml_i_b.shape[-1]
    hd = Wm // H
    K = ml_conv_w.shape[1]
    L = CHUNK
    assert rw_hd == LANE // 2 and dl == LANE // 2 and al == LANE // 2 and gl <= 2 * LANE
    assert K == 4 and hd % LANE == 0 and 2 * H <= LANE and W == Wm and 2 * W == D
    assert Ts <= L and Ts % (2 * SUBLANE) == 0 and D % COL_TILE == 0 and (11 * D // 2) % MISC_COLS == 0

    lead = (-n_meta) % L
    head_rows = lead + n_meta
    n_head = head_rows // L
    n_chunks = n_head + T // L
    lead_s = L - Ts
    n_main = B * T
    sample0 = n_main + B * head_rows
    N = sample0 + Bs * Ts
    assert T % L == 0 and n_main % ROW_TILE == 0 and N % PROJ_ROW_TILE == 0
    x_main = x_prompt.reshape(n_main, D)
    head = jnp.concatenate([jnp.zeros((lead, D), F32), meta_tokens.astype(F32)], axis=0)
    x_extra = jnp.concatenate([jnp.broadcast_to(head[None], (B, head_rows, D)).reshape(B * head_rows, D),
                               x_sample.reshape(Bs * Ts, D)], axis=0)

    wgu1, wd1 = _swiglu_weights(ffn1_w_gate[0], ffn1_w_up[0], ffn1_w_down[0])
    wgu2, wd2 = _swiglu_weights(ffn2_w_gate[0], ffn2_w_up[0], ffn2_w_down[0])
    wi = w_in[0]
    o_lora = 3 * W
    o_qk = o_lora + dl + al + gl
    o_v = o_qk + 2 * Wm
    o_o = o_v + Wm
    o_i = o_o + Wm
    o_gate = o_i + 2 * H
    zc = lambda n: jnp.zeros((D, n), wi.dtype)
    misc_w = jnp.concatenate([wi[:, o_lora:o_qk], zc(MISC_IF - (dl + al + gl)),
                              wi[:, o_i:o_gate], zc(MISC_COLS - MISC_IF - 2 * H)], axis=1)
    w_proj = jnp.concatenate([wi[:, o_gate:], wi[:, o_qk:o_v], wi[:, :o_lora], wi[:, o_v:o_i], misc_w],
                             axis=1).astype(BF16)

    def misc_vec(v_lora, fill=0.0):
        return jnp.pad(v_lora, [(0, 0)] * (v_lora.ndim - 1) + [(0, 3 * LANE - v_lora.shape[-1])],
                       constant_values=fill)

    row = lambda v: v.reshape(1, -1).astype(F32)
    rw_prm = (
        row(rw_mu[0, :o_lora]), misc_vec(row(rw_mu[0, o_lora:])),
        row(rw_w0[0]), _pad_rows(rw_w2[0], LANE).astype(BF16),
        row(rw_a0[0]), jnp.concatenate([jnp.zeros((dl, W), F32), rw_a2[0]], axis=0).astype(BF16),
        _pad_rows(rw_g2[0], 2 * LANE).astype(BF16),
        row(rw_kk[0]), row(rw_ka[0]), row(rw_rk[0]), row(rw_ln_w[0]), row(rw_ln_b[0]),
    )
    ifb = jnp.pad(jnp.concatenate([ml_i_b[0], ml_f_b[0]]).reshape(1, 2 * H), ((0, 0), (0, LANE - 2 * H)))
    ml_prm = (ml_conv_w[0].astype(F32), row(ml_conv_b[0]), ifb.astype(F32), row(ml_norm_w[0]))

    def rw_state_in(shift, wkv):
        b = shift.shape[0]
        s = wkv.reshape(b, rw_heads // 2, 2, rw_hd, rw_hd)
        z = jnp.zeros_like(s[:, :, 0])
        bd = jnp.concatenate([jnp.concatenate([s[:, :, 0], z], axis=-1),
                              jnp.concatenate([z, s[:, :, 1]], axis=-1)], axis=-2)
        return shift[:, None, :o_lora], misc_vec(shift[:, None, o_lora:]), bd

    def rw_state_out(bd):
        h = rw_hd
        return jnp.stack([bd[:, :, :h, :h], bd[:, :, h:, h:]], axis=2).reshape(bd.shape[0], rw_heads, h, h)

    def conv_in(buf):
        return jnp.pad(buf, ((0, 0), (SUBLANE - (K - 1), 0), (0, 0)))

    def m_in(m):
        return jnp.pad(m, ((0, 0), (0, LANE - H)))[:, None, :]

    x1 = _ffn((x_main, x_extra), row(ffn1_norm[0]), wgu1, wd1, row(final_norm),
              n_main=n_main, split_out=False, final_norm=False)
    proj = _proj(x1, row(mix_norm[0]), w_proj)

    zeros = lambda *s: jnp.zeros(s, F32)
    main_chunks = T // L

    def prompt_row(b, c):
        return jnp.where(c < n_head, n_main // L + b * n_head + c, b * main_chunks + c - n_head)

    seqs = (
        dict(batch=B, n_chunks=n_chunks, rows=L, seq_row=prompt_row, lead=lead),
        dict(batch=Bs, n_chunks=1, rows=Ts, seq_row=lambda b, c: sample0 // Ts + b, lead=lead_s),
    )
    rw_states = (
        rw_state_in(zeros(B, o_qk), zeros(B, rw_heads, rw_hd, rw_hd)),
        rw_state_in(state_rwkv_shift[0], state_rwkv_wkv[0]),
    )
    ml_states = (
        (conv_in(zeros(B, K - 1, 2 * Wm)), zeros(B, H, hd, hd), zeros(B, H, hd), m_in(zeros(B, H))),
        (conv_in(state_mlstm_conv[0]), state_mlstm_C[0], state_mlstm_n[0], m_in(state_mlstm_m[0])),
    )
    y_rw = jnp.zeros((N, W), BF16)
    y_ml = jnp.zeros((N, Wm), BF16)
    rw_out, ml_out = [], []
    for seq, rws, mls in zip(seqs, rw_states, ml_states):
        y_rw, s_end = _rwkv(proj, y_rw, *rws, rw_prm, width=W, d_model=D, **seq)
        y_ml, c_end, n_end, m_end = _mlstm(proj, y_ml, *mls, ml_prm, heads=H, hd=hd, d_model=D, **seq)
        rw_out.append(rw_state_out(s_end))
        ml_out.append((c_end, n_end, m_end[:, 0, :H]))

    x2 = _merge(x1, y_rw, y_ml, proj, w_br_rw[0].astype(BF16), w_br_ml[0].astype(BF16), w_out[0].astype(BF16))
    y_main, y_extra = _ffn((x2,), row(ffn2_norm[0]), wgu2, wd2, row(final_norm),
                           n_main=n_main, split_out=True, final_norm=True)

    c_rkv = 2 * D + 2 * Wm
    c_misc = proj.shape[1] - MISC_COLS

    def seq_states(row0, batch, t_len):
        last = row0 + (jnp.arange(batch)[:, None] + 1) * t_len - (K - 1) + jnp.arange(K - 1)[None, :]
        tail = proj[last.reshape(-1)].reshape(batch, K - 1, -1)
        shift = jnp.concatenate([tail[:, -1, c_rkv:c_rkv + 3 * W],
                                 tail[:, -1, c_misc:c_misc + dl + al + gl]], axis=-1)
        return shift, tail[:, :, 2 * D:2 * D + 2 * Wm]

    p_shift, p_conv = seq_states(0, B, T)
    s_shift, s_conv = seq_states(sample0, Bs, Ts)
    y_prompt = y_main.reshape(B, T, D)
    y_sample = y_extra[B * head_rows:].reshape(Bs, Ts, D)
    d1 = lambda a: a[None]
    return (y_prompt, y_sample,
            d1(p_shift), d1(rw_out[0]), d1(p_conv), d1(ml_out[0][0]), d1(ml_out[0][1]), d1(ml_out[0][2]),
            d1(s_shift), d1(rw_out[1]), d1(s_conv), d1(ml_out[1][0]), d1(ml_out[1][1]), d1(ml_out[1][2]))
```

```python
import functools

import jax
import jax.numpy as jnp
from jax import lax
from jax.experimental import pallas as pl
from jax.experimental.pallas import tpu as pltpu

F32 = jnp.float32
BF16 = jnp.bfloat16

LANE = 128
SUBLANE = 8
VMEM_LIMIT_BYTES = 56 * 1024 * 1024
CHUNK = 64
ROW_TILE = 512
PROJ_ROW_TILE = 1024
FF_TILE = 512
COL_TILE = 512
MERGE_ROW_TILE = 1024
MERGE_COL_TILE = 256
MISC_COLS = 512
MISC_IF = 384
RMS_EPS = 1e-6
RW_GN_EPS = 64e-5
ML_LN_EPS = 1e-5
NEG = -1e30


def _dot(a, b):
    return jnp.dot(a, b, preferred_element_type=F32)


def _dot_nt(a, b):
    return lax.dot_general(a, b, (((1,), (1,)), ((), ())), preferred_element_type=F32)


def _dot_tn(a, b):
    return lax.dot_general(a, b, (((0,), (0,)), ((), ())), preferred_element_type=F32)


def _split3(x):
    h1 = x.astype(BF16)
    r1 = x - h1.astype(F32)
    h2 = r1.astype(BF16)
    r2 = r1 - h2.astype(F32)
    return h1, h2, r2.astype(BF16)


def _cumsum_rows(tri, x):
    n = x.shape[1]
    y = _dot(tri, jnp.concatenate(_split3(x), axis=1))
    return y[:, :n] + y[:, n:2 * n] + y[:, 2 * n:]


def _segsum(x, seg):
    rows = x.shape[0]
    y = _dot(jnp.concatenate(_split3(x), axis=0), seg)
    return y[:rows] + y[rows:2 * rows] + y[2 * rows:]


def _sigmoid(z):
    return jax.nn.sigmoid(z)


def _softplus(z):
    return jnp.maximum(z, 0.0) + jnp.log1p(jnp.exp(-jnp.abs(z)))


def _rmsnorm(x, g):
    return x * lax.rsqrt(jnp.mean(x * x, axis=-1, keepdims=True) + RMS_EPS) * g


def _iota(shape, dim):
    return lax.broadcasted_iota(jnp.int32, shape, dim)


def _params(*semantics):
    return pltpu.CompilerParams(dimension_semantics=semantics, vmem_limit_bytes=VMEM_LIMIT_BYTES)


def _ffn_kernel(*refs, n_f, n_main, split_in, split_out, final_norm):
    n_x = 2 if split_in else 1
    n_o = 2 if split_out else 1
    x_refs = refs[:n_x]
    g_ref, wg_ref, wu_ref, wd_ref, fg_ref = refs[n_x:n_x + 5]
    o_refs = refs[n_x + 5:n_x + 5 + n_o]
    xn_ref, acc_ref = refs[n_x + 5 + n_o:]
    i = pl.program_id(0)
    j = pl.program_id(1)

    def load_x():
        if split_in:
            return jnp.where(i < n_main, x_refs[0][...], x_refs[1][...])
        return x_refs[0][...]

    @pl.when(j == 0)
    def _():
        xn_ref[...] = _rmsnorm(load_x(), g_ref[...]).astype(BF16)
        acc_ref[...] = jnp.zeros_like(acc_ref)

    xn = xn_ref[...]
    gate = _dot(xn, wg_ref[...])
    h = (gate * _sigmoid(gate) * _dot(xn, wu_ref[...])).astype(BF16)
    acc_ref[...] += _dot(h, wd_ref[...])

    def finish(o_ref):
        y = load_x() + 0.5 * acc_ref[...]
        if final_norm:
            y = _rmsnorm(y, fg_ref[...])
        o_ref[...] = y

    last = j == n_f - 1
    if split_out:
        pl.when(last & (i < n_main))(lambda: finish(o_refs[0]))
        pl.when(last & (i >= n_main))(lambda: finish(o_refs[1]))
    else:
        pl.when(last)(lambda: finish(o_refs[0]))


def _ffn(xs, norm_g, wg, wu, wd, final_g, *, split_out, final_norm):
    split_in = len(xs) == 2
    d = xs[-1].shape[-1]
    if split_in:
        b, t = xs[0].shape[:2]
        n = b * t + xs[1].shape[0]
    else:
        n = xs[0].shape[0]
        b, t = split_out if split_out else (1, n)
    n_f = wd.shape[0] // FF_TILE
    tpb = t // ROW_TILE
    nm = b * tpb
    kern = functools.partial(_ffn_kernel, n_f=n_f, n_main=nm, split_in=split_in,
                             split_out=bool(split_out), final_norm=final_norm)
    whole = pl.BlockSpec((ROW_TILE, d), lambda i, j: (i, 0))
    main = pl.BlockSpec((None, ROW_TILE, d),
                        lambda i, j: (jnp.minimum(i, nm - 1) // tpb, jnp.minimum(i, nm - 1) % tpb, 0))
    extra = pl.BlockSpec((ROW_TILE, d), lambda i, j: (jnp.maximum(i - nm, 0), 0))
    return pl.pallas_call(
        kern,
        grid=(n // ROW_TILE, n_f),
        in_specs=([main, extra] if split_in else [whole]) + [
            pl.BlockSpec((1, d), lambda i, j: (0, 0)),
            pl.BlockSpec((d, FF_TILE), lambda i, j: (0, j)),
            pl.BlockSpec((d, FF_TILE), lambda i, j: (0, j)),
            pl.BlockSpec((FF_TILE, d), lambda i, j: (j, 0)),
            pl.BlockSpec((1, d), lambda i, j: (0, 0)),
        ],
        out_specs=[main, extra] if split_out else whole,
        out_shape=([jax.ShapeDtypeStruct((b, t, d), F32), jax.ShapeDtypeStruct((n - b * t, d), F32)]
                   if split_out else jax.ShapeDtypeStruct((n, d), F32)),
        scratch_shapes=[pltpu.VMEM((ROW_TILE, d), BF16), pltpu.VMEM((ROW_TILE, d), F32)],
        compiler_params=_params("arbitrary", "arbitrary"),
        name="ffn_final" if final_norm else "ffn",
    )(*xs, norm_g, wg, wu, wd, final_g)


def _proj_kernel(x_ref, g_ref, w_ref, o_ref, xn_ref):
    @pl.when(pl.program_id(1) == 0)
    def _():
        xn_ref[...] = _rmsnorm(x_ref[...], g_ref[...]).astype(BF16)

    o_ref[...] = _dot(xn_ref[...], w_ref[...])


def _proj(x, norm_g, w):
    n, d = x.shape
    cols = w.shape[1]
    tm = PROJ_ROW_TILE
    return pl.pallas_call(
        _proj_kernel,
        grid=(n // tm, cols // COL_TILE),
        in_specs=[
            pl.BlockSpec((tm, d), lambda i, j: (i, 0)),
            pl.BlockSpec((1, d), lambda i, j: (0, 0)),
            pl.BlockSpec((d, COL_TILE), lambda i, j: (0, j)),
        ],
        out_specs=pl.BlockSpec((tm, COL_TILE), lambda i, j: (i, j)),
        out_shape=jax.ShapeDtypeStruct((n, cols), F32),
        scratch_shapes=[pltpu.VMEM((tm, d), BF16)],
        compiler_params=_params("parallel", "arbitrary"),
        name="in_proj",
    )(x, norm_g, w)


def _merge_kernel(x_ref, yrw_ref, yml_ref, ga_ref, gb_ref, wrw_ref, wml_ref, wo_ref, o_ref, *, n_j):
    j = pl.program_id(1)

    @pl.when(j == 0)
    def _():
        o_ref[...] = jnp.zeros_like(o_ref)

    merged = (_sigmoid(ga_ref[...]) * _dot(yrw_ref[...], wrw_ref[...])
              + _sigmoid(gb_ref[...]) * _dot(yml_ref[...], wml_ref[...]))
    o_ref[...] += _dot(merged.astype(BF16), wo_ref[...])

    @pl.when(j == n_j - 1)
    def _():
        o_ref[...] = x_ref[...] + o_ref[...]


def _merge(x, y_rw, y_ml, proj, w_rw, w_ml, w_out):
    n, d = x.shape
    w = y_rw.shape[1]
    tm, tc = MERGE_ROW_TILE, MERGE_COL_TILE
    n_j = d // tc
    kern = functools.partial(_merge_kernel, n_j=n_j)
    return pl.pallas_call(
        kern,
        grid=(n // tm, n_j),
        in_specs=[
            pl.BlockSpec((tm, d), lambda i, j: (i, 0)),
            pl.BlockSpec((tm, w), lambda i, j: (i, 0)),
            pl.BlockSpec((tm, w), lambda i, j: (i, 0)),
            pl.BlockSpec((tm, tc), lambda i, j: (i, j)),
            pl.BlockSpec((tm, tc), lambda i, j: (i, n_j + j)),
            pl.BlockSpec((w, tc), lambda i, j: (0, j)),
            pl.BlockSpec((w, tc), lambda i, j: (0, j)),
            pl.BlockSpec((tc, d), lambda i, j: (j, 0)),
        ],
        out_specs=pl.BlockSpec((tm, d), lambda i, j: (i, 0)),
        out_shape=jax.ShapeDtypeStruct((n, d), F32),
        compiler_params=_params("parallel", "arbitrary"),
        name="merge",
    )(x, y_rw, y_ml, proj, proj, w_rw, w_ml, w_out)


def _each(f, *lists):
    return [f(*xs) for xs in zip(*lists)]


def _rwkv_pairs(r, k, v, a, lw, g, s0, kkw, kaw, rkw, lnw, lnb, cst):
    L = r[0].shape[0]
    tri, seg, head0, strict, incl, col_head0, eye2, blockdiag = cst
    bf = lambda x: x.astype(BF16)

    kk_ = _each(lambda k_, w_: k_ * w_, k, kkw)
    kn = _each(lambda x: _segsum(x * x, seg), kk_)
    cum = _each(lambda x: _cumsum_rows(tri, x), lw)
    kk = _each(lambda x, n_: x / jnp.maximum(jnp.sqrt(n_), 1e-12), kk_, kn)
    k2 = _each(lambda k_, a_, w_: k_ * (1.0 + (a_ - 1.0) * w_), k, a, kaw)
    kka = _each(lambda x, a_: x * a_, kk, a)
    c_end = [c[L - 1:L, :] for c in cum]
    e_inv = [jnp.exp(-c) for c in cum]
    e_rem = _each(lambda c, ce: jnp.exp(ce - c), cum, c_end)
    at = _each(lambda x, c, l_: bf(-x * jnp.exp(c - l_)), kk, cum, lw)
    rt = _each(lambda x, c: bf(x * jnp.exp(c)), r, cum)
    bt = _each(lambda x, e: bf(x * e), kka, e_inv)
    kt = _each(lambda x, e: bf(x * e), k2, e_inv)
    vb = [bf(x) for x in v]
    zero = jnp.zeros_like(vb[0])

    def by_head(x):
        return jnp.concatenate([jnp.where(head0, x, zero), jnp.where(head0, zero, x)], axis=0)

    lhs = _each(lambda x, y_: jnp.concatenate([x, y_], axis=0), at, rt)
    ab = _each(lambda l_, x: _dot_nt(l_, by_head(x)), lhs, bt)
    ak = _each(lambda l_, x: _dot_nt(l_, by_head(x)), lhs, kt)
    a_rb = [jnp.where(incl, x[L:], 0.0) for x in ab]
    a_ak = [jnp.where(strict, x[:L], 0.0) for x in ak]
    a_rk = [jnp.where(incl, x[L:], 0.0) for x in ak]

    def blockdiag2(x):
        n_cat = jnp.where(strict, x[:L], 0.0)
        return jnp.concatenate([jnp.where(col_head0, n_cat, 0.0), jnp.where(col_head0, 0.0, n_cat)], axis=0)

    p = [blockdiag2(x) for x in ab]
    t_inv = [eye2 + x for x in p]
    for _ in range(L.bit_length() - 2):
        p = [_dot(bf(x), bf(x)) for x in p]
        t_inv = _each(lambda t_, x: t_ + _dot(bf(t_), bf(x)), t_inv, p)

    sb = [bf(x) for x in s0]
    vs = [by_head(x) for x in vb]
    x0 = _each(lambda a_, s_, m_, v_: _dot_nt(a_, s_) + _dot(bf(m_), v_), at, sb, a_ak, vs)
    us = _each(lambda t_, x: _dot(bf(t_), by_head(bf(x))), t_inv, x0)
    y = _each(lambda r_, s_, m1, m2, u_, v_: _dot_nt(r_, s_) + _dot(
        bf(jnp.concatenate([m1, m2], axis=1)), jnp.concatenate([bf(u_), v_], axis=0)),
        rt, sb, a_rb, a_rk, us, vs)
    upd = _each(lambda u_, v_, x1, x2, e: _dot_tn(
        jnp.concatenate([bf(u_[:L] + u_[L:]), v_], axis=0),
        jnp.concatenate([bf(x1 * e), bf(x2 * e)], axis=0)), us, vb, kka, k2, e_rem)
    s_new = _each(lambda s_, ce, u_: s_ * jnp.exp(ce) + jnp.where(blockdiag, u_, 0.0), s0, c_end, upd)

    hd = float(LANE // 2)
    mean = [_segsum(x, seg) / hd for x in y]
    bsum = _each(lambda r_, k_, w_: _segsum(r_ * k_ * w_, seg), r, k2, rkw)
    yc = _each(lambda x, m_: x - m_, y, mean)
    var = [_segsum(x * x, seg) / hd for x in yc]
    out = _each(lambda x, v_, w_, b_, bs, vv, g_: (x * lax.rsqrt(v_ + RW_GN_EPS) * w_ + b_ + bs * vv) * g_,
                yc, var, lnw, lnb, bsum, v, g)
    return out, s_new


def _rwkv_kernel(rkv_ref, misc_ref, sh_rkv_ref, sh_misc_ref, s0_ref,
                 mu_rkv_ref, mu_misc_ref, w0_ref, w2_ref, a0_ref, a2_ref, g2_ref,
                 kk_ref, ka_ref, rk_ref, lnw_ref, lnb_ref, y_prev_ref,
                 y_ref, s_ref,
                 carry_rkv, carry_misc, r_s, k_s, v_s, a_s, lw_s, g_s,
                 *, rows, lead, width):
    del y_prev_ref
    L = CHUNK
    c = pl.program_id(1)

    @pl.when(c == 0)
    def _():
        carry_rkv[...] = sh_rkv_ref[0]
        carry_misc[...] = sh_misc_ref[0]
        s_ref[...] = s0_ref[...]

    p = rkv_ref[...]
    pm = misc_ref[:, :3 * LANE]
    if rows < L:
        p = jnp.concatenate([jnp.zeros((L - rows, p.shape[1]), F32), p], axis=0)
        pm = jnp.concatenate([jnp.zeros((L - rows, pm.shape[1]), F32), pm], axis=0)
    row = _iota((L, 1), 0)
    first = jnp.where(c == 0, lead, 0)
    valid = row >= first
    p = jnp.where(valid, p, 0.0)
    pm = jnp.where(valid, pm, 0.0)

    def shift_mix(cur, carry_ref, mu):
        prev = jnp.where(row == first, carry_ref[...], pltpu.roll(cur, 1, axis=0))
        carry_ref[...] = cur[L - 1:L, :]
        return jnp.where(valid, cur + mu * (prev - cur), 0.0)

    u = shift_mix(p, carry_rkv, mu_rkv_ref[...])
    um = shift_mix(pm, carry_misc, mu_misc_ref[...])

    lora = um[:, :LANE]
    wl = w0_ref[...] + _dot(jnp.tanh(lora).astype(BF16), w2_ref[...])
    wlog = -_softplus(-wl) - 0.5
    lw_s[...] = jnp.where(valid, -jnp.exp(wlog), 0.0)
    a_s[...] = _sigmoid(a0_ref[...] + _dot(lora.astype(BF16), a2_ref[...]))
    g_s[...] = _dot(_sigmoid(um[:, LANE:]).astype(BF16), g2_ref[...])
    r_s[...] = u[:, :width]
    k_s[...] = u[:, width:2 * width]
    v_s[...] = u[:, 2 * width:]

    lane = _iota((1, LANE), 1)
    half = LANE // 2
    col2 = _iota((L, 2 * L), 1)
    t_i = _iota((L, 2 * L), 0)
    s_i = col2 & (L - 1)
    cst = (
        (_iota((L, L), 1) <= _iota((L, L), 0)).astype(BF16),
        ((_iota((LANE, LANE), 0) < half) == (_iota((LANE, LANE), 1) < half)).astype(BF16),
        lane < half,
        s_i < t_i, s_i <= t_i,
        col2 < L,
        (_iota((2 * L, 2 * L), 0) == _iota((2 * L, 2 * L), 1)).astype(F32),
        (_iota((LANE, LANE), 0) < half) == (_iota((LANE, LANE), 1) < half),
    )
    sls = [slice(j * LANE, (j + 1) * LANE) for j in range(width // LANE)]
    pick = lambda ref: [ref[:, sl] for sl in sls]
    outs, s_new = _rwkv_pairs(pick(r_s), pick(k_s), pick(v_s), pick(a_s), pick(lw_s), pick(g_s),
                              [s_ref[0, j] for j in range(len(sls))],
                              pick(kk_ref), pick(ka_ref), pick(rk_ref), pick(lnw_ref), pick(lnb_ref), cst)
    for j, sl in enumerate(sls):
        s_ref[0, j] = s_new[j]
        y_ref[:, sl] = outs[j][L - rows:, :].astype(y_ref.dtype)


def _rwkv(proj, y_prev, shift_rkv, shift_misc, s0, prm, *, batch, n_chunks, rows, seq_row, width, lead, d_model):
    n = proj.shape[0]
    n_pairs = width // LANE
    rkv_blk = (2 * d_model + 2 * width) // (3 * width)
    misc_blk = (proj.shape[1] - MISC_COLS) // MISC_COLS
    kern = functools.partial(_rwkv_kernel, rows=rows, lead=lead, width=width)
    full = lambda shape: pl.BlockSpec(shape, lambda b, c: (0,) * len(shape))
    in_specs = [
        pl.BlockSpec((rows, 3 * width), lambda b, c: (seq_row(b, c), rkv_blk)),
        pl.BlockSpec((rows, MISC_COLS), lambda b, c: (seq_row(b, c), misc_blk)),
        pl.BlockSpec((1, 1, 3 * width), lambda b, c: (b, 0, 0)),
        pl.BlockSpec((1, 1, 3 * LANE), lambda b, c: (b, 0, 0)),
        pl.BlockSpec((1, n_pairs, LANE, LANE), lambda b, c: (b, 0, 0, 0)),
        full((1, 3 * width)), full((1, 3 * LANE)),
        full((1, width)), full((LANE, width)), full((1, width)), full((LANE, width)), full((2 * LANE, width)),
        full((1, width)), full((1, width)), full((1, width)), full((1, width)), full((1, width)),
        pl.BlockSpec(memory_space=pl.ANY),
    ]
    out_specs = [
        pl.BlockSpec((rows, width), lambda b, c: (seq_row(b, c), 0)),
        pl.BlockSpec((1, n_pairs, LANE, LANE), lambda b, c: (b, 0, 0, 0)),
    ]
    out_shape = [
        jax.ShapeDtypeStruct((n, width), BF16),
        jax.ShapeDtypeStruct((batch, n_pairs, LANE, LANE), F32),
    ]
    scratch = [pltpu.VMEM((1, 3 * width), F32), pltpu.VMEM((1, 3 * LANE), F32)]
    scratch += [pltpu.VMEM((CHUNK, width), F32) for _ in range(6)]
    return pl.pallas_call(
        kern, grid=(batch, n_chunks), in_specs=in_specs, out_specs=out_specs, out_shape=out_shape,
        scratch_shapes=scratch, input_output_aliases={len(in_specs) - 1: 0},
        compiler_params=_params("arbitrary", "arbitrary"), name="rwkv7_chunk",
    )(proj, proj, shift_rkv, shift_misc, s0, *prm, y_prev)


def _mlstm_kernel(qk_ref, v_ref, o_ref, misc_ref, conv0_ref, c0_ref, n0_ref, m0_ref,
                  cw_ref, cb_ref, ifb_ref, nw_ref, y_prev_ref,
                  y_ref, c_ref, n_ref, m_ref,
                  ext, *, rows, lead, heads, hd):
    del y_prev_ref
    L = CHUNK
    c = pl.program_id(1)
    width = heads * hd

    @pl.when(c == 0)
    def _():
        c_ref[...] = c0_ref[...]
        n_ref[...] = n0_ref[...]
        m_ref[...] = m0_ref[...]
        ext[0:SUBLANE, :] = jnp.zeros((SUBLANE, 2 * width), F32)

    def chunk_rows(x):
        if rows < L:
            return jnp.concatenate([jnp.zeros((L - rows, x.shape[1]), F32), x], axis=0)
        return x

    row = _iota((L, 1), 0)
    valid = row >= jnp.where(c == 0, lead, 0)

    ext[SUBLANE:SUBLANE + L, :] = jnp.where(valid, chunk_rows(qk_ref[...]), 0.0)

    @pl.when(c == 0)
    def _():
        ext[lead:lead + SUBLANE, :] = conv0_ref[0]

    conv = cb_ref[...]
    for j in range(4):
        conv = conv + cw_ref[j:j + 1, :] * ext[SUBLANE - 3 + j:SUBLANE - 3 + j + L, :]
    ext[0:SUBLANE, :] = ext[L:L + SUBLANE, :]
    qk = jnp.where(valid, conv * _sigmoid(conv), 0.0)
    v = jnp.where(valid, chunk_rows(v_ref[...]), 0.0)
    og = _sigmoid(chunk_rows(o_ref[...]))

    lane = _iota((1, LANE), 1)
    is_i = lane < heads
    z = chunk_rows(misc_ref[:, MISC_IF:MISC_IF + LANE]) + ifb_ref[...]
    logsig = jnp.minimum(z, 0.0) - jnp.log1p(jnp.exp(-jnp.abs(z)))
    gates = jnp.where(valid, jnp.where(is_i, z, logsig), jnp.where(is_i, NEG, 0.0))
    tri = (_iota((L, L), 1) <= _iota((L, L), 0)).astype(BF16)
    cum = _cumsum_rows(tri, jnp.where(is_i, 0.0, gates))
    pad = jnp.zeros((LANE - L, LANE), F32)
    gates_t = jnp.concatenate([gates, pad], axis=0).T
    cum_t = jnp.concatenate([cum, pad], axis=0).T
    causal = _iota((L, L), 1) <= _iota((L, L), 0)
    m_all = m_ref[0]

    for h in range(heads):
        sl = slice(h * hd, (h + 1) * hd)
        q = qk[:, sl]
        kx = qk[:, width + h * hd:width + (h + 1) * hd] * (hd ** -0.5)
        vh = v[:, sl]
        b_col = cum[:, heads + h:heads + h + 1]
        i_col = gates[:, h:h + 1]
        b_row = cum_t[heads + h:heads + h + 1, :L]
        i_row = gates_t[h:h + 1, :L]
        b_end = b_col[L - 1:L, :]
        m_prev = jnp.sum(jnp.where(lane == h, m_all, 0.0), axis=1, keepdims=True)

        log_inter = b_col + m_prev
        dmat = jnp.where(causal, b_col - b_row + i_row, NEG)
        m_q = jnp.maximum(log_inter, jnp.max(dmat, axis=-1, keepdims=True))
        w_inter = jnp.exp(log_inter - m_q)
        qb = q.astype(BF16)
        s = _dot_nt(qb, kx.astype(BF16)) * jnp.exp(dmat - m_q)
        c_h = c_ref[0, h]
        n_h = n_ref[0, h:h + 1, :]
        num = w_inter * _dot(qb, c_h.astype(BF16)) + _dot(s.astype(BF16), vh.astype(BF16))
        den = w_inter * jnp.sum(q * n_h, axis=-1, keepdims=True) + jnp.sum(s, axis=-1, keepdims=True)
        hcell = num / jnp.maximum(jnp.abs(den), jnp.exp(-m_q))

        g_col = b_end - b_col + i_col
        m_new = jnp.maximum(b_end + m_prev, jnp.max(g_col, axis=0, keepdims=True))
        a_st = jnp.exp(b_end + m_prev - m_new)
        wkk = jnp.exp(g_col - m_new) * kx
        c_ref[0, h] = a_st * c_h + _dot_tn(wkk.astype(BF16), vh.astype(BF16))
        n_ref[0, h:h + 1, :] = a_st * n_h + jnp.sum(wkk, axis=0, keepdims=True)
        m_all = jnp.where(lane == h, m_new, m_all)

        mu = jnp.mean(hcell, axis=-1, keepdims=True)
        hc = hcell - mu
        var = jnp.mean(hc * hc, axis=-1, keepdims=True)
        yh = hc * lax.rsqrt(var + ML_LN_EPS) * nw_ref[:, sl] * og[:, sl]
        y_ref[:, sl] = yh[L - rows:, :].astype(y_ref.dtype)

    m_ref[0] = m_all


def _mlstm(proj, y_prev, conv0, c0, n0, m0, prm, *, batch, n_chunks, rows, seq_row, heads, hd, lead, d_model):
    n = proj.shape[0]
    width = heads * hd
    qk_blk = (2 * d_model) // (2 * width)
    v_blk = (2 * d_model + 2 * width + 3 * (d_model - width)) // width
    misc_blk = (proj.shape[1] - MISC_COLS) // MISC_COLS
    kern = functools.partial(_mlstm_kernel, rows=rows, lead=lead, heads=heads, hd=hd)
    full = lambda shape: pl.BlockSpec(shape, lambda b, c: (0,) * len(shape))
    in_specs = [
        pl.BlockSpec((rows, 2 * width), lambda b, c: (seq_row(b, c), qk_blk)),
        pl.BlockSpec((rows, width), lambda b, c: (seq_row(b, c), v_blk)),
        pl.BlockSpec((rows, width), lambda b, c: (seq_row(b, c), v_blk + 1)),
        pl.BlockSpec((rows, MISC_COLS), lambda b, c: (seq_row(b, c), misc_blk)),
        pl.BlockSpec((1, SUBLANE, 2 * width), lambda b, c: (b, 0, 0)),
        pl.BlockSpec((1, heads, hd, hd), lambda b, c: (b, 0, 0, 0)),
        pl.BlockSpec((1, heads, hd), lambda b, c: (b, 0, 0)),
        pl.BlockSpec((1, 1, LANE), lambda b, c: (b, 0, 0)),
        full((4, 2 * width)), full((1, 2 * width)), full((1, LANE)), full((1, width)),
        pl.BlockSpec(memory_space=pl.ANY),
    ]
    out_specs = [
        pl.BlockSpec((rows, width), lambda b, c: (seq_row(b, c), 0)),
        pl.BlockSpec((1, heads, hd, hd), lambda b, c: (b, 0, 0, 0)),
        pl.BlockSpec((1, heads, hd), lambda b, c: (b, 0, 0)),
        pl.BlockSpec((1, 1, LANE), lambda b, c: (b, 0, 0)),
    ]
    out_shape = [
        jax.ShapeDtypeStruct((n, width), BF16),
        jax.ShapeDtypeStruct((batch, heads, hd, hd), F32),
        jax.ShapeDtypeStruct((batch, heads, hd), F32),
        jax.ShapeDtypeStruct((batch, 1, LANE), F32),
    ]
    return pl.pallas_call(
        kern, grid=(batch, n_chunks), in_specs=in_specs, out_specs=out_specs, out_shape=out_shape,
        scratch_shapes=[pltpu.VMEM((CHUNK + SUBLANE, 2 * width), F32)],
        input_output_aliases={len(in_specs) - 1: 0},
        compiler_params=_params("arbitrary", "arbitrary"), name="mlstm_chunk",
    )(proj, proj, proj, proj, conv0, c0, n0, m0, *prm, y_prev)


def _pad_rows(w, rows):
    return jnp.pad(w, ((0, rows - w.shape[0]), (0, 0)))


def _swiglu_weights(w_gate, w_up, w_down):
    f = w_gate.shape[1]
    fp = -(-f // FF_TILE) * FF_TILE
    pad_cols = lambda w: jnp.pad(w, ((0, 0), (0, fp - f))).astype(BF16)
    return pad_cols(w_gate), pad_cols(w_up), _pad_rows(w_down, fp).astype(BF16)


def kernel(x_prompt, x_sample, state_rwkv_shift, state_rwkv_wkv, state_mlstm_conv, state_mlstm_C,
           state_mlstm_n, state_mlstm_m, meta_tokens, ffn1_norm, ffn1_w_gate, ffn1_w_up, ffn1_w_down,
           mix_norm, w_in, rw_mu, rw_w0, rw_w2, rw_a0, rw_a2, rw_g2, rw_kk, rw_ka, rw_rk, rw_ln_w, rw_ln_b,
           ml_conv_w, ml_conv_b, ml_i_b, ml_f_b, ml_norm_w, w_br_rw, w_br_ml, w_out,
           ffn2_norm, ffn2_w_gate, ffn2_w_up, ffn2_w_down, final_norm):
    assert ffn1_norm.shape[0] == 1, "single-layer trunk"
    B, T, D = x_prompt.shape
    Bs, Ts, _ = x_sample.shape
    n_meta = meta_tokens.shape[0]
    W = rw_w0.shape[-1]
    dl, al, gl = rw_w2.shape[1], rw_a2.shape[1], rw_g2.shape[1]
    rw_heads, rw_hd = rw_rk.shape[1], rw_rk.shape[2]
    Wm = ml_norm_w.shape[-1]
    H = ml_i_b.shape[-1]
    hd = Wm // H
    K = ml_conv_w.shape[1]
    L = CHUNK
    assert rw_hd == LANE // 2 and dl == LANE // 2 and al == LANE // 2 and gl <= 2 * LANE
    assert K == 4 and hd % LANE == 0 and 2 * H <= LANE and W == Wm and 2 * W == D
    assert Ts <= L and Ts % (2 * SUBLANE) == 0 and D % COL_TILE == 0 and (11 * D // 2) % MISC_COLS == 0

    lead = (-n_meta) % L
    head_rows = lead + n_meta
    n_head = head_rows // L
    n_chunks = n_head + T // L
    lead_s = L - Ts
    n_main = B * T
    sample0 = n_main + B * head_rows
    N = sample0 + Bs * Ts
    assert T % L == 0 and T % ROW_TILE == 0 and N % PROJ_ROW_TILE == 0
    head = jnp.concatenate([jnp.zeros((lead, D), F32), meta_tokens.astype(F32)], axis=0)
    x_extra = jnp.concatenate([jnp.broadcast_to(head[None], (B, head_rows, D)).reshape(B * head_rows, D),
                               x_sample.reshape(Bs * Ts, D)], axis=0)

    ffn1_w = _swiglu_weights(ffn1_w_gate[0], ffn1_w_up[0], ffn1_w_down[0])
    ffn2_w = _swiglu_weights(ffn2_w_gate[0], ffn2_w_up[0], ffn2_w_down[0])
    wi = w_in[0]
    o_lora = 3 * W
    o_qk = o_lora + dl + al + gl
    o_v = o_qk + 2 * Wm
    o_o = o_v + Wm
    o_i = o_o + Wm
    o_gate = o_i + 2 * H
    zc = lambda n: jnp.zeros((D, n), wi.dtype)
    misc_w = jnp.concatenate([wi[:, o_lora:o_qk], zc(MISC_IF - (dl + al + gl)),
                              wi[:, o_i:o_gate], zc(MISC_COLS - MISC_IF - 2 * H)], axis=1)
    w_proj = jnp.concatenate([wi[:, o_gate:], wi[:, o_qk:o_v], wi[:, :o_lora], wi[:, o_v:o_i], misc_w],
                             axis=1).astype(BF16)

    def misc_vec(v_lora, fill=0.0):
        return jnp.pad(v_lora, [(0, 0)] * (v_lora.ndim - 1) + [(0, 3 * LANE - v_lora.shape[-1])],
                       constant_values=fill)

    row = lambda v: v.reshape(1, -1).astype(F32)
    rw_prm = (
        row(rw_mu[0, :o_lora]), misc_vec(row(rw_mu[0, o_lora:])),
        row(rw_w0[0]), _pad_rows(rw_w2[0], LANE).astype(BF16),
        row(rw_a0[0]), jnp.concatenate([jnp.zeros((dl, W), F32), rw_a2[0]], axis=0).astype(BF16),
        _pad_rows(rw_g2[0], 2 * LANE).astype(BF16),
        row(rw_kk[0]), row(rw_ka[0]), row(rw_rk[0]), row(rw_ln_w[0]), row(rw_ln_b[0]),
    )
    ifb = jnp.pad(jnp.concatenate([ml_i_b[0], ml_f_b[0]]).reshape(1, 2 * H), ((0, 0), (0, LANE - 2 * H)))
    ml_prm = (ml_conv_w[0].astype(F32), row(ml_conv_b[0]), ifb.astype(F32), row(ml_norm_w[0]))

    def rw_state_in(shift, wkv):
        b = shift.shape[0]
        s = wkv.reshape(b, rw_heads // 2, 2, rw_hd, rw_hd)
        z = jnp.zeros_like(s[:, :, 0])
        bd = jnp.concatenate([jnp.concatenate([s[:, :, 0], z], axis=-1),
                              jnp.concatenate([z, s[:, :, 1]], axis=-1)], axis=-2)
        return shift[:, None, :o_lora], misc_vec(shift[:, None, o_lora:]), bd

    def rw_state_out(bd):
        h = rw_hd
        return jnp.stack([bd[:, :, :h, :h], bd[:, :, h:, h:]], axis=2).reshape(bd.shape[0], rw_heads, h, h)

    def conv_in(buf):
        return jnp.pad(buf, ((0, 0), (SUBLANE - (K - 1), 0), (0, 0)))

    def m_in(m):
        return jnp.pad(m, ((0, 0), (0, LANE - H)))[:, None, :]

    x1 = _ffn((x_prompt, x_extra), row(ffn1_norm[0]), *ffn1_w, row(final_norm),
              split_out=None, final_norm=False)
    proj = _proj(x1, row(mix_norm[0]), w_proj)

    zeros = lambda *s: jnp.zeros(s, F32)
    main_chunks = T // L

    def prompt_row(b, c):
        return jnp.where(c < n_head, n_main // L + b * n_head + c, b * main_chunks + c - n_head)

    seqs = (
        dict(batch=B, n_chunks=n_chunks, rows=L, seq_row=prompt_row, lead=lead),
        dict(batch=Bs, n_chunks=1, rows=Ts, seq_row=lambda b, c: sample0 // Ts + b, lead=lead_s),
    )
    rw_states = (
        rw_state_in(zeros(B, o_qk), zeros(B, rw_heads, rw_hd, rw_hd)),
        rw_state_in(state_rwkv_shift[0], state_rwkv_wkv[0]),
    )
    ml_states = (
        (conv_in(zeros(B, K - 1, 2 * Wm)), zeros(B, H, hd, hd), zeros(B, H, hd), m_in(zeros(B, H))),
        (conv_in(state_mlstm_conv[0]), state_mlstm_C[0], state_mlstm_n[0], m_in(state_mlstm_m[0])),
    )
    y_rw = jnp.zeros((N, W), BF16)
    y_ml = jnp.zeros((N, Wm), BF16)
    rw_out, ml_out = [], []
    for seq, rws, mls in zip(seqs, rw_states, ml_states):
        y_rw, s_end = _rwkv(proj, y_rw, *rws, rw_prm, width=W, d_model=D, **seq)
        y_ml, c_end, n_end, m_end = _mlstm(proj, y_ml, *mls, ml_prm, heads=H, hd=hd, d_model=D, **seq)
        rw_out.append(rw_state_out(s_end))
        ml_out.append((c_end, n_end, m_end[:, 0, :H]))

    x2 = _merge(x1, y_rw, y_ml, proj, w_br_rw[0].astype(BF16), w_br_ml[0].astype(BF16), w_out[0].astype(BF16))
    y_prompt, y_extra = _ffn((x2,), row(ffn2_norm[0]), *ffn2_w, row(final_norm),
                             split_out=(B, T), final_norm=True)

    c_rkv = 2 * D + 2 * Wm
    c_misc = proj.shape[1] - MISC_COLS

    def seq_states(row0, batch, t_len):
        last = row0 + (jnp.arange(batch)[:, None] + 1) * t_len - (K - 1) + jnp.arange(K - 1)[None, :]
        tail = proj[last.reshape(-1)].reshape(batch, K - 1, -1)
        shift = jnp.concatenate([tail[:, -1, c_rkv:c_rkv + 3 * W],
                                 tail[:, -1, c_misc:c_misc + dl + al + gl]], axis=-1)
        return shift, tail[:, :, 2 * D:2 * D + 2 * Wm]

    p_shift, p_conv = seq_states(0, B, T)
    s_shift, s_conv = seq_states(sample0, Bs, Ts)
    y_sample = y_extra[B * head_rows:].reshape(Bs, Ts, D)
    d1 = lambda a: a[None]
    return (y_prompt, y_sample,
            d1(p_shift), d1(rw_out[0]), d1(p_conv), d1(ml_out[0][0]), d1(ml_out[0][1]), d1(ml_out[0][2]),
            d1(s_shift), d1(rw_out[1]), d1(s_conv), d1(ml_out[1][0]), d1(ml_out[1][1]), d1(ml_out[1][2]))
```

```python
import functools

import jax
import jax.numpy as jnp
from jax import lax
from jax.experimental import pallas as pl
from jax.experimental.pallas import tpu as pltpu

F32 = jnp.float32
BF16 = jnp.bfloat16

LANE = 128
SUBLANE = 8
VMEM_LIMIT_BYTES = 56 * 1024 * 1024
CHUNK = 64
ROW_TILE = 512
PROJ_ROW_TILE = 1024
FF_TILE = 512
COL_TILE = 512
MERGE_ROW_TILE = 1024
MERGE_COL_TILE = 256
MISC_COLS = 512
MISC_IF = 384
RMS_EPS = 1e-6
RW_GN_EPS = 64e-5
ML_LN_EPS = 1e-5
NEG = -1e30
DECAY_SCALE = 0.6065306597126334


def _dot(a, b):
    return jnp.dot(a, b, preferred_element_type=F32)


def _dot_nt(a, b):
    return lax.dot_general(a, b, (((1,), (1,)), ((), ())), preferred_element_type=F32)


def _dot_tn(a, b):
    return lax.dot_general(a, b, (((0,), (0,)), ((), ())), preferred_element_type=F32)


def _split3(x):
    h1 = x.astype(BF16)
    r1 = x - h1.astype(F32)
    h2 = r1.astype(BF16)
    r2 = r1 - h2.astype(F32)
    return h1, h2, r2.astype(BF16)


def _cumsum_rows(tri, x):
    n = x.shape[1]
    y = _dot(tri, jnp.concatenate(_split3(x), axis=1))
    return y[:, :n] + y[:, n:2 * n] + y[:, 2 * n:]


def _segsum(x, seg):
    rows = x.shape[0]
    h1, h2, _ = _split3(x)
    y = _dot(jnp.concatenate([h1, h2], axis=0), seg)
    return y[:rows] + y[rows:]


def _sigmoid(z):
    return jax.nn.sigmoid(z)


def _rmsnorm(x, g):
    return x * lax.rsqrt(jnp.mean(x * x, axis=-1, keepdims=True) + RMS_EPS) * g


def _iota(shape, dim):
    return lax.broadcasted_iota(jnp.int32, shape, dim)


def _params(*semantics):
    return pltpu.CompilerParams(dimension_semantics=semantics, vmem_limit_bytes=VMEM_LIMIT_BYTES)


def _ffn_kernel(*refs, n_f, n_main, split_in, split_out, final_norm):
    n_x = 2 if split_in else 1
    n_o = 2 if split_out else 1
    x_refs = refs[:n_x]
    g_ref, wg_ref, wu_ref, wd_ref, fg_ref = refs[n_x:n_x + 5]
    o_refs = refs[n_x + 5:n_x + 5 + n_o]
    xn_ref, acc_ref = refs[n_x + 5 + n_o:]
    i = pl.program_id(0)
    j = pl.program_id(1)

    def load_x():
        if split_in:
            return jnp.where(i < n_main, x_refs[0][...], x_refs[1][...])
        return x_refs[0][...]

    @pl.when(j == 0)
    def _():
        xn_ref[...] = _rmsnorm(load_x(), g_ref[...]).astype(BF16)
        acc_ref[...] = jnp.zeros_like(acc_ref)

    xn = xn_ref[...]
    gate = _dot(xn, wg_ref[...])
    h = (gate * _sigmoid(gate) * _dot(xn, wu_ref[...])).astype(BF16)
    acc_ref[...] += _dot(h, wd_ref[...])

    def finish(o_ref):
        y = load_x() + 0.5 * acc_ref[...]
        if final_norm:
            y = _rmsnorm(y, fg_ref[...])
        o_ref[...] = y

    last = j == n_f - 1
    if split_out:
        pl.when(last & (i < n_main))(lambda: finish(o_refs[0]))
        pl.when(last & (i >= n_main))(lambda: finish(o_refs[1]))
    else:
        pl.when(last)(lambda: finish(o_refs[0]))


def _ffn(xs, norm_g, wg, wu, wd, final_g, *, split_out, final_norm):
    split_in = len(xs) == 2
    d = xs[-1].shape[-1]
    if split_in:
        b, t = xs[0].shape[:2]
        n = b * t + xs[1].shape[0]
    else:
        n = xs[0].shape[0]
        b, t = split_out if split_out else (1, n)
    n_f = wd.shape[0] // FF_TILE
    tpb = t // ROW_TILE
    nm = b * tpb
    kern = functools.partial(_ffn_kernel, n_f=n_f, n_main=nm, split_in=split_in,
                             split_out=bool(split_out), final_norm=final_norm)
    whole = pl.BlockSpec((ROW_TILE, d), lambda i, j: (i, 0))
    main = pl.BlockSpec((None, ROW_TILE, d),
                        lambda i, j: (jnp.minimum(i, nm - 1) // tpb, jnp.minimum(i, nm - 1) % tpb, 0))
    extra = pl.BlockSpec((ROW_TILE, d), lambda i, j: (jnp.maximum(i - nm, 0), 0))
    return pl.pallas_call(
        kern,
        grid=(n // ROW_TILE, n_f),
        in_specs=([main, extra] if split_in else [whole]) + [
            pl.BlockSpec((1, d), lambda i, j: (0, 0)),
            pl.BlockSpec((d, FF_TILE), lambda i, j: (0, j)),
            pl.BlockSpec((d, FF_TILE), lambda i, j: (0, j)),
            pl.BlockSpec((FF_TILE, d), lambda i, j: (j, 0)),
            pl.BlockSpec((1, d), lambda i, j: (0, 0)),
        ],
        out_specs=[main, extra] if split_out else whole,
        out_shape=([jax.ShapeDtypeStruct((b, t, d), F32), jax.ShapeDtypeStruct((n - b * t, d), F32)]
                   if split_out else jax.ShapeDtypeStruct((n, d), F32)),
        scratch_shapes=[pltpu.VMEM((ROW_TILE, d), BF16), pltpu.VMEM((ROW_TILE, d), F32)],
        compiler_params=_params("arbitrary", "arbitrary"),
        name="ffn_final" if final_norm else "ffn",
    )(*xs, norm_g, wg, wu, wd, final_g)


def _proj_kernel(x_ref, g_ref, w_ref, o_ref, xn_ref):
    @pl.when(pl.program_id(1) == 0)
    def _():
        xn_ref[...] = _rmsnorm(x_ref[...], g_ref[...]).astype(BF16)

    o_ref[...] = _dot(xn_ref[...], w_ref[...])


def _proj(x, norm_g, w):
    n, d = x.shape
    cols = w.shape[1]
    tm = PROJ_ROW_TILE
    return pl.pallas_call(
        _proj_kernel,
        grid=(n // tm, cols // COL_TILE),
        in_specs=[
            pl.BlockSpec((tm, d), lambda i, j: (i, 0)),
            pl.BlockSpec((1, d), lambda i, j: (0, 0)),
            pl.BlockSpec((d, COL_TILE), lambda i, j: (0, j)),
        ],
        out_specs=pl.BlockSpec((tm, COL_TILE), lambda i, j: (i, j)),
        out_shape=jax.ShapeDtypeStruct((n, cols), F32),
        scratch_shapes=[pltpu.VMEM((tm, d), BF16)],
        compiler_params=_params("parallel", "arbitrary"),
        name="in_proj",
    )(x, norm_g, w)


def _merge_kernel(x_ref, yrw_ref, yml_ref, ga_ref, gb_ref, wrw_ref, wml_ref, wo_ref, o_ref, *, n_j):
    j = pl.program_id(1)

    @pl.when(j == 0)
    def _():
        o_ref[...] = jnp.zeros_like(o_ref)

    merged = (_sigmoid(ga_ref[...]) * _dot(yrw_ref[...], wrw_ref[...])
              + _sigmoid(gb_ref[...]) * _dot(yml_ref[...], wml_ref[...]))
    o_ref[...] += _dot(merged.astype(BF16), wo_ref[...])

    @pl.when(j == n_j - 1)
    def _():
        o_ref[...] = x_ref[...] + o_ref[...]


def _merge(x, y_rw, y_ml, proj, w_rw, w_ml, w_out):
    n, d = x.shape
    w = y_rw.shape[1]
    tm, tc = MERGE_ROW_TILE, MERGE_COL_TILE
    n_j = d // tc
    kern = functools.partial(_merge_kernel, n_j=n_j)
    return pl.pallas_call(
        kern,
        grid=(n // tm, n_j),
        in_specs=[
            pl.BlockSpec((tm, d), lambda i, j: (i, 0)),
            pl.BlockSpec((tm, w), lambda i, j: (i, 0)),
            pl.BlockSpec((tm, w), lambda i, j: (i, 0)),
            pl.BlockSpec((tm, tc), lambda i, j: (i, j)),
            pl.BlockSpec((tm, tc), lambda i, j: (i, n_j + j)),
            pl.BlockSpec((w, tc), lambda i, j: (0, j)),
            pl.BlockSpec((w, tc), lambda i, j: (0, j)),
            pl.BlockSpec((tc, d), lambda i, j: (j, 0)),
        ],
        out_specs=pl.BlockSpec((tm, d), lambda i, j: (i, 0)),
        out_shape=jax.ShapeDtypeStruct((n, d), F32),
        compiler_params=_params("parallel", "arbitrary"),
        name="merge",
    )(x, y_rw, y_ml, proj, proj, w_rw, w_ml, w_out)


def _each(f, *lists):
    return [f(*xs) for xs in zip(*lists)]


def _rwkv_pairs(r, k, v, a, lw, g, s0, kkw, kaw, rkw, lnw, lnb, cst, result):
    L = r[0].shape[0]
    tri, seg, head0, strict, incl, col_head0, eye2, blockdiag = cst
    bf = lambda x: x.astype(BF16)

    kk_ = _each(lambda k_, w_: k_ * w_, k, kkw)
    kn = _each(lambda x: _segsum(x * x, seg), kk_)
    cum = _each(lambda x: _cumsum_rows(tri, x), lw)
    yield
    kk = _each(lambda x, n_: x / jnp.maximum(jnp.sqrt(n_), 1e-12), kk_, kn)
    k2 = _each(lambda k_, a_, w_: k_ * (1.0 + (a_ - 1.0) * w_), k, a, kaw)
    kka = _each(lambda x, a_: x * a_, kk, a)
    c_end = [c[L - 1:L, :] for c in cum]
    e_inv = [jnp.exp(-c) for c in cum]
    e_rem = _each(lambda c, ce: jnp.exp(ce - c), cum, c_end)
    at = _each(lambda x, c, l_: bf(-x * jnp.exp(c - l_)), kk, cum, lw)
    rt = _each(lambda x, c: bf(x * jnp.exp(c)), r, cum)
    bt = _each(lambda x, e: bf(x * e), kka, e_inv)
    kt = _each(lambda x, e: bf(x * e), k2, e_inv)
    vb = [bf(x) for x in v]
    zero = jnp.zeros_like(vb[0])

    def by_head(x):
        return jnp.concatenate([jnp.where(head0, x, zero), jnp.where(head0, zero, x)], axis=0)

    lhs = _each(lambda x, y_: jnp.concatenate([x, y_], axis=0), at, rt)
    ab = _each(lambda l_, x: _dot_nt(l_, by_head(x)), lhs, bt)
    ak = _each(lambda l_, x: _dot_nt(l_, by_head(x)), lhs, kt)
    yield
    a_rb = [jnp.where(incl, x[L:], 0.0) for x in ab]
    a_ak = [jnp.where(strict, x[:L], 0.0) for x in ak]
    a_rk = [jnp.where(incl, x[L:], 0.0) for x in ak]

    def blockdiag2(x):
        n_cat = jnp.where(strict, x[:L], 0.0)
        return jnp.concatenate([jnp.where(col_head0, n_cat, 0.0), jnp.where(col_head0, 0.0, n_cat)], axis=0)

    p = [blockdiag2(x) for x in ab]
    t_inv = [eye2 + x for x in p]
    sb = [bf(x) for x in s0]
    vs = [by_head(x) for x in vb]
    x0 = _each(lambda a_, s_, m_, v_: _dot_nt(a_, s_) + _dot(bf(m_), v_), at, sb, a_ak, vs)
    for _ in range(L.bit_length() - 2):
        p = [_dot(bf(x), bf(x)) for x in p]
        yield
        t_inv = _each(lambda t_, x: t_ + _dot(bf(t_), bf(x)), t_inv, p)
    yield
    us = _each(lambda t_, x: _dot(bf(t_), by_head(bf(x))), t_inv, x0)
    yield
    y = _each(lambda r_, s_, m1, m2, u_, v_: _dot_nt(r_, s_) + _dot(
        bf(jnp.concatenate([m1, m2], axis=1)), jnp.concatenate([bf(u_), v_], axis=0)),
        rt, sb, a_rb, a_rk, us, vs)
    upd = _each(lambda u_, v_, x1, x2, e: _dot_tn(
        jnp.concatenate([bf(u_[:L] + u_[L:]), v_], axis=0),
        jnp.concatenate([bf(x1 * e), bf(x2 * e)], axis=0)), us, vb, kka, k2, e_rem)
    yield
    s_new = _each(lambda s_, ce, u_: s_ * jnp.exp(ce) + jnp.where(blockdiag, u_, 0.0), s0, c_end, upd)

    hd = float(LANE // 2)
    mean = [_segsum(x, seg) / hd for x in y]
    bsum = _each(lambda r_, k_, w_: _segsum(r_ * k_ * w_, seg), r, k2, rkw)
    yield
    yc = _each(lambda x, m_: x - m_, y, mean)
    var = [_segsum(x * x, seg) / hd for x in yc]
    yield
    out = _each(lambda x, v_, w_, b_, bs, vv, g_: (x * lax.rsqrt(v_ + RW_GN_EPS) * w_ + b_ + bs * vv) * g_,
                yc, var, lnw, lnb, bsum, v, g)
    result.extend([out, s_new])


def _rwkv_steps(c, refs, *, rows, lead, width):
    (rkv_ref, misc_ref, mu_rkv_ref, mu_misc_ref, w0_ref, w2_ref, a0_ref, a2_ref, g2_ref,
     kk_ref, ka_ref, rk_ref, lnw_ref, lnb_ref, tri_ref, seg_ref, eye_ref, y_ref, s_ref,
     carry_rkv, carry_misc, r_s, k_s, v_s, a_s, lw_s, g_s) = refs
    L = CHUNK
    p = rkv_ref[...]
    pm = misc_ref[:, :3 * LANE]
    if rows < L:
        p = jnp.concatenate([jnp.zeros((L - rows, p.shape[1]), F32), p], axis=0)
        pm = jnp.concatenate([jnp.zeros((L - rows, pm.shape[1]), F32), pm], axis=0)
    row = _iota((L, 1), 0)
    first = jnp.where(c == 0, lead, 0)
    valid = row >= first
    p = jnp.where(valid, p, 0.0)
    pm = jnp.where(valid, pm, 0.0)

    def shift_mix(cur, carry_ref, mu):
        prev = jnp.where(row == first, carry_ref[...], pltpu.roll(cur, 1, axis=0))
        carry_ref[...] = cur[L - 1:L, :]
        return jnp.where(valid, cur + mu * (prev - cur), 0.0)

    u = shift_mix(p, carry_rkv, mu_rkv_ref[...])
    um = shift_mix(pm, carry_misc, mu_misc_ref[...])

    lora = um[:, :LANE]
    wl = w0_ref[...] + _dot(jnp.tanh(lora).astype(BF16), w2_ref[...])
    lw_s[...] = jnp.where(valid, -DECAY_SCALE * _sigmoid(wl), 0.0)
    a_s[...] = _sigmoid(a0_ref[...] + _dot(lora.astype(BF16), a2_ref[...]))
    g_s[...] = _dot(_sigmoid(um[:, LANE:]).astype(BF16), g2_ref[...])
    r_s[...] = u[:, :width]
    k_s[...] = u[:, width:2 * width]
    v_s[...] = u[:, 2 * width:]
    yield

    lane = _iota((1, LANE), 1)
    half = LANE // 2
    col2 = _iota((L, 2 * L), 1)
    t_i = _iota((L, 2 * L), 0)
    s_i = col2 & (L - 1)
    cst = (
        tri_ref[...], seg_ref[...],
        lane < half,
        s_i < t_i, s_i <= t_i,
        col2 < L,
        eye_ref[...],
        (_iota((LANE, LANE), 0) < half) == (_iota((LANE, LANE), 1) < half),
    )
    sls = [slice(j * LANE, (j + 1) * LANE) for j in range(width // LANE)]
    pick = lambda ref: [ref[:, sl] for sl in sls]
    result = []
    yield from _rwkv_pairs(pick(r_s), pick(k_s), pick(v_s), pick(a_s), pick(lw_s), pick(g_s),
                           [s_ref[0, j] for j in range(len(sls))],
                           pick(kk_ref), pick(ka_ref), pick(rk_ref), pick(lnw_ref), pick(lnb_ref), cst, result)
    outs, s_new = result
    for j, sl in enumerate(sls):
        s_ref[0, j] = s_new[j]
        y_ref[:, sl] = outs[j][L - rows:, :].astype(y_ref.dtype)


def _mlstm_steps(c, refs, *, rows, lead, heads, hd):
    (qk_ref, v_ref, o_ref, misc_ref, conv0_ref, cw_ref, cb_ref, ifb_ref, nw_ref, tri_ref,
     y_ref, c_ref, n_ref, m_ref, ext) = refs
    L = CHUNK
    width = heads * hd

    def chunk_rows(x):
        if rows < L:
            return jnp.concatenate([jnp.zeros((L - rows, x.shape[1]), F32), x], axis=0)
        return x

    row = _iota((L, 1), 0)
    valid = row >= jnp.where(c == 0, lead, 0)

    carried = jnp.concatenate([jnp.zeros((lead - SUBLANE, 2 * width), F32), conv0_ref[0],
                               jnp.zeros((L - lead, 2 * width), F32)], axis=0)
    use_carried = (c == 0) & (row >= lead - SUBLANE) & (row < lead)
    ext[SUBLANE:SUBLANE + L, :] = jnp.where(valid, chunk_rows(qk_ref[...]), jnp.where(use_carried, carried, 0.0))
    conv = cb_ref[...]
    for j in range(4):
        conv = conv + cw_ref[j:j + 1, :] * ext[SUBLANE - 3 + j:SUBLANE - 3 + j + L, :]
    ext[0:SUBLANE, :] = ext[L:L + SUBLANE, :]
    qk = jnp.where(valid, conv * _sigmoid(conv), 0.0)
    v = jnp.where(valid, chunk_rows(v_ref[...]), 0.0)
    og = _sigmoid(chunk_rows(o_ref[...]))
    yield

    lane = _iota((1, LANE), 1)
    is_i = lane < heads
    z = chunk_rows(misc_ref[:, MISC_IF:MISC_IF + LANE]) + ifb_ref[...]
    logsig = jnp.minimum(z, 0.0) - jnp.log1p(jnp.exp(-jnp.abs(z)))
    gates = jnp.where(valid, jnp.where(is_i, z, logsig), jnp.where(is_i, NEG, 0.0))
    cum = _cumsum_rows(tri_ref[...], jnp.where(is_i, 0.0, gates))
    pad = jnp.zeros((LANE - L, LANE), F32)
    gates_t = jnp.concatenate([gates, pad], axis=0).T
    cum_t = jnp.concatenate([cum, pad], axis=0).T
    causal = _iota((L, L), 1) <= _iota((L, L), 0)
    m_all = m_ref[0]
    yield

    for h in range(heads):
        sl = slice(h * hd, (h + 1) * hd)
        q = qk[:, sl]
        kx = qk[:, width + h * hd:width + (h + 1) * hd] * (hd ** -0.5)
        vh = v[:, sl]
        b_col = cum[:, heads + h:heads + h + 1]
        i_col = gates[:, h:h + 1]
        b_row = cum_t[heads + h:heads + h + 1, :L]
        i_row = gates_t[h:h + 1, :L]
        b_end = b_col[L - 1:L, :]
        m_prev = jnp.sum(jnp.where(lane == h, m_all, 0.0), axis=1, keepdims=True)

        log_inter = b_col + m_prev
        dmat = jnp.where(causal, b_col - b_row + i_row, NEG)
        m_q = jnp.maximum(log_inter, jnp.max(dmat, axis=-1, keepdims=True))
        w_inter = jnp.exp(log_inter - m_q)
        qb = q.astype(BF16)
        s = _dot_nt(qb, kx.astype(BF16)) * jnp.exp(dmat - m_q)
        c_h = c_ref[0, h]
        n_h = n_ref[0, h:h + 1, :]
        yield
        num = w_inter * _dot(qb, c_h.astype(BF16)) + _dot(s.astype(BF16), vh.astype(BF16))
        den = w_inter * jnp.sum(q * n_h, axis=-1, keepdims=True) + jnp.sum(s, axis=-1, keepdims=True)
        hcell = num / jnp.maximum(jnp.abs(den), jnp.exp(-m_q))
        yield

        g_col = b_end - b_col + i_col
        m_new = jnp.maximum(b_end + m_prev, jnp.max(g_col, axis=0, keepdims=True))
        a_st = jnp.exp(b_end + m_prev - m_new)
        wkk = jnp.exp(g_col - m_new) * kx
        c_ref[0, h] = a_st * c_h + _dot_tn(wkk.astype(BF16), vh.astype(BF16))
        n_ref[0, h:h + 1, :] = a_st * n_h + jnp.sum(wkk, axis=0, keepdims=True)
        m_all = jnp.where(lane == h, m_new, m_all)

        mu = jnp.mean(hcell, axis=-1, keepdims=True)
        hc = hcell - mu
        var = jnp.mean(hc * hc, axis=-1, keepdims=True)
        yh = hc * lax.rsqrt(var + ML_LN_EPS) * nw_ref[:, sl] * og[:, sl]
        y_ref[:, sl] = yh[L - rows:, :].astype(y_ref.dtype)
        yield

    m_ref[0] = m_all


N_RW_IN, N_ML_IN = 20, 13


def _mixers_kernel(*refs, rows, lead, width, heads, hd):
    rw_in = refs[:N_RW_IN]
    ml_in = refs[N_RW_IN:N_RW_IN + N_ML_IN]
    outs = refs[N_RW_IN + N_ML_IN + 2:N_RW_IN + N_ML_IN + 8]
    scratch = refs[N_RW_IN + N_ML_IN + 8:]
    (rkv_ref, misc_ref, sh_rkv_ref, sh_misc_ref, s0_ref), rw_prm = rw_in[:5], rw_in[5:]
    (qk_ref, v_ref, o_ref, misc2_ref, conv0_ref, c0_ref, n0_ref, m0_ref), ml_prm = ml_in[:8], ml_in[8:]
    y_rw_ref, s_ref, y_ml_ref, c_ref, n_ref, m_ref = outs
    carry_rkv, carry_misc = scratch[:2]
    ext = scratch[-1]
    c = pl.program_id(1)

    @pl.when(c == 0)
    def _():
        carry_rkv[...] = sh_rkv_ref[0]
        carry_misc[...] = sh_misc_ref[0]
        s_ref[...] = s0_ref[...]
        c_ref[...] = c0_ref[...]
        n_ref[...] = n0_ref[...]
        m_ref[...] = m0_ref[...]
        ext[0:SUBLANE, :] = jnp.zeros((SUBLANE, ext.shape[1]), F32)

    parts = [
        _rwkv_steps(c, (rkv_ref, misc_ref) + tuple(rw_prm) + (y_rw_ref, s_ref) + tuple(scratch[:-1]),
                    rows=rows, lead=lead, width=width),
        _mlstm_steps(c, (qk_ref, v_ref, o_ref, misc2_ref, conv0_ref) + tuple(ml_prm)
                     + (y_ml_ref, c_ref, n_ref, m_ref, ext), rows=rows, lead=lead, heads=heads, hd=hd),
    ]
    while parts:
        for part in list(parts):
            if next(part, StopIteration) is StopIteration:
                parts.remove(part)


def _mixers(proj, y_rw_prev, y_ml_prev, rw_state, ml_state, rw_prm, ml_prm, *,
            batch, n_chunks, rows, seq_row, lead, width, heads, hd, d_model):
    n = proj.shape[0]
    n_pairs = width // LANE
    rkv_blk = (2 * d_model + 2 * width) // (3 * width)
    qk_blk = (2 * d_model) // (2 * width)
    v_blk = (2 * d_model + 2 * width + 3 * width) // width
    misc_blk = (proj.shape[1] - MISC_COLS) // MISC_COLS
    kern = functools.partial(_mixers_kernel, rows=rows, lead=lead, width=width, heads=heads, hd=hd)
    full = lambda shape: pl.BlockSpec(shape, lambda b, c: (0,) * len(shape))
    per_seq = lambda shape: pl.BlockSpec(shape, lambda b, c: (b,) + (0,) * (len(shape) - 1))
    cols = lambda w_, blk: pl.BlockSpec((rows, w_), lambda b, c: (seq_row(b, c), blk))
    state_specs = [per_seq((1, n_pairs, LANE, LANE)), per_seq((1, heads, hd, hd)),
                   per_seq((1, heads, hd)), per_seq((1, 1, LANE))]
    rw_specs = [
        cols(3 * width, rkv_blk), cols(MISC_COLS, misc_blk),
        per_seq((1, 1, 3 * width)), per_seq((1, 1, 3 * LANE)), state_specs[0],
        full((1, 3 * width)), full((1, 3 * LANE)),
        full((1, width)), full((LANE, width)), full((1, width)), full((LANE, width)), full((2 * LANE, width)),
        full((1, width)), full((1, width)), full((1, width)), full((1, width)), full((1, width)),
        full((CHUNK, CHUNK)), full((LANE, LANE)), full((2 * CHUNK, 2 * CHUNK)),
    ]
    ml_specs = [
        cols(2 * width, qk_blk), cols(width, v_blk), cols(width, v_blk + 1), cols(MISC_COLS, misc_blk),
        per_seq((1, SUBLANE, 2 * width)), *state_specs[1:],
        full((4, 2 * width)), full((1, 2 * width)), full((1, LANE)), full((1, width)), full((CHUNK, CHUNK)),
    ]
    assert len(rw_specs) == N_RW_IN and len(ml_specs) == N_ML_IN and lead >= SUBLANE
    in_specs = rw_specs + ml_specs + [pl.BlockSpec(memory_space=pl.ANY)] * 2
    y_spec = pl.BlockSpec((rows, width), lambda b, c: (seq_row(b, c), 0))
    out_specs = [y_spec, state_specs[0], y_spec] + state_specs[1:]
    sds = jax.ShapeDtypeStruct
    out_shape = [
        sds((n, width), BF16), sds((batch, n_pairs, LANE, LANE), F32),
        sds((n, width), BF16), sds((batch, heads, hd, hd), F32), sds((batch, heads, hd), F32),
        sds((batch, 1, LANE), F32),
    ]
    scratch = [pltpu.VMEM((1, 3 * width), F32), pltpu.VMEM((1, 3 * LANE), F32)]
    scratch += [pltpu.VMEM((CHUNK, width), F32) for _ in range(6)]
    scratch += [pltpu.VMEM((CHUNK + SUBLANE, 2 * width), F32)]
    n_in = len(in_specs)
    return pl.pallas_call(
        kern, grid=(batch, n_chunks), in_specs=in_specs, out_specs=out_specs, out_shape=out_shape,
        scratch_shapes=scratch, input_output_aliases={n_in - 2: 0, n_in - 1: 2},
        compiler_params=_params("arbitrary", "arbitrary"), name="mixers",
    )(proj, proj, *rw_state, *rw_prm, proj, proj, proj, proj, *ml_state, *ml_prm, y_rw_prev, y_ml_prev)


def _pad_rows(w, rows):
    return jnp.pad(w, ((0, rows - w.shape[0]), (0, 0)))


def _swiglu_weights(w_gate, w_up, w_down):
    f = w_gate.shape[1]
    fp = -(-f // FF_TILE) * FF_TILE
    pad_cols = lambda w: jnp.pad(w, ((0, 0), (0, fp - f))).astype(BF16)
    return pad_cols(w_gate), pad_cols(w_up), _pad_rows(w_down, fp).astype(BF16)


def kernel(x_prompt, x_sample, state_rwkv_shift, state_rwkv_wkv, state_mlstm_conv, state_mlstm_C,
           state_mlstm_n, state_mlstm_m, meta_tokens, ffn1_norm, ffn1_w_gate, ffn1_w_up, ffn1_w_down,
           mix_norm, w_in, rw_mu, rw_w0, rw_w2, rw_a0, rw_a2, rw_g2, rw_kk, rw_ka, rw_rk, rw_ln_w, rw_ln_b,
           ml_conv_w, ml_conv_b, ml_i_b, ml_f_b, ml_norm_w, w_br_rw, w_br_ml, w_out,
           ffn2_norm, ffn2_w_gate, ffn2_w_up, ffn2_w_down, final_norm):
    assert ffn1_norm.shape[0] == 1, "single-layer trunk"
    B, T, D = x_prompt.shape
    Bs, Ts, _ = x_sample.shape
    n_meta = meta_tokens.shape[0]
    W = rw_w0.shape[-1]
    dl, al, gl = rw_w2.shape[1], rw_a2.shape[1], rw_g2.shape[1]
    rw_heads, rw_hd = rw_rk.shape[1], rw_rk.shape[2]
    Wm = ml_norm_w.shape[-1]
    H = ml_i_b.shape[-1]
    hd = Wm // H
    K = ml_conv_w.shape[1]
    L = CHUNK
    assert rw_hd == LANE // 2 and dl == LANE // 2 and al == LANE // 2 and gl <= 2 * LANE
    assert K == 4 and hd % LANE == 0 and 2 * H <= LANE and W == Wm and 2 * W == D
    assert Ts <= L and Ts % (2 * SUBLANE) == 0 and D % COL_TILE == 0 and (11 * D // 2) % MISC_COLS == 0

    lead = (-n_meta) % L
    head_rows = lead + n_meta
    n_head = head_rows // L
    n_chunks = n_head + T // L
    lead_s = L - Ts
    n_main = B * T
    sample0 = n_main + B * head_rows
    N = sample0 + Bs * Ts
    assert T % L == 0 and T % ROW_TILE == 0 and N % PROJ_ROW_TILE == 0
    head = jnp.concatenate([jnp.zeros((lead, D), F32), meta_tokens.astype(F32)], axis=0)
    x_extra = jnp.concatenate([jnp.broadcast_to(head[None], (B, head_rows, D)).reshape(B * head_rows, D),
                               x_sample.reshape(Bs * Ts, D)], axis=0)

    ffn1_w = _swiglu_weights(ffn1_w_gate[0], ffn1_w_up[0], ffn1_w_down[0])
    ffn2_w = _swiglu_weights(ffn2_w_gate[0], ffn2_w_up[0], ffn2_w_down[0])
    wi = w_in[0]
    o_lora = 3 * W
    o_qk = o_lora + dl + al + gl
    o_v = o_qk + 2 * Wm
    o_o = o_v + Wm
    o_i = o_o + Wm
    o_gate = o_i + 2 * H
    zc = lambda n: jnp.zeros((D, n), wi.dtype)
    misc_w = jnp.concatenate([wi[:, o_lora:o_qk], zc(MISC_IF - (dl + al + gl)),
                              wi[:, o_i:o_gate], zc(MISC_COLS - MISC_IF - 2 * H)], axis=1)
    w_proj = jnp.concatenate([wi[:, o_gate:], wi[:, o_qk:o_v], wi[:, :o_lora], wi[:, o_v:o_i], misc_w],
                             axis=1).astype(BF16)

    def misc_vec(v_lora, fill=0.0):
        return jnp.pad(v_lora, [(0, 0)] * (v_lora.ndim - 1) + [(0, 3 * LANE - v_lora.shape[-1])],
                       constant_values=fill)

    row = lambda v: v.reshape(1, -1).astype(F32)
    rw_prm = (
        row(rw_mu[0, :o_lora]), misc_vec(row(rw_mu[0, o_lora:])),
        row(rw_w0[0]), _pad_rows(rw_w2[0], LANE).astype(BF16),
        row(rw_a0[0]), jnp.concatenate([jnp.zeros((dl, W), F32), rw_a2[0]], axis=0).astype(BF16),
        _pad_rows(rw_g2[0], 2 * LANE).astype(BF16),
        row(rw_kk[0]), row(rw_ka[0]), row(rw_rk[0]), row(rw_ln_w[0]), row(rw_ln_b[0]),
    )
    ifb = jnp.pad(jnp.concatenate([ml_i_b[0], ml_f_b[0]]).reshape(1, 2 * H), ((0, 0), (0, LANE - 2 * H)))
    tri = jnp.tril(jnp.ones((L, L), BF16))
    same_head = jnp.kron(jnp.eye(LANE // rw_hd, dtype=BF16), jnp.ones((rw_hd, rw_hd), BF16))
    rw_prm = rw_prm + (tri, same_head, jnp.eye(2 * L, dtype=F32))
    ml_prm = (ml_conv_w[0].astype(F32), row(ml_conv_b[0]), ifb.astype(F32), row(ml_norm_w[0]), tri)

    def rw_state_in(shift, wkv):
        b = shift.shape[0]
        s = wkv.reshape(b, rw_heads // 2, 2, rw_hd, rw_hd)
        z = jnp.zeros_like(s[:, :, 0])
        bd = jnp.concatenate([jnp.concatenate([s[:, :, 0], z], axis=-1),
                              jnp.concatenate([z, s[:, :, 1]], axis=-1)], axis=-2)
        return shift[:, None, :o_lora], misc_vec(shift[:, None, o_lora:]), bd

    def rw_state_out(bd):
        h = rw_hd
        return jnp.stack([bd[:, :, :h, :h], bd[:, :, h:, h:]], axis=2).reshape(bd.shape[0], rw_heads, h, h)

    def conv_in(buf):
        return jnp.pad(buf, ((0, 0), (SUBLANE - (K - 1), 0), (0, 0)))

    def m_in(m):
        return jnp.pad(m, ((0, 0), (0, LANE - H)))[:, None, :]

    x1 = _ffn((x_prompt, x_extra), row(ffn1_norm[0]), *ffn1_w, row(final_norm),
              split_out=None, final_norm=False)
    proj = _proj(x1, row(mix_norm[0]), w_proj)

    zeros = lambda *s: jnp.zeros(s, F32)
    main_chunks = T // L

    def prompt_row(b, c):
        return jnp.where(c < n_head, n_main // L + b * n_head + c, b * main_chunks + c - n_head)

    seqs = (
        dict(batch=B, n_chunks=n_chunks, rows=L, seq_row=prompt_row, lead=lead),
        dict(batch=Bs, n_chunks=1, rows=Ts, seq_row=lambda b, c: sample0 // Ts + b, lead=lead_s),
    )
    rw_states = (
        rw_state_in(zeros(B, o_qk), zeros(B, rw_heads, rw_hd, rw_hd)),
        rw_state_in(state_rwkv_shift[0], state_rwkv_wkv[0]),
    )
    ml_states = (
        (conv_in(zeros(B, K - 1, 2 * Wm)), zeros(B, H, hd, hd), zeros(B, H, hd), m_in(zeros(B, H))),
        (conv_in(state_mlstm_conv[0]), state_mlstm_C[0], state_mlstm_n[0], m_in(state_mlstm_m[0])),
    )
    y_rw = jnp.zeros((N, W), BF16)
    y_ml = jnp.zeros((N, Wm), BF16)
    rw_out, ml_out = [], []
    for seq, rws, mls in zip(seqs, rw_states, ml_states):
        y_rw, s_end, y_ml, c_end, n_end, m_end = _mixers(proj, y_rw, y_ml, rws, mls, rw_prm, ml_prm,
                                                         width=W, heads=H, hd=hd, d_model=D, **seq)
        rw_out.append(rw_state_out(s_end))
        ml_out.append((c_end, n_end, m_end[:, 0, :H]))

    x2 = _merge(x1, y_rw, y_ml, proj, w_br_rw[0].astype(BF16), w_br_ml[0].astype(BF16), w_out[0].astype(BF16))
    y_prompt, y_extra = _ffn((x2,), row(ffn2_norm[0]), *ffn2_w, row(final_norm),
                             split_out=(B, T), final_norm=True)

    c_rkv = 2 * D + 2 * Wm
    c_misc = proj.shape[1] - MISC_COLS

    def seq_states(row0, batch, t_len):
        last = row0 + (jnp.arange(batch)[:, None] + 1) * t_len - (K - 1) + jnp.arange(K - 1)[None, :]
        tail = proj[last.reshape(-1)].reshape(batch, K - 1, -1)
        shift = jnp.concatenate([tail[:, -1, c_rkv:c_rkv + 3 * W],
                                 tail[:, -1, c_misc:c_misc + dl + al + gl]], axis=-1)
        return shift, tail[:, :, 2 * D:2 * D + 2 * Wm]

    p_shift, p_conv = seq_states(0, B, T)
    s_shift, s_conv = seq_states(sample0, Bs, Ts)
    y_sample = y_extra[B * head_rows:].reshape(Bs, Ts, D)
    d1 = lambda a: a[None]
    return (y_prompt, y_sample,
            d1(p_shift), d1(rw_out[0]), d1(p_conv), d1(ml_out[0][0]), d1(ml_out[0][1]), d1(ml_out[0][2]),
            d1(s_shift), d1(rw_out[1]), d1(s_conv), d1(ml_out[1][0]), d1(ml_out[1][1]), d1(ml_out[1][2]))
```

```python
import functools

import jax
import jax.numpy as jnp
from jax import lax
from jax.experimental import pallas as pl
from jax.experimental.pallas import tpu as pltpu

F32 = jnp.float32
BF16 = jnp.bfloat16

LANE = 128
SUBLANE = 8
VMEM_LIMIT_BYTES = 56 * 1024 * 1024
CHUNK = 64
ROW_TILE = 512
PROJ_ROW_TILE = 1024
FF_TILE = 512
COL_TILE = 512
MERGE_ROW_TILE = 256
MISC_COLS = 512
MISC_IF = 384
RMS_EPS = 1e-6
RW_GN_EPS = 64e-5
ML_LN_EPS = 1e-5
NEG = -1e30
DECAY_SCALE = 0.6065306597126334


def _dot(a, b):
    return jnp.dot(a, b, preferred_element_type=F32)


def _dot_nt(a, b):
    return lax.dot_general(a, b, (((1,), (1,)), ((), ())), preferred_element_type=F32)


def _dot_tn(a, b):
    return lax.dot_general(a, b, (((0,), (0,)), ((), ())), preferred_element_type=F32)


def _split3(x):
    h1 = x.astype(BF16)
    r1 = x - h1.astype(F32)
    h2 = r1.astype(BF16)
    r2 = r1 - h2.astype(F32)
    return h1, h2, r2.astype(BF16)


def _cumsum_rows(tri, x):
    n = x.shape[1]
    y = _dot(tri, jnp.concatenate(_split3(x), axis=1))
    return y[:, :n] + y[:, n:2 * n] + y[:, 2 * n:]


def _segsum(x, seg):
    rows = x.shape[0]
    h1, h2, _ = _split3(x)
    y = _dot(jnp.concatenate([h1, h2], axis=0), seg)
    return y[:rows] + y[rows:]


def _sigmoid(z):
    return jax.nn.sigmoid(z)


def _rmsnorm(x, g):
    return x * lax.rsqrt(jnp.mean(x * x, axis=-1, keepdims=True) + RMS_EPS) * g


def _iota(shape, dim):
    return lax.broadcasted_iota(jnp.int32, shape, dim)


def _params(*semantics):
    return pltpu.CompilerParams(dimension_semantics=semantics, vmem_limit_bytes=VMEM_LIMIT_BYTES)


def _ffn_kernel(*refs, n_f, n_main, split_in, split_out, final_norm):
    n_x = 2 if split_in else 1
    n_o = 2 if split_out else 1
    x_refs = refs[:n_x]
    g_ref, wg_ref, wu_ref, wd_ref, fg_ref = refs[n_x:n_x + 5]
    o_refs = refs[n_x + 5:n_x + 5 + n_o]
    xn_ref, acc_ref = refs[n_x + 5 + n_o:]
    i = pl.program_id(0)
    j = pl.program_id(1)

    def load_x():
        if split_in:
            return jnp.where(i < n_main, x_refs[0][...], x_refs[1][...])
        return x_refs[0][...]

    @pl.when(j == 0)
    def _():
        xn_ref[...] = _rmsnorm(load_x(), g_ref[...]).astype(BF16)
        acc_ref[...] = jnp.zeros_like(acc_ref)

    xn = xn_ref[...]
    gate = _dot(xn, wg_ref[...])
    h = (gate * _sigmoid(gate) * _dot(xn, wu_ref[...])).astype(BF16)
    acc_ref[...] += _dot(h, wd_ref[...])

    def finish(o_ref):
        y = load_x() + 0.5 * acc_ref[...]
        if final_norm:
            y = _rmsnorm(y, fg_ref[...])
        o_ref[...] = y

    last = j == n_f - 1
    if split_out:
        pl.when(last & (i < n_main))(lambda: finish(o_refs[0]))
        pl.when(last & (i >= n_main))(lambda: finish(o_refs[1]))
    else:
        pl.when(last)(lambda: finish(o_refs[0]))


def _ffn(xs, norm_g, wg, wu, wd, final_g, *, split_out, final_norm):
    split_in = len(xs) == 2
    d = xs[-1].shape[-1]
    if split_in:
        b, t = xs[0].shape[:2]
        n = b * t + xs[1].shape[0]
    else:
        n = xs[0].shape[0]
        b, t = split_out if split_out else (1, n)
    n_f = wd.shape[0] // FF_TILE
    tpb = t // ROW_TILE
    nm = b * tpb
    kern = functools.partial(_ffn_kernel, n_f=n_f, n_main=nm, split_in=split_in,
                             split_out=bool(split_out), final_norm=final_norm)
    whole = pl.BlockSpec((ROW_TILE, d), lambda i, j: (i, 0))
    main = pl.BlockSpec((None, ROW_TILE, d),
                        lambda i, j: (jnp.minimum(i, nm - 1) // tpb, jnp.minimum(i, nm - 1) % tpb, 0))
    extra = pl.BlockSpec((ROW_TILE, d), lambda i, j: (jnp.maximum(i - nm, 0), 0))
    return pl.pallas_call(
        kern,
        grid=(n // ROW_TILE, n_f),
        in_specs=([main, extra] if split_in else [whole]) + [
            pl.BlockSpec((1, d), lambda i, j: (0, 0)),
            pl.BlockSpec((d, FF_TILE), lambda i, j: (0, j)),
            pl.BlockSpec((d, FF_TILE), lambda i, j: (0, j)),
            pl.BlockSpec((FF_TILE, d), lambda i, j: (j, 0)),
            pl.BlockSpec((1, d), lambda i, j: (0, 0)),
        ],
        out_specs=[main, extra] if split_out else whole,
        out_shape=([jax.ShapeDtypeStruct((b, t, d), F32), jax.ShapeDtypeStruct((n - b * t, d), F32)]
                   if split_out else jax.ShapeDtypeStruct((n, d), F32)),
        scratch_shapes=[pltpu.VMEM((ROW_TILE, d), BF16), pltpu.VMEM((ROW_TILE, d), F32)],
        compiler_params=_params("arbitrary", "arbitrary"),
        name="ffn_final" if final_norm else "ffn",
    )(*xs, norm_g, wg, wu, wd, final_g)


def _proj_kernel(x_ref, g_ref, w_ref, o_ref, xn_ref):
    @pl.when(pl.program_id(1) == 0)
    def _():
        xn_ref[...] = _rmsnorm(x_ref[...], g_ref[...]).astype(BF16)

    o_ref[...] = _dot(xn_ref[...], w_ref[...])


def _proj(x, norm_g, w):
    n, d = x.shape
    cols = w.shape[1]
    tm = PROJ_ROW_TILE
    return pl.pallas_call(
        _proj_kernel,
        grid=(n // tm, cols // COL_TILE),
        in_specs=[
            pl.BlockSpec((tm, d), lambda i, j: (i, 0)),
            pl.BlockSpec((1, d), lambda i, j: (0, 0)),
            pl.BlockSpec((d, COL_TILE), lambda i, j: (0, j)),
        ],
        out_specs=pl.BlockSpec((tm, COL_TILE), lambda i, j: (i, j)),
        out_shape=jax.ShapeDtypeStruct((n, cols), F32),
        scratch_shapes=[pltpu.VMEM((tm, d), BF16)],
        compiler_params=_params("parallel", "arbitrary"),
        name="in_proj",
    )(x, norm_g, w)


def _merge_kernel(x_ref, yrw_ref, yml_ref, ga_ref, gb_ref, wrw_ref, wml_ref, wo_ref, o_ref):
    merged = (_sigmoid(ga_ref[...]) * _dot(yrw_ref[...], wrw_ref[...])
              + _sigmoid(gb_ref[...]) * _dot(yml_ref[...], wml_ref[...]))
    o_ref[...] = x_ref[...] + _dot(merged.astype(BF16), wo_ref[...])


def _merge(x, y_rw, y_ml, proj, w_rw, w_ml, w_out):
    n, d = x.shape
    w = y_rw.shape[1]
    tm = MERGE_ROW_TILE
    rows = lambda width, blk: pl.BlockSpec((tm, width), lambda i: (i, blk))
    resident = lambda shape: pl.BlockSpec(shape, lambda i: (0, 0), pipeline_mode=pl.Buffered(1))
    return pl.pallas_call(
        _merge_kernel,
        grid=(n // tm,),
        in_specs=[rows(d, 0), rows(w, 0), rows(w, 0), rows(d, 0), rows(d, 1),
                  resident((w, d)), resident((w, d)), resident((d, d))],
        out_specs=rows(d, 0),
        out_shape=jax.ShapeDtypeStruct((n, d), F32),
        compiler_params=_params("parallel"),
        name="merge",
    )(x, y_rw, y_ml, proj, proj, w_rw, w_ml, w_out)


def _each(f, *lists):
    return [f(*xs) for xs in zip(*lists)]


def _rwkv_pairs(r, k, v, a, lw, g, s0, kkw, kaw, rkw, lnw, lnb, cst, result):
    L = r[0].shape[0]
    tri, seg, head0, strict, incl, col_head0, eye2, blockdiag = cst
    bf = lambda x: x.astype(BF16)

    kk_ = _each(lambda k_, w_: k_ * w_, k, kkw)
    kn = _each(lambda x: _segsum(x * x, seg), kk_)
    cum = _each(lambda x: _cumsum_rows(tri, x), lw)
    yield
    kk = _each(lambda x, n_: x / jnp.maximum(jnp.sqrt(n_), 1e-12), kk_, kn)
    k2 = _each(lambda k_, a_, w_: k_ * (1.0 + (a_ - 1.0) * w_), k, a, kaw)
    kka = _each(lambda x, a_: x * a_, kk, a)
    c_end = [c[L - 1:L, :] for c in cum]
    e_inv = [jnp.exp(-c) for c in cum]
    e_rem = _each(lambda c, ce: jnp.exp(ce - c), cum, c_end)
    at = _each(lambda x, c, l_: bf(-x * jnp.exp(c - l_)), kk, cum, lw)
    rt = _each(lambda x, c: bf(x * jnp.exp(c)), r, cum)
    bt = _each(lambda x, e: bf(x * e), kka, e_inv)
    kt = _each(lambda x, e: bf(x * e), k2, e_inv)
    vb = [bf(x) for x in v]
    zero = jnp.zeros_like(vb[0])

    def by_head(x):
        return jnp.concatenate([jnp.where(head0, x, zero), jnp.where(head0, zero, x)], axis=0)

    lhs = _each(lambda x, y_: jnp.concatenate([x, y_], axis=0), at, rt)
    ab = _each(lambda l_, x: _dot_nt(l_, by_head(x)), lhs, bt)
    ak = _each(lambda l_, x: _dot_nt(l_, by_head(x)), lhs, kt)
    yield
    a_rb = [jnp.where(incl, x[L:], 0.0) for x in ab]
    a_ak = [jnp.where(strict, x[:L], 0.0) for x in ak]
    a_rk = [jnp.where(incl, x[L:], 0.0) for x in ak]

    def blockdiag2(x):
        n_cat = jnp.where(strict, x[:L], 0.0)
        return jnp.concatenate([jnp.where(col_head0, n_cat, 0.0), jnp.where(col_head0, 0.0, n_cat)], axis=0)

    p = [blockdiag2(x) for x in ab]
    t_inv = [eye2 + x for x in p]
    sb = [bf(x) for x in s0]
    vs = [by_head(x) for x in vb]
    x0 = _each(lambda a_, s_, m_, v_: _dot_nt(a_, s_) + _dot(bf(m_), v_), at, sb, a_ak, vs)
    for _ in range(L.bit_length() - 2):
        p = [_dot(bf(x), bf(x)) for x in p]
        yield
        t_inv = _each(lambda t_, x: t_ + _dot(bf(t_), bf(x)), t_inv, p)
    yield
    us = _each(lambda t_, x: _dot(bf(t_), by_head(bf(x))), t_inv, x0)
    yield
    y = _each(lambda r_, s_, m1, m2, u_, v_: _dot_nt(r_, s_) + _dot(
        bf(jnp.concatenate([m1, m2], axis=1)), jnp.concatenate([bf(u_), v_], axis=0)),
        rt, sb, a_rb, a_rk, us, vs)
    upd = _each(lambda u_, v_, x1, x2, e: _dot_tn(
        jnp.concatenate([bf(u_[:L] + u_[L:]), v_], axis=0),
        jnp.concatenate([bf(x1 * e), bf(x2 * e)], axis=0)), us, vb, kka, k2, e_rem)
    yield
    s_new = _each(lambda s_, ce, u_: s_ * jnp.exp(ce) + jnp.where(blockdiag, u_, 0.0), s0, c_end, upd)

    hd = float(LANE // 2)
    mean = [_segsum(x, seg) / hd for x in y]
    bsum = _each(lambda r_, k_, w_: _segsum(r_ * k_ * w_, seg), r, k2, rkw)
    yield
    yc = _each(lambda x, m_: x - m_, y, mean)
    var = [_segsum(x * x, seg) / hd for x in yc]
    yield
    out = _each(lambda x, v_, w_, b_, bs, vv, g_: (x * lax.rsqrt(v_ + RW_GN_EPS) * w_ + b_ + bs * vv) * g_,
                yc, var, lnw, lnb, bsum, v, g)
    result.extend([out, s_new])


def _rwkv_steps(c, refs, *, rows, lead, width):
    (rkv_ref, misc_ref, mu_rkv_ref, mu_misc_ref, w0_ref, w2_ref, a0_ref, a2_ref, g2_ref,
     kk_ref, ka_ref, rk_ref, lnw_ref, lnb_ref, tri_ref, seg_ref, eye_ref, y_ref, s_ref,
     carry_rkv, carry_misc, r_s, k_s, v_s, a_s, lw_s, g_s) = refs
    L = CHUNK
    p = rkv_ref[...]
    pm = misc_ref[:, :3 * LANE]
    if rows < L:
        p = jnp.concatenate([jnp.zeros((L - rows, p.shape[1]), F32), p], axis=0)
        pm = jnp.concatenate([jnp.zeros((L - rows, pm.shape[1]), F32), pm], axis=0)
    row = _iota((L, 1), 0)
    first = jnp.where(c == 0, lead, 0)
    valid = row >= first
    p = jnp.where(valid, p, 0.0)
    pm = jnp.where(valid, pm, 0.0)

    def shift_mix(cur, carry_ref, mu):
        prev = jnp.where(row == first, carry_ref[...], pltpu.roll(cur, 1, axis=0))
        carry_ref[...] = cur[L - 1:L, :]
        return jnp.where(valid, cur + mu * (prev - cur), 0.0)

    u = shift_mix(p, carry_rkv, mu_rkv_ref[...])
    um = shift_mix(pm, carry_misc, mu_misc_ref[...])

    lora = um[:, :LANE]
    wl = w0_ref[...] + _dot(jnp.tanh(lora).astype(BF16), w2_ref[...])
    lw_s[...] = jnp.where(valid, -DECAY_SCALE * _sigmoid(wl), 0.0)
    a_s[...] = _sigmoid(a0_ref[...] + _dot(lora.astype(BF16), a2_ref[...]))
    g_s[...] = _dot(_sigmoid(um[:, LANE:]).astype(BF16), g2_ref[...])
    r_s[...] = u[:, :width]
    k_s[...] = u[:, width:2 * width]
    v_s[...] = u[:, 2 * width:]
    yield

    lane = _iota((1, LANE), 1)
    half = LANE // 2
    col2 = _iota((L, 2 * L), 1)
    t_i = _iota((L, 2 * L), 0)
    s_i = col2 & (L - 1)
    cst = (
        tri_ref[...], seg_ref[...],
        lane < half,
        s_i < t_i, s_i <= t_i,
        col2 < L,
        eye_ref[...],
        (_iota((LANE, LANE), 0) < half) == (_iota((LANE, LANE), 1) < half),
    )
    sls = [slice(j * LANE, (j + 1) * LANE) for j in range(width // LANE)]
    pick = lambda ref: [ref[:, sl] for sl in sls]
    result = []
    yield from _rwkv_pairs(pick(r_s), pick(k_s), pick(v_s), pick(a_s), pick(lw_s), pick(g_s),
                           [s_ref[0, j] for j in range(len(sls))],
                           pick(kk_ref), pick(ka_ref), pick(rk_ref), pick(lnw_ref), pick(lnb_ref), cst, result)
    outs, s_new = result
    for j, sl in enumerate(sls):
        s_ref[0, j] = s_new[j]
        y_ref[:, sl] = outs[j][L - rows:, :].astype(y_ref.dtype)


def _mlstm_steps(c, refs, *, rows, lead, heads, hd):
    (qk_ref, v_ref, o_ref, misc_ref, conv0_ref, cw_ref, cb_ref, ifb_ref, nw_ref, tri_ref,
     y_ref, c_ref, n_ref, m_ref, ext) = refs
    L = CHUNK
    width = heads * hd

    def chunk_rows(x):
        if rows < L:
            return jnp.concatenate([jnp.zeros((L - rows, x.shape[1]), F32), x], axis=0)
        return x

    row = _iota((L, 1), 0)
    valid = row >= jnp.where(c == 0, lead, 0)

    carried = jnp.concatenate([jnp.zeros((lead - SUBLANE, 2 * width), F32), conv0_ref[0],
                               jnp.zeros((L - lead, 2 * width), F32)], axis=0)
    use_carried = (c == 0) & (row >= lead - SUBLANE) & (row < lead)
    ext[SUBLANE:SUBLANE + L, :] = jnp.where(valid, chunk_rows(qk_ref[...]), jnp.where(use_carried, carried, 0.0))
    conv = cb_ref[...]
    for j in range(4):
        conv = conv + cw_ref[j:j + 1, :] * ext[SUBLANE - 3 + j:SUBLANE - 3 + j + L, :]
    ext[0:SUBLANE, :] = ext[L:L + SUBLANE, :]
    qk = jnp.where(valid, conv * _sigmoid(conv), 0.0)
    v = jnp.where(valid, chunk_rows(v_ref[...]), 0.0)
    og = _sigmoid(chunk_rows(o_ref[...]))
    yield

    lane = _iota((1, LANE), 1)
    is_i = lane < heads
    z = chunk_rows(misc_ref[:, MISC_IF:MISC_IF + LANE]) + ifb_ref[...]
    logsig = jnp.minimum(z, 0.0) - jnp.log1p(jnp.exp(-jnp.abs(z)))
    gates = jnp.where(valid, jnp.where(is_i, z, logsig), jnp.where(is_i, NEG, 0.0))
    cum = _cumsum_rows(tri_ref[...], jnp.where(is_i, 0.0, gates))
    pad = jnp.zeros((LANE - L, LANE), F32)
    gates_t = jnp.concatenate([gates, pad], axis=0).T
    cum_t = jnp.concatenate([cum, pad], axis=0).T
    causal = _iota((L, L), 1) <= _iota((L, L), 0)
    m_all = m_ref[0]
    yield

    for h in range(heads):
        sl = slice(h * hd, (h + 1) * hd)
        q = qk[:, sl]
        kx = qk[:, width + h * hd:width + (h + 1) * hd] * (hd ** -0.5)
        vh = v[:, sl]
        b_col = cum[:, heads + h:heads + h + 1]
        i_col = gates[:, h:h + 1]
        b_row = cum_t[heads + h:heads + h + 1, :L]
        i_row = gates_t[h:h + 1, :L]
        b_end = b_col[L - 1:L, :]
        m_prev = jnp.sum(jnp.where(lane == h, m_all, 0.0), axis=1, keepdims=True)

        log_inter = b_col + m_prev
        dmat = jnp.where(causal, b_col - b_row + i_row, NEG)
        m_q = jnp.maximum(log_inter, jnp.max(dmat, axis=-1, keepdims=True))
        w_inter = jnp.exp(log_inter - m_q)
        qb = q.astype(BF16)
        s = _dot_nt(qb, kx.astype(BF16)) * jnp.exp(dmat - m_q)
        c_h = c_ref[0, h]
        n_h = n_ref[0, h:h + 1, :]
        yield
        num = w_inter * _dot(qb, c_h.astype(BF16)) + _dot(s.astype(BF16), vh.astype(BF16))
        den = w_inter * jnp.sum(q * n_h, axis=-1, keepdims=True) + jnp.sum(s, axis=-1, keepdims=True)
        hcell = num / jnp.maximum(jnp.abs(den), jnp.exp(-m_q))
        yield

        g_col = b_end - b_col + i_col
        m_new = jnp.maximum(b_end + m_prev, jnp.max(g_col, axis=0, keepdims=True))
        a_st = jnp.exp(b_end + m_prev - m_new)
        wkk = jnp.exp(g_col - m_new) * kx
        c_ref[0, h] = a_st * c_h + _dot_tn(wkk.astype(BF16), vh.astype(BF16))
        n_ref[0, h:h + 1, :] = a_st * n_h + jnp.sum(wkk, axis=0, keepdims=True)
        m_all = jnp.where(lane == h, m_new, m_all)

        mu = jnp.mean(hcell, axis=-1, keepdims=True)
        hc = hcell - mu
        var = jnp.mean(hc * hc, axis=-1, keepdims=True)
        yh = hc * lax.rsqrt(var + ML_LN_EPS) * nw_ref[:, sl] * og[:, sl]
        y_ref[:, sl] = yh[L - rows:, :].astype(y_ref.dtype)
        yield

    m_ref[0] = m_all


N_RW_IN, N_ML_IN = 20, 13


def _mixers_kernel(*refs, rows, lead, width, heads, hd):
    rw_in = refs[:N_RW_IN]
    ml_in = refs[N_RW_IN:N_RW_IN + N_ML_IN]
    outs = refs[N_RW_IN + N_ML_IN + 2:N_RW_IN + N_ML_IN + 8]
    scratch = refs[N_RW_IN + N_ML_IN + 8:]
    (rkv_ref, misc_ref, sh_rkv_ref, sh_misc_ref, s0_ref), rw_prm = rw_in[:5], rw_in[5:]
    (qk_ref, v_ref, o_ref, misc2_ref, conv0_ref, c0_ref, n0_ref, m0_ref), ml_prm = ml_in[:8], ml_in[8:]
    y_rw_ref, s_ref, y_ml_ref, c_ref, n_ref, m_ref = outs
    carry_rkv, carry_misc = scratch[:2]
    ext = scratch[-1]
    c = pl.program_id(1)

    @pl.when(c == 0)
    def _():
        carry_rkv[...] = sh_rkv_ref[0]
        carry_misc[...] = sh_misc_ref[0]
        s_ref[...] = s0_ref[...]
        c_ref[...] = c0_ref[...]
        n_ref[...] = n0_ref[...]
        m_ref[...] = m0_ref[...]
        ext[0:SUBLANE, :] = jnp.zeros((SUBLANE, ext.shape[1]), F32)

    parts = [
        _rwkv_steps(c, (rkv_ref, misc_ref) + tuple(rw_prm) + (y_rw_ref, s_ref) + tuple(scratch[:-1]),
                    rows=rows, lead=lead, width=width),
        _mlstm_steps(c, (qk_ref, v_ref, o_ref, misc2_ref, conv0_ref) + tuple(ml_prm)
                     + (y_ml_ref, c_ref, n_ref, m_ref, ext), rows=rows, lead=lead, heads=heads, hd=hd),
    ]
    while parts:
        for part in list(parts):
            if next(part, StopIteration) is StopIteration:
                parts.remove(part)


def _mixers(proj, y_rw_prev, y_ml_prev, rw_state, ml_state, rw_prm, ml_prm, *,
            batch, n_chunks, rows, seq_row, lead, width, heads, hd, d_model):
    n = proj.shape[0]
    n_pairs = width // LANE
    rkv_blk = (2 * d_model + 2 * width) // (3 * width)
    qk_blk = (2 * d_model) // (2 * width)
    v_blk = (2 * d_model + 2 * width + 3 * width) // width
    misc_blk = (proj.shape[1] - MISC_COLS) // MISC_COLS
    kern = functools.partial(_mixers_kernel, rows=rows, lead=lead, width=width, heads=heads, hd=hd)
    full = lambda shape: pl.BlockSpec(shape, lambda b, c: (0,) * len(shape))
    per_seq = lambda shape: pl.BlockSpec(shape, lambda b, c: (b,) + (0,) * (len(shape) - 1))
    cols = lambda w_, blk: pl.BlockSpec((rows, w_), lambda b, c: (seq_row(b, c), blk))
    state_specs = [per_seq((1, n_pairs, LANE, LANE)), per_seq((1, heads, hd, hd)),
                   per_seq((1, heads, hd)), per_seq((1, 1, LANE))]
    rw_specs = [
        cols(3 * width, rkv_blk), cols(MISC_COLS, misc_blk),
        per_seq((1, 1, 3 * width)), per_seq((1, 1, 3 * LANE)), state_specs[0],
        full((1, 3 * width)), full((1, 3 * LANE)),
        full((1, width)), full((LANE, width)), full((1, width)), full((LANE, width)), full((2 * LANE, width)),
        full((1, width)), full((1, width)), full((1, width)), full((1, width)), full((1, width)),
        full((CHUNK, CHUNK)), full((LANE, LANE)), full((2 * CHUNK, 2 * CHUNK)),
    ]
    ml_specs = [
        cols(2 * width, qk_blk), cols(width, v_blk), cols(width, v_blk + 1), cols(MISC_COLS, misc_blk),
        per_seq((1, SUBLANE, 2 * width)), *state_specs[1:],
        full((4, 2 * width)), full((1, 2 * width)), full((1, LANE)), full((1, width)), full((CHUNK, CHUNK)),
    ]
    assert len(rw_specs) == N_RW_IN and len(ml_specs) == N_ML_IN and lead >= SUBLANE
    in_specs = rw_specs + ml_specs + [pl.BlockSpec(memory_space=pl.ANY)] * 2
    y_spec = pl.BlockSpec((rows, width), lambda b, c: (seq_row(b, c), 0))
    out_specs = [y_spec, state_specs[0], y_spec] + state_specs[1:]
    sds = jax.ShapeDtypeStruct
    out_shape = [
        sds((n, width), BF16), sds((batch, n_pairs, LANE, LANE), F32),
        sds((n, width), BF16), sds((batch, heads, hd, hd), F32), sds((batch, heads, hd), F32),
        sds((batch, 1, LANE), F32),
    ]
    scratch = [pltpu.VMEM((1, 3 * width), F32), pltpu.VMEM((1, 3 * LANE), F32)]
    scratch += [pltpu.VMEM((CHUNK, width), F32) for _ in range(6)]
    scratch += [pltpu.VMEM((CHUNK + SUBLANE, 2 * width), F32)]
    n_in = len(in_specs)
    return pl.pallas_call(
        kern, grid=(batch, n_chunks), in_specs=in_specs, out_specs=out_specs, out_shape=out_shape,
        scratch_shapes=scratch, input_output_aliases={n_in - 2: 0, n_in - 1: 2},
        compiler_params=_params("arbitrary", "arbitrary"), name="mixers",
    )(proj, proj, *rw_state, *rw_prm, proj, proj, proj, proj, *ml_state, *ml_prm, y_rw_prev, y_ml_prev)


def _pad_rows(w, rows):
    return jnp.pad(w, ((0, rows - w.shape[0]), (0, 0)))


def _swiglu_weights(w_gate, w_up, w_down):
    f = w_gate.shape[1]
    fp = -(-f // FF_TILE) * FF_TILE
    pad_cols = lambda w: jnp.pad(w.astype(BF16), ((0, 0), (0, fp - f)))
    return pad_cols(w_gate), pad_cols(w_up), _pad_rows(w_down.astype(BF16), fp)


def kernel(x_prompt, x_sample, state_rwkv_shift, state_rwkv_wkv, state_mlstm_conv, state_mlstm_C,
           state_mlstm_n, state_mlstm_m, meta_tokens, ffn1_norm, ffn1_w_gate, ffn1_w_up, ffn1_w_down,
           mix_norm, w_in, rw_mu, rw_w0, rw_w2, rw_a0, rw_a2, rw_g2, rw_kk, rw_ka, rw_rk, rw_ln_w, rw_ln_b,
           ml_conv_w, ml_conv_b, ml_i_b, ml_f_b, ml_norm_w, w_br_rw, w_br_ml, w_out,
           ffn2_norm, ffn2_w_gate, ffn2_w_up, ffn2_w_down, final_norm):
    assert ffn1_norm.shape[0] == 1, "single-layer trunk"
    B, T, D = x_prompt.shape
    Bs, Ts, _ = x_sample.shape
    n_meta = meta_tokens.shape[0]
    W = rw_w0.shape[-1]
    dl, al, gl = rw_w2.shape[1], rw_a2.shape[1], rw_g2.shape[1]
    rw_heads, rw_hd = rw_rk.shape[1], rw_rk.shape[2]
    Wm = ml_norm_w.shape[-1]
    H = ml_i_b.shape[-1]
    hd = Wm // H
    K = ml_conv_w.shape[1]
    L = CHUNK
    assert rw_hd == LANE // 2 and dl == LANE // 2 and al == LANE // 2 and gl <= 2 * LANE
    assert K == 4 and hd % LANE == 0 and 2 * H <= LANE and W == Wm and 2 * W == D
    assert Ts <= L and Ts % (2 * SUBLANE) == 0 and D % COL_TILE == 0 and (11 * D // 2) % MISC_COLS == 0

    lead = (-n_meta) % L
    head_rows = lead + n_meta
    n_head = head_rows // L
    n_chunks = n_head + T // L
    lead_s = L - Ts
    n_main = B * T
    sample0 = n_main + B * head_rows
    N = sample0 + Bs * Ts
    assert T % L == 0 and T % ROW_TILE == 0 and N % PROJ_ROW_TILE == 0
    head = jnp.concatenate([jnp.zeros((lead, D), F32), meta_tokens.astype(F32)], axis=0)
    x_extra = jnp.concatenate([jnp.broadcast_to(head[None], (B, head_rows, D)).reshape(B * head_rows, D),
                               x_sample.reshape(Bs * Ts, D)], axis=0)

    ffn1_w = _swiglu_weights(ffn1_w_gate[0], ffn1_w_up[0], ffn1_w_down[0])
    ffn2_w = _swiglu_weights(ffn2_w_gate[0], ffn2_w_up[0], ffn2_w_down[0])
    wi = w_in[0]
    o_lora = 3 * W
    o_qk = o_lora + dl + al + gl
    o_v = o_qk + 2 * Wm
    o_o = o_v + Wm
    o_i = o_o + Wm
    o_gate = o_i + 2 * H
    zc = lambda n: jnp.zeros((D, n), wi.dtype)
    misc_w = jnp.concatenate([wi[:, o_lora:o_qk], zc(MISC_IF - (dl + al + gl)),
                              wi[:, o_i:o_gate], zc(MISC_COLS - MISC_IF - 2 * H)], axis=1)
    w_proj = jnp.concatenate([wi[:, o_gate:], wi[:, o_qk:o_v], wi[:, :o_lora], wi[:, o_v:o_i], misc_w],
                             axis=1).astype(BF16)

    def misc_vec(v_lora, fill=0.0):
        return jnp.pad(v_lora, [(0, 0)] * (v_lora.ndim - 1) + [(0, 3 * LANE - v_lora.shape[-1])],
                       constant_values=fill)

    row = lambda v: v.reshape(1, -1).astype(F32)
    rw_prm = (
        row(rw_mu[0, :o_lora]), misc_vec(row(rw_mu[0, o_lora:])),
        row(rw_w0[0]), _pad_rows(rw_w2[0], LANE).astype(BF16),
        row(rw_a0[0]), jnp.concatenate([jnp.zeros((dl, W), F32), rw_a2[0]], axis=0).astype(BF16),
        _pad_rows(rw_g2[0], 2 * LANE).astype(BF16),
        row(rw_kk[0]), row(rw_ka[0]), row(rw_rk[0]), row(rw_ln_w[0]), row(rw_ln_b[0]),
    )
    ifb = jnp.pad(jnp.concatenate([ml_i_b[0], ml_f_b[0]]).reshape(1, 2 * H), ((0, 0), (0, LANE - 2 * H)))
    tri = jnp.tril(jnp.ones((L, L), BF16))
    same_head = jnp.kron(jnp.eye(LANE // rw_hd, dtype=BF16), jnp.ones((rw_hd, rw_hd), BF16))
    rw_prm = rw_prm + (tri, same_head, jnp.eye(2 * L, dtype=F32))
    ml_prm = (ml_conv_w[0].astype(F32), row(ml_conv_b[0]), ifb.astype(F32), row(ml_norm_w[0]), tri)

    def rw_state_in(shift, wkv):
        b = shift.shape[0]
        s = wkv.reshape(b, rw_heads // 2, 2, rw_hd, rw_hd)
        z = jnp.zeros_like(s[:, :, 0])
        bd = jnp.concatenate([jnp.concatenate([s[:, :, 0], z], axis=-1),
                              jnp.concatenate([z, s[:, :, 1]], axis=-1)], axis=-2)
        return shift[:, None, :o_lora], misc_vec(shift[:, None, o_lora:]), bd

    def rw_state_out(bd):
        h = rw_hd
        return jnp.stack([bd[:, :, :h, :h], bd[:, :, h:, h:]], axis=2).reshape(bd.shape[0], rw_heads, h, h)

    def conv_in(buf):
        return jnp.pad(buf, ((0, 0), (SUBLANE - (K - 1), 0), (0, 0)))

    def m_in(m):
        return jnp.pad(m, ((0, 0), (0, LANE - H)))[:, None, :]

    x1 = _ffn((x_prompt, x_extra), row(ffn1_norm[0]), *ffn1_w, row(final_norm),
              split_out=None, final_norm=False)
    proj = _proj(x1, row(mix_norm[0]), w_proj)

    zeros = lambda *s: jnp.zeros(s, F32)
    main_chunks = T // L

    def prompt_row(b, c):
        return jnp.where(c < n_head, n_main // L + b * n_head + c, b * main_chunks + c - n_head)

    seqs = (
        dict(batch=B, n_chunks=n_chunks, rows=L, seq_row=prompt_row, lead=lead),
        dict(batch=Bs, n_chunks=1, rows=Ts, seq_row=lambda b, c: sample0 // Ts + b, lead=lead_s),
    )
    rw_states = (
        rw_state_in(zeros(B, o_qk), zeros(B, rw_heads, rw_hd, rw_hd)),
        rw_state_in(state_rwkv_shift[0], state_rwkv_wkv[0]),
    )
    ml_states = (
        (conv_in(zeros(B, K - 1, 2 * Wm)), zeros(B, H, hd, hd), zeros(B, H, hd), m_in(zeros(B, H))),
        (conv_in(state_mlstm_conv[0]), state_mlstm_C[0], state_mlstm_n[0], m_in(state_mlstm_m[0])),
    )
    y_rw = jnp.zeros((N, W), BF16)
    y_ml = jnp.zeros((N, Wm), BF16)
    rw_out, ml_out = [], []
    for seq, rws, mls in zip(seqs, rw_states, ml_states):
        y_rw, s_end, y_ml, c_end, n_end, m_end = _mixers(proj, y_rw, y_ml, rws, mls, rw_prm, ml_prm,
                                                         width=W, heads=H, hd=hd, d_model=D, **seq)
        rw_out.append(rw_state_out(s_end))
        ml_out.append((c_end, n_end, m_end[:, 0, :H]))

    x2 = _merge(x1, y_rw, y_ml, proj, w_br_rw[0].astype(BF16), w_br_ml[0].astype(BF16), w_out[0].astype(BF16))
    y_prompt, y_extra = _ffn((x2,), row(ffn2_norm[0]), *ffn2_w, row(final_norm),
                             split_out=(B, T), final_norm=True)

    c_rkv = 2 * D + 2 * Wm
    c_misc = proj.shape[1] - MISC_COLS

    def seq_states(row0, batch, t_len):
        last = row0 + (jnp.arange(batch)[:, None] + 1) * t_len - (K - 1) + jnp.arange(K - 1)[None, :]
        tail = proj[last.reshape(-1)].reshape(batch, K - 1, -1)
        shift = jnp.concatenate([tail[:, -1, c_rkv:c_rkv + 3 * W],
                                 tail[:, -1, c_misc:c_misc + dl + al + gl]], axis=-1)
        return shift, tail[:, :, 2 * D:2 * D + 2 * Wm]

    p_shift, p_conv = seq_states(0, B, T)
    s_shift, s_conv = seq_states(sample0, Bs, Ts)
    y_sample = y_extra[B * head_rows:].reshape(Bs, Ts, D)
    d1 = lambda a: a[None]
    return (y_prompt, y_sample,
            d1(p_shift), d1(rw_out[0]), d1(p_conv), d1(ml_out[0][0]), d1(ml_out[0][1]), d1(ml_out[0][2]),
            d1(s_shift), d1(rw_out[1]), d1(s_conv), d1(ml_out[1][0]), d1(ml_out[1][1]), d1(ml_out[1][2]))
```

```python
import functools

import jax
import jax.numpy as jnp
from jax import lax
from jax.experimental import pallas as pl
from jax.experimental.pallas import tpu as pltpu

F32 = jnp.float32
BF16 = jnp.bfloat16

LANE = 128
SUBLANE = 8
VMEM_LIMIT_BYTES = 56 * 1024 * 1024
CHUNK = 64
ROW_TILE = 512
PROJ_ROW_TILE = 1024
FF_TILE = 512
PROJ_COL_TILE = 1024
MERGE_ROW_TILE = 256
MISC_COLS = 512
MISC_IF = 384
RMS_EPS = 1e-6
RW_GN_EPS = 64e-5
ML_LN_EPS = 1e-5
NEG = -1e30
DECAY_SCALE = 0.6065306597126334


def _dot(a, b):
    return jnp.dot(a, b, preferred_element_type=F32)


def _dot_nt(a, b):
    return lax.dot_general(a, b, (((1,), (1,)), ((), ())), preferred_element_type=F32)


def _dot_tn(a, b):
    return lax.dot_general(a, b, (((0,), (0,)), ((), ())), preferred_element_type=F32)


def _split3(x):
    h1 = x.astype(BF16)
    r1 = x - h1.astype(F32)
    h2 = r1.astype(BF16)
    r2 = r1 - h2.astype(F32)
    return h1, h2, r2.astype(BF16)


def _cumsum_rows(tri, x):
    n = x.shape[1]
    y = _dot(tri, jnp.concatenate(_split3(x), axis=1))
    return y[:, :n] + y[:, n:2 * n] + y[:, 2 * n:]


def _segsum(x, seg):
    rows = x.shape[0]
    h1, h2, _ = _split3(x)
    y = _dot(jnp.concatenate([h1, h2], axis=0), seg)
    return y[:rows] + y[rows:]


def _sigmoid(z):
    return jax.nn.sigmoid(z)


def _rmsnorm(x, g):
    return x * lax.rsqrt(jnp.mean(x * x, axis=-1, keepdims=True) + RMS_EPS) * g


def _iota(shape, dim):
    return lax.broadcasted_iota(jnp.int32, shape, dim)


def _params(*semantics):
    return pltpu.CompilerParams(dimension_semantics=semantics, vmem_limit_bytes=VMEM_LIMIT_BYTES)


def _ffn_kernel(*refs, n_f, n_main, split_in, split_out, final_norm):
    n_x = 2 if split_in else 1
    n_o = 2 if split_out else 1
    x_refs = refs[:n_x]
    g_ref, wg_ref, wu_ref, wd_ref, fg_ref = refs[n_x:n_x + 5]
    o_refs = refs[n_x + 5:n_x + 5 + n_o]
    xn_ref, acc_ref = refs[n_x + 5 + n_o:]
    i = pl.program_id(0)
    j = pl.program_id(1)

    def load_x():
        if split_in:
            return jnp.where(i < n_main, x_refs[0][...], x_refs[1][...])
        return x_refs[0][...]

    @pl.when(j == 0)
    def _():
        xn_ref[...] = _rmsnorm(load_x(), g_ref[...]).astype(BF16)
        acc_ref[...] = jnp.zeros_like(acc_ref)

    xn = xn_ref[...]
    gate = _dot(xn, wg_ref[...])
    h = (gate * _sigmoid(gate) * _dot(xn, wu_ref[...])).astype(BF16)
    acc_ref[...] += _dot(h, wd_ref[...])

    def finish(o_ref):
        y = load_x() + 0.5 * acc_ref[...]
        if final_norm:
            y = _rmsnorm(y, fg_ref[...])
        o_ref[...] = y

    last = j == n_f - 1
    if split_out:
        pl.when(last & (i < n_main))(lambda: finish(o_refs[0]))
        pl.when(last & (i >= n_main))(lambda: finish(o_refs[1]))
    else:
        pl.when(last)(lambda: finish(o_refs[0]))


def _ffn(xs, norm_g, wg, wu, wd, final_g, *, split_out, final_norm):
    split_in = len(xs) == 2
    d = xs[-1].shape[-1]
    if split_in:
        b, t = xs[0].shape[:2]
        n = b * t + xs[1].shape[0]
    else:
        n = xs[0].shape[0]
        b, t = split_out if split_out else (1, n)
    n_f = wd.shape[0] // FF_TILE
    tpb = t // ROW_TILE
    nm = b * tpb
    kern = functools.partial(_ffn_kernel, n_f=n_f, n_main=nm, split_in=split_in,
                             split_out=bool(split_out), final_norm=final_norm)
    whole = pl.BlockSpec((ROW_TILE, d), lambda i, j: (i, 0))
    main = pl.BlockSpec((None, ROW_TILE, d),
                        lambda i, j: (jnp.minimum(i, nm - 1) // tpb, jnp.minimum(i, nm - 1) % tpb, 0))
    extra = pl.BlockSpec((ROW_TILE, d), lambda i, j: (jnp.maximum(i - nm, 0), 0))
    return pl.pallas_call(
        kern,
        grid=(n // ROW_TILE, n_f),
        in_specs=([main, extra] if split_in else [whole]) + [
            pl.BlockSpec((1, d), lambda i, j: (0, 0)),
            pl.BlockSpec((d, FF_TILE), lambda i, j: (0, j)),
            pl.BlockSpec((d, FF_TILE), lambda i, j: (0, j)),
            pl.BlockSpec((FF_TILE, d), lambda i, j: (j, 0)),
            pl.BlockSpec((1, d), lambda i, j: (0, 0)),
        ],
        out_specs=[main, extra] if split_out else whole,
        out_shape=([jax.ShapeDtypeStruct((b, t, d), F32), jax.ShapeDtypeStruct((n - b * t, d), F32)]
                   if split_out else jax.ShapeDtypeStruct((n, d), F32)),
        scratch_shapes=[pltpu.VMEM((ROW_TILE, d), BF16), pltpu.VMEM((ROW_TILE, d), F32)],
        compiler_params=_params("arbitrary", "arbitrary"),
        name="ffn_final" if final_norm else "ffn",
    )(*xs, norm_g, wg, wu, wd, final_g)


def _proj_kernel(x_ref, g_ref, w_ref, wm_ref, o_ref, om_ref, xn_ref):
    @pl.when(pl.program_id(1) == 0)
    def _():
        xn_ref[...] = _rmsnorm(x_ref[...], g_ref[...]).astype(BF16)
        om_ref[...] = _dot(xn_ref[...], wm_ref[...])

    o_ref[...] = _dot(xn_ref[...], w_ref[...])


def _proj(x, norm_g, w, w_misc):
    n, d = x.shape
    cols = w.shape[1]
    tm, tn = PROJ_ROW_TILE, PROJ_COL_TILE
    return pl.pallas_call(
        _proj_kernel,
        grid=(n // tm, cols // tn),
        in_specs=[
            pl.BlockSpec((tm, d), lambda i, j: (i, 0)),
            pl.BlockSpec((1, d), lambda i, j: (0, 0)),
            pl.BlockSpec((d, tn), lambda i, j: (0, j)),
            pl.BlockSpec((d, MISC_COLS), lambda i, j: (0, 0)),
        ],
        out_specs=[pl.BlockSpec((tm, tn), lambda i, j: (i, j)),
                   pl.BlockSpec((tm, MISC_COLS), lambda i, j: (i, 0))],
        out_shape=[jax.ShapeDtypeStruct((n, cols), F32), jax.ShapeDtypeStruct((n, MISC_COLS), F32)],
        scratch_shapes=[pltpu.VMEM((tm, d), BF16)],
        compiler_params=_params("parallel", "arbitrary"),
        name="in_proj",
    )(x, norm_g, w, w_misc)


def _merge_kernel(x_ref, yrw_ref, yml_ref, ga_ref, gb_ref, wrw_ref, wml_ref, wo_ref, o_ref):
    merged = (_sigmoid(ga_ref[...]) * _dot(yrw_ref[...], wrw_ref[...])
              + _sigmoid(gb_ref[...]) * _dot(yml_ref[...], wml_ref[...]))
    o_ref[...] = x_ref[...] + _dot(merged.astype(BF16), wo_ref[...])


def _merge(x, y_rw, y_ml, proj, w_rw, w_ml, w_out):
    n, d = x.shape
    w = y_rw.shape[1]
    tm = MERGE_ROW_TILE
    rows = lambda width, blk: pl.BlockSpec((tm, width), lambda i: (i, blk))
    resident = lambda shape: pl.BlockSpec(shape, lambda i: (0, 0), pipeline_mode=pl.Buffered(1))
    return pl.pallas_call(
        _merge_kernel,
        grid=(n // tm,),
        in_specs=[rows(d, 0), rows(w, 0), rows(w, 0), rows(d, 0), rows(d, 1),
                  resident((w, d)), resident((w, d)), resident((d, d))],
        out_specs=rows(d, 0),
        out_shape=jax.ShapeDtypeStruct((n, d), F32),
        compiler_params=_params("parallel"),
        name="merge",
    )(x, y_rw, y_ml, proj, proj, w_rw, w_ml, w_out)


def _each(f, *lists):
    return [f(*xs) for xs in zip(*lists)]


def _rwkv_pairs(r, k, v, a, lw, g, s0, kkw, kaw, rkw, lnw, lnb, cst, result):
    L = r[0].shape[0]
    tri, seg, head0, strict, incl, col_head0, eye2, blockdiag = cst
    bf = lambda x: x.astype(BF16)

    kk_ = _each(lambda k_, w_: k_ * w_, k, kkw)
    kn = _each(lambda x: _segsum(x * x, seg), kk_)
    cum = _each(lambda x: _cumsum_rows(tri, x), lw)
    yield
    kk = _each(lambda x, n_: x / jnp.maximum(jnp.sqrt(n_), 1e-12), kk_, kn)
    k2 = _each(lambda k_, a_, w_: k_ * (1.0 + (a_ - 1.0) * w_), k, a, kaw)
    kka = _each(lambda x, a_: x * a_, kk, a)
    c_end = [c[L - 1:L, :] for c in cum]
    e_inv = [jnp.exp(-c) for c in cum]
    e_rem = _each(lambda c, ce: jnp.exp(ce - c), cum, c_end)
    at = _each(lambda x, c, l_: bf(-x * jnp.exp(c - l_)), kk, cum, lw)
    rt = _each(lambda x, c: bf(x * jnp.exp(c)), r, cum)
    bt = _each(lambda x, e: bf(x * e), kka, e_inv)
    kt = _each(lambda x, e: bf(x * e), k2, e_inv)
    vb = [bf(x) for x in v]
    zero = jnp.zeros_like(vb[0])

    def by_head(x):
        return jnp.concatenate([jnp.where(head0, x, zero), jnp.where(head0, zero, x)], axis=0)

    lhs = _each(lambda x, y_: jnp.concatenate([x, y_], axis=0), at, rt)
    ab = _each(lambda l_, x: _dot_nt(l_, by_head(x)), lhs, bt)
    ak = _each(lambda l_, x: _dot_nt(l_, by_head(x)), lhs, kt)
    yield
    a_rb = [jnp.where(incl, x[L:], 0.0) for x in ab]
    a_ak = [jnp.where(strict, x[:L], 0.0) for x in ak]
    a_rk = [jnp.where(incl, x[L:], 0.0) for x in ak]

    def blockdiag2(x):
        n_cat = jnp.where(strict, x[:L], 0.0)
        return jnp.concatenate([jnp.where(col_head0, n_cat, 0.0), jnp.where(col_head0, 0.0, n_cat)], axis=0)

    p = [blockdiag2(x) for x in ab]
    t_inv = [eye2 + x for x in p]
    sb = [bf(x) for x in s0]
    vs = [by_head(x) for x in vb]
    x0 = _each(lambda a_, s_, m_, v_: _dot_nt(a_, s_) + _dot(bf(m_), v_), at, sb, a_ak, vs)
    for _ in range(L.bit_length() - 2):
        p = [_dot(bf(x), bf(x)) for x in p]
        yield
        t_inv = _each(lambda t_, x: t_ + _dot(bf(t_), bf(x)), t_inv, p)
    yield
    us = _each(lambda t_, x: _dot(bf(t_), by_head(bf(x))), t_inv, x0)
    yield
    y = _each(lambda r_, s_, m1, m2, u_, v_: _dot_nt(r_, s_) + _dot(
        bf(jnp.concatenate([m1, m2], axis=1)), jnp.concatenate([bf(u_), v_], axis=0)),
        rt, sb, a_rb, a_rk, us, vs)
    upd = _each(lambda u_, v_, x1, x2, e: _dot_tn(
        jnp.concatenate([bf(u_[:L] + u_[L:]), v_], axis=0),
        jnp.concatenate([bf(x1 * e), bf(x2 * e)], axis=0)), us, vb, kka, k2, e_rem)
    yield
    s_new = _each(lambda s_, ce, u_: s_ * jnp.exp(ce) + jnp.where(blockdiag, u_, 0.0), s0, c_end, upd)

    hd = float(LANE // 2)
    mean = [_segsum(x, seg) / hd for x in y]
    bsum = _each(lambda r_, k_, w_: _segsum(r_ * k_ * w_, seg), r, k2, rkw)
    yield
    yc = _each(lambda x, m_: x - m_, y, mean)
    var = [_segsum(x * x, seg) / hd for x in yc]
    yield
    out = _each(lambda x, v_, w_, b_, bs, vv, g_: (x * lax.rsqrt(v_ + RW_GN_EPS) * w_ + b_ + bs * vv) * g_,
                yc, var, lnw, lnb, bsum, v, g)
    result.extend([out, s_new])


def _rwkv_steps(c, refs, *, rows, lead, width):
    (rkv_ref, misc_ref, mu_rkv_ref, mu_misc_ref, w0_ref, w2_ref, a0_ref, a2_ref, g2_ref,
     kk_ref, ka_ref, rk_ref, lnw_ref, lnb_ref, tri_ref, seg_ref, eye_ref, y_ref, s_ref,
     carry_rkv, carry_misc, r_s, k_s, v_s, a_s, lw_s, g_s) = refs
    L = CHUNK
    p = rkv_ref[...]
    pm = misc_ref[:, :3 * LANE]
    if rows < L:
        p = jnp.concatenate([jnp.zeros((L - rows, p.shape[1]), F32), p], axis=0)
        pm = jnp.concatenate([jnp.zeros((L - rows, pm.shape[1]), F32), pm], axis=0)
    row = _iota((L, 1), 0)
    first = jnp.where(c == 0, lead, 0)
    valid = row >= first
    p = jnp.where(valid, p, 0.0)
    pm = jnp.where(valid, pm, 0.0)

    def shift_mix(cur, carry_ref, mu):
        prev = jnp.where(row == first, carry_ref[...], pltpu.roll(cur, 1, axis=0))
        carry_ref[...] = cur[L - 1:L, :]
        return jnp.where(valid, cur + mu * (prev - cur), 0.0)

    u = shift_mix(p, carry_rkv, mu_rkv_ref[...])
    um = shift_mix(pm, carry_misc, mu_misc_ref[...])

    lora = um[:, :LANE]
    wl = w0_ref[...] + _dot(jnp.tanh(lora).astype(BF16), w2_ref[...])
    lw_s[...] = jnp.where(valid, -DECAY_SCALE * _sigmoid(wl), 0.0)
    a_s[...] = _sigmoid(a0_ref[...] + _dot(lora.astype(BF16), a2_ref[...]))
    g_s[...] = _dot(_sigmoid(um[:, LANE:]).astype(BF16), g2_ref[...])
    r_s[...] = u[:, :width]
    k_s[...] = u[:, width:2 * width]
    v_s[...] = u[:, 2 * width:]
    yield

    lane = _iota((1, LANE), 1)
    half = LANE // 2
    col2 = _iota((L, 2 * L), 1)
    t_i = _iota((L, 2 * L), 0)
    s_i = col2 & (L - 1)
    cst = (
        tri_ref[...], seg_ref[...],
        lane < half,
        s_i < t_i, s_i <= t_i,
        col2 < L,
        eye_ref[...],
        (_iota((LANE, LANE), 0) < half) == (_iota((LANE, LANE), 1) < half),
    )
    sls = [slice(j * LANE, (j + 1) * LANE) for j in range(width // LANE)]
    pick = lambda ref: [ref[:, sl] for sl in sls]
    result = []
    yield from _rwkv_pairs(pick(r_s), pick(k_s), pick(v_s), pick(a_s), pick(lw_s), pick(g_s),
                           [s_ref[0, j] for j in range(len(sls))],
                           pick(kk_ref), pick(ka_ref), pick(rk_ref), pick(lnw_ref), pick(lnb_ref), cst, result)
    outs, s_new = result
    for j, sl in enumerate(sls):
        s_ref[0, j] = s_new[j]
        y_ref[:, sl] = outs[j][L - rows:, :].astype(y_ref.dtype)


def _mlstm_steps(c, refs, *, rows, lead, heads, hd):
    (qk_ref, v_ref, o_ref, misc_ref, conv0_ref, cw_ref, cb_ref, ifb_ref, nw_ref, tri_ref,
     y_ref, c_ref, n_ref, m_ref, ext) = refs
    L = CHUNK
    width = heads * hd

    def chunk_rows(x):
        if rows < L:
            return jnp.concatenate([jnp.zeros((L - rows, x.shape[1]), F32), x], axis=0)
        return x

    row = _iota((L, 1), 0)
    valid = row >= jnp.where(c == 0, lead, 0)

    carried = jnp.concatenate([jnp.zeros((lead - SUBLANE, 2 * width), F32), conv0_ref[0],
                               jnp.zeros((L - lead, 2 * width), F32)], axis=0)
    use_carried = (c == 0) & (row >= lead - SUBLANE) & (row < lead)
    ext[SUBLANE:SUBLANE + L, :] = jnp.where(valid, chunk_rows(qk_ref[...]), jnp.where(use_carried, carried, 0.0))
    conv = cb_ref[...]
    for j in range(4):
        conv = conv + cw_ref[j:j + 1, :] * ext[SUBLANE - 3 + j:SUBLANE - 3 + j + L, :]
    ext[0:SUBLANE, :] = ext[L:L + SUBLANE, :]
    qk = jnp.where(valid, conv * _sigmoid(conv), 0.0)
    v = jnp.where(valid, chunk_rows(v_ref[...]), 0.0)
    og = _sigmoid(chunk_rows(o_ref[...]))
    yield

    lane = _iota((1, LANE), 1)
    is_i = lane < heads
    z = chunk_rows(misc_ref[:, MISC_IF:MISC_IF + LANE]) + ifb_ref[...]
    logsig = jnp.minimum(z, 0.0) - jnp.log1p(jnp.exp(-jnp.abs(z)))
    gates = jnp.where(valid, jnp.where(is_i, z, logsig), jnp.where(is_i, NEG, 0.0))
    cum = _cumsum_rows(tri_ref[...], jnp.where(is_i, 0.0, gates))
    pad = jnp.zeros((LANE - L, LANE), F32)
    gates_t = jnp.concatenate([gates, pad], axis=0).T
    cum_t = jnp.concatenate([cum, pad], axis=0).T
    causal = _iota((L, L), 1) <= _iota((L, L), 0)
    m_all = m_ref[0]
    yield

    for h in range(heads):
        sl = slice(h * hd, (h + 1) * hd)
        q = qk[:, sl]
        kx = qk[:, width + h * hd:width + (h + 1) * hd] * (hd ** -0.5)
        vh = v[:, sl]
        b_col = cum[:, heads + h:heads + h + 1]
        i_col = gates[:, h:h + 1]
        b_row = cum_t[heads + h:heads + h + 1, :L]
        i_row = gates_t[h:h + 1, :L]
        b_end = b_col[L - 1:L, :]
        m_prev = jnp.sum(jnp.where(lane == h, m_all, 0.0), axis=1, keepdims=True)

        log_inter = b_col + m_prev
        dmat = jnp.where(causal, b_col - b_row + i_row, NEG)
        m_q = jnp.maximum(log_inter, jnp.max(dmat, axis=-1, keepdims=True))
        w_inter = jnp.exp(log_inter - m_q)
        qb = q.astype(BF16)
        s = _dot_nt(qb, kx.astype(BF16)) * jnp.exp(dmat - m_q)
        c_h = c_ref[0, h]
        n_h = n_ref[0, h:h + 1, :]
        yield
        num = w_inter * _dot(qb, c_h.astype(BF16)) + _dot(s.astype(BF16), vh.astype(BF16))
        den = w_inter * jnp.sum(q * n_h, axis=-1, keepdims=True) + jnp.sum(s, axis=-1, keepdims=True)
        hcell = num / jnp.maximum(jnp.abs(den), jnp.exp(-m_q))
        yield

        g_col = b_end - b_col + i_col
        m_new = jnp.maximum(b_end + m_prev, jnp.max(g_col, axis=0, keepdims=True))
        a_st = jnp.exp(b_end + m_prev - m_new)
        wkk = jnp.exp(g_col - m_new) * kx
        c_ref[0, h] = a_st * c_h + _dot_tn(wkk.astype(BF16), vh.astype(BF16))
        n_ref[0, h:h + 1, :] = a_st * n_h + jnp.sum(wkk, axis=0, keepdims=True)
        m_all = jnp.where(lane == h, m_new, m_all)

        mu = jnp.mean(hcell, axis=-1, keepdims=True)
        hc = hcell - mu
        var = jnp.mean(hc * hc, axis=-1, keepdims=True)
        yh = hc * lax.rsqrt(var + ML_LN_EPS) * nw_ref[:, sl] * og[:, sl]
        y_ref[:, sl] = yh[L - rows:, :].astype(y_ref.dtype)
        yield

    m_ref[0] = m_all


N_RW_IN, N_ML_IN = 20, 13


def _mixers_kernel(*refs, rows, lead, width, heads, hd):
    rw_in = refs[:N_RW_IN]
    ml_in = refs[N_RW_IN:N_RW_IN + N_ML_IN]
    outs = refs[N_RW_IN + N_ML_IN + 2:N_RW_IN + N_ML_IN + 8]
    scratch = refs[N_RW_IN + N_ML_IN + 8:]
    (rkv_ref, misc_ref, sh_rkv_ref, sh_misc_ref, s0_ref), rw_prm = rw_in[:5], rw_in[5:]
    (qk_ref, v_ref, o_ref, misc2_ref, conv0_ref, c0_ref, n0_ref, m0_ref), ml_prm = ml_in[:8], ml_in[8:]
    y_rw_ref, s_ref, y_ml_ref, c_ref, n_ref, m_ref = outs
    carry_rkv, carry_misc = scratch[:2]
    ext = scratch[-1]
    c = pl.program_id(1)

    @pl.when(c == 0)
    def _():
        carry_rkv[...] = sh_rkv_ref[0]
        carry_misc[...] = sh_misc_ref[0]
        s_ref[...] = s0_ref[...]
        c_ref[...] = c0_ref[...]
        n_ref[...] = n0_ref[...]
        m_ref[...] = m0_ref[...]
        ext[0:SUBLANE, :] = jnp.zeros((SUBLANE, ext.shape[1]), F32)

    parts = [
        _mlstm_steps(c, (qk_ref, v_ref, o_ref, misc2_ref, conv0_ref) + tuple(ml_prm)
                     + (y_ml_ref, c_ref, n_ref, m_ref, ext), rows=rows, lead=lead, heads=heads, hd=hd),
        _rwkv_steps(c, (rkv_ref, misc_ref) + tuple(rw_prm) + (y_rw_ref, s_ref) + tuple(scratch[:-1]),
                    rows=rows, lead=lead, width=width),
    ]
    while parts:
        for part in list(parts):
            if next(part, StopIteration) is StopIteration:
                parts.remove(part)


def _mixers(proj, proj_misc, y_rw_prev, y_ml_prev, rw_state, ml_state, rw_prm, ml_prm, *,
            batch, n_chunks, rows, seq_row, lead, width, heads, hd, d_model):
    n = proj.shape[0]
    n_pairs = width // LANE
    rkv_blk = (2 * d_model + 2 * width) // (3 * width)
    qk_blk = (2 * d_model) // (2 * width)
    v_blk = (2 * d_model + 2 * width + 3 * width) // width
    kern = functools.partial(_mixers_kernel, rows=rows, lead=lead, width=width, heads=heads, hd=hd)
    full = lambda shape: pl.BlockSpec(shape, lambda b, c: (0,) * len(shape))
    per_seq = lambda shape: pl.BlockSpec(shape, lambda b, c: (b,) + (0,) * (len(shape) - 1))
    cols = lambda w_, blk: pl.BlockSpec((rows, w_), lambda b, c: (seq_row(b, c), blk))
    state_specs = [per_seq((1, n_pairs, LANE, LANE)), per_seq((1, heads, hd, hd)),
                   per_seq((1, heads, hd)), per_seq((1, 1, LANE))]
    rw_specs = [
        cols(3 * width, rkv_blk), cols(MISC_COLS, 0),
        per_seq((1, 1, 3 * width)), per_seq((1, 1, 3 * LANE)), state_specs[0],
        full((1, 3 * width)), full((1, 3 * LANE)),
        full((1, width)), full((LANE, width)), full((1, width)), full((LANE, width)), full((2 * LANE, width)),
        full((1, width)), full((1, width)), full((1, width)), full((1, width)), full((1, width)),
        full((CHUNK, CHUNK)), full((LANE, LANE)), full((2 * CHUNK, 2 * CHUNK)),
    ]
    ml_specs = [
        cols(2 * width, qk_blk), cols(width, v_blk), cols(width, v_blk + 1), cols(MISC_COLS, 0),
        per_seq((1, SUBLANE, 2 * width)), *state_specs[1:],
        full((4, 2 * width)), full((1, 2 * width)), full((1, LANE)), full((1, width)), full((CHUNK, CHUNK)),
    ]
    assert len(rw_specs) == N_RW_IN and len(ml_specs) == N_ML_IN and lead >= SUBLANE
    in_specs = rw_specs + ml_specs + [pl.BlockSpec(memory_space=pl.ANY)] * 2
    y_spec = pl.BlockSpec((rows, width), lambda b, c: (seq_row(b, c), 0))
    out_specs = [y_spec, state_specs[0], y_spec] + state_specs[1:]
    sds = jax.ShapeDtypeStruct
    out_shape = [
        sds((n, width), BF16), sds((batch, n_pairs, LANE, LANE), F32),
        sds((n, width), BF16), sds((batch, heads, hd, hd), F32), sds((batch, heads, hd), F32),
        sds((batch, 1, LANE), F32),
    ]
    scratch = [pltpu.VMEM((1, 3 * width), F32), pltpu.VMEM((1, 3 * LANE), F32)]
    scratch += [pltpu.VMEM((CHUNK, width), F32) for _ in range(6)]
    scratch += [pltpu.VMEM((CHUNK + SUBLANE, 2 * width), F32)]
    n_in = len(in_specs)
    return pl.pallas_call(
        kern, grid=(batch, n_chunks), in_specs=in_specs, out_specs=out_specs, out_shape=out_shape,
        scratch_shapes=scratch, input_output_aliases={n_in - 2: 0, n_in - 1: 2},
        compiler_params=_params("arbitrary", "arbitrary"), name="mixers",
    )(proj, proj_misc, *rw_state, *rw_prm, proj, proj, proj, proj_misc, *ml_state, *ml_prm, y_rw_prev, y_ml_prev)


def _pad_rows(w, rows):
    return jnp.pad(w, ((0, rows - w.shape[0]), (0, 0)))


def _swiglu_weights(w_gate, w_up, w_down):
    f = w_gate.shape[1]
    fp = -(-f // FF_TILE) * FF_TILE
    pad_cols = lambda w: jnp.pad(w.astype(BF16), ((0, 0), (0, fp - f)))
    return pad_cols(w_gate), pad_cols(w_up), _pad_rows(w_down.astype(BF16), fp)


def kernel(x_prompt, x_sample, state_rwkv_shift, state_rwkv_wkv, state_mlstm_conv, state_mlstm_C,
           state_mlstm_n, state_mlstm_m, meta_tokens, ffn1_norm, ffn1_w_gate, ffn1_w_up, ffn1_w_down,
           mix_norm, w_in, rw_mu, rw_w0, rw_w2, rw_a0, rw_a2, rw_g2, rw_kk, rw_ka, rw_rk, rw_ln_w, rw_ln_b,
           ml_conv_w, ml_conv_b, ml_i_b, ml_f_b, ml_norm_w, w_br_rw, w_br_ml, w_out,
           ffn2_norm, ffn2_w_gate, ffn2_w_up, ffn2_w_down, final_norm):
    assert ffn1_norm.shape[0] == 1, "single-layer trunk"
    B, T, D = x_prompt.shape
    Bs, Ts, _ = x_sample.shape
    n_meta = meta_tokens.shape[0]
    W = rw_w0.shape[-1]
    dl, al, gl = rw_w2.shape[1], rw_a2.shape[1], rw_g2.shape[1]
    rw_heads, rw_hd = rw_rk.shape[1], rw_rk.shape[2]
    Wm = ml_norm_w.shape[-1]
    H = ml_i_b.shape[-1]
    hd = Wm // H
    K = ml_conv_w.shape[1]
    L = CHUNK
    assert rw_hd == LANE // 2 and dl == LANE // 2 and al == LANE // 2 and gl <= 2 * LANE
    assert K == 4 and hd % LANE == 0 and 2 * H <= LANE and W == Wm and 2 * W == D
    assert Ts <= L and Ts % (2 * SUBLANE) == 0 and (11 * D // 2) % PROJ_COL_TILE == 0

    lead = (-n_meta) % L
    head_rows = lead + n_meta
    n_head = head_rows // L
    n_chunks = n_head + T // L
    lead_s = L - Ts
    n_main = B * T
    sample0 = n_main + B * head_rows
    N = sample0 + Bs * Ts
    assert T % L == 0 and T % ROW_TILE == 0 and N % PROJ_ROW_TILE == 0
    head = jnp.concatenate([jnp.zeros((lead, D), F32), meta_tokens.astype(F32)], axis=0)
    x_extra = jnp.concatenate([jnp.broadcast_to(head[None], (B, head_rows, D)).reshape(B * head_rows, D),
                               x_sample.reshape(Bs * Ts, D)], axis=0)

    ffn1_w = _swiglu_weights(ffn1_w_gate[0], ffn1_w_up[0], ffn1_w_down[0])
    ffn2_w = _swiglu_weights(ffn2_w_gate[0], ffn2_w_up[0], ffn2_w_down[0])
    wi = w_in[0]
    o_lora = 3 * W
    o_qk = o_lora + dl + al + gl
    o_v = o_qk + 2 * Wm
    o_o = o_v + Wm
    o_i = o_o + Wm
    o_gate = o_i + 2 * H
    zc = lambda n: jnp.zeros((D, n), wi.dtype)
    misc_w = jnp.concatenate([wi[:, o_lora:o_qk], zc(MISC_IF - (dl + al + gl)),
                              wi[:, o_i:o_gate], zc(MISC_COLS - MISC_IF - 2 * H)], axis=1)
    w_proj = jnp.concatenate([wi[:, o_gate:], wi[:, o_qk:o_v], wi[:, :o_lora], wi[:, o_v:o_i]],
                             axis=1).astype(BF16)

    def misc_vec(v_lora, fill=0.0):
        return jnp.pad(v_lora, [(0, 0)] * (v_lora.ndim - 1) + [(0, 3 * LANE - v_lora.shape[-1])],
                       constant_values=fill)

    row = lambda v: v.reshape(1, -1).astype(F32)
    rw_prm = (
        row(rw_mu[0, :o_lora]), misc_vec(row(rw_mu[0, o_lora:])),
        row(rw_w0[0]), _pad_rows(rw_w2[0], LANE).astype(BF16),
        row(rw_a0[0]), jnp.concatenate([jnp.zeros((dl, W), F32), rw_a2[0]], axis=0).astype(BF16),
        _pad_rows(rw_g2[0], 2 * LANE).astype(BF16),
        row(rw_kk[0]), row(rw_ka[0]), row(rw_rk[0]), row(rw_ln_w[0]), row(rw_ln_b[0]),
    )
    ifb = jnp.pad(jnp.concatenate([ml_i_b[0], ml_f_b[0]]).reshape(1, 2 * H), ((0, 0), (0, LANE - 2 * H)))
    tri = jnp.tril(jnp.ones((L, L), BF16))
    same_head = jnp.kron(jnp.eye(LANE // rw_hd, dtype=BF16), jnp.ones((rw_hd, rw_hd), BF16))
    rw_prm = rw_prm + (tri, same_head, jnp.eye(2 * L, dtype=F32))
    ml_prm = (ml_conv_w[0].astype(F32), row(ml_conv_b[0]), ifb.astype(F32), row(ml_norm_w[0]), tri)

    def rw_state_in(shift, wkv):
        b = shift.shape[0]
        s = wkv.reshape(b, rw_heads // 2, 2, rw_hd, rw_hd)
        z = jnp.zeros_like(s[:, :, 0])
        bd = jnp.concatenate([jnp.concatenate([s[:, :, 0], z], axis=-1),
                              jnp.concatenate([z, s[:, :, 1]], axis=-1)], axis=-2)
        return shift[:, None, :o_lora], misc_vec(shift[:, None, o_lora:]), bd

    def rw_state_out(bd):
        h = rw_hd
        return jnp.stack([bd[:, :, :h, :h], bd[:, :, h:, h:]], axis=2).reshape(bd.shape[0], rw_heads, h, h)

    def conv_in(buf):
        return jnp.pad(buf, ((0, 0), (SUBLANE - (K - 1), 0), (0, 0)))

    def m_in(m):
        return jnp.pad(m, ((0, 0), (0, LANE - H)))[:, None, :]

    x1 = _ffn((x_prompt, x_extra), row(ffn1_norm[0]), *ffn1_w, row(final_norm),
              split_out=None, final_norm=False)
    proj, proj_misc = _proj(x1, row(mix_norm[0]), w_proj, misc_w.astype(BF16))

    zeros = lambda *s: jnp.zeros(s, F32)
    main_chunks = T // L

    def prompt_row(b, c):
        return jnp.where(c < n_head, n_main // L + b * n_head + c, b * main_chunks + c - n_head)

    seqs = (
        dict(batch=B, n_chunks=n_chunks, rows=L, seq_row=prompt_row, lead=lead),
        dict(batch=Bs, n_chunks=1, rows=Ts, seq_row=lambda b, c: sample0 // Ts + b, lead=lead_s),
    )
    rw_states = (
        rw_state_in(zeros(B, o_qk), zeros(B, rw_heads, rw_hd, rw_hd)),
        rw_state_in(state_rwkv_shift[0], state_rwkv_wkv[0]),
    )
    ml_states = (
        (conv_in(zeros(B, K - 1, 2 * Wm)), zeros(B, H, hd, hd), zeros(B, H, hd), m_in(zeros(B, H))),
        (conv_in(state_mlstm_conv[0]), state_mlstm_C[0], state_mlstm_n[0], m_in(state_mlstm_m[0])),
    )
    y_rw = jnp.zeros((N, W), BF16)
    y_ml = jnp.zeros((N, Wm), BF16)
    rw_out, ml_out = [], []
    for seq, rws, mls in zip(seqs, rw_states, ml_states):
        y_rw, s_end, y_ml, c_end, n_end, m_end = _mixers(proj, proj_misc, y_rw, y_ml, rws, mls, rw_prm, ml_prm,
                                                         width=W, heads=H, hd=hd, d_model=D, **seq)
        rw_out.append(rw_state_out(s_end))
        ml_out.append((c_end, n_end, m_end[:, 0, :H]))

    x2 = _merge(x1, y_rw, y_ml, proj, w_br_rw[0].astype(BF16), w_br_ml[0].astype(BF16), w_out[0].astype(BF16))
    y_prompt, y_extra = _ffn((x2,), row(ffn2_norm[0]), *ffn2_w, row(final_norm),
                             split_out=(B, T), final_norm=True)

    c_rkv = 2 * D + 2 * Wm

    def seq_states(row0, batch, t_len):
        last = row0 + (jnp.arange(batch)[:, None] + 1) * t_len - (K - 1) + jnp.arange(K - 1)[None, :]
        tail = proj[last.reshape(-1)].reshape(batch, K - 1, -1)
        shift = jnp.concatenate([tail[:, -1, c_rkv:c_rkv + 3 * W],
                                 proj_misc[last[:, -1], :dl + al + gl]], axis=-1)
        return shift, tail[:, :, 2 * D:2 * D + 2 * Wm]

    p_shift, p_conv = seq_states(0, B, T)
    s_shift, s_conv = seq_states(sample0, Bs, Ts)
    y_sample = y_extra[B * head_rows:].reshape(Bs, Ts, D)
    d1 = lambda a: a[None]
    return (y_prompt, y_sample,
            d1(p_shift), d1(rw_out[0]), d1(p_conv), d1(ml_out[0][0]), d1(ml_out[0][1]), d1(ml_out[0][2]),
            d1(s_shift), d1(rw_out[1]), d1(s_conv), d1(ml_out[1][0]), d1(ml_out[1][1]), d1(ml_out[1][2]))
```

```python
import functools

import jax
import jax.numpy as jnp
from jax import lax
from jax.experimental import pallas as pl
from jax.experimental.pallas import tpu as pltpu

F32 = jnp.float32
BF16 = jnp.bfloat16

LANE = 128
SUBLANE = 8
VMEM_LIMIT_BYTES = 56 * 1024 * 1024
CHUNK = 64
ROW_TILE = 512
PROJ_ROW_TILE = 1024
FF_TILE = 512
PROJ_COL_TILE = 1024
MERGE_ROW_TILE = 256
MISC_COLS = 512
MISC_IF = 384
RMS_EPS = 1e-6
RW_GN_EPS = 64e-5
ML_LN_EPS = 1e-5
NEG = -1e30
DECAY_SCALE = 0.6065306597126334


def _dot(a, b):
    return jnp.dot(a, b, preferred_element_type=F32)


def _dot_nt(a, b):
    return lax.dot_general(a, b, (((1,), (1,)), ((), ())), preferred_element_type=F32)


def _dot_tn(a, b):
    return lax.dot_general(a, b, (((0,), (0,)), ((), ())), preferred_element_type=F32)


def _split3(x):
    h1 = x.astype(BF16)
    r1 = x - h1.astype(F32)
    h2 = r1.astype(BF16)
    r2 = r1 - h2.astype(F32)
    return h1, h2, r2.astype(BF16)


def _cumsum_rows(tri, x):
    n = x.shape[1]
    y = _dot(tri, jnp.concatenate(_split3(x), axis=1))
    return y[:, :n] + y[:, n:2 * n] + y[:, 2 * n:]


def _segsum(x, seg):
    rows = x.shape[0]
    h1, h2, _ = _split3(x)
    y = _dot(jnp.concatenate([h1, h2], axis=0), seg)
    return y[:rows] + y[rows:]


def _sigmoid(z):
    return jax.nn.sigmoid(z)


def _rmsnorm(x, g):
    return x * lax.rsqrt(jnp.mean(x * x, axis=-1, keepdims=True) + RMS_EPS) * g


def _iota(shape, dim):
    return lax.broadcasted_iota(jnp.int32, shape, dim)


def _params(*semantics):
    return pltpu.CompilerParams(dimension_semantics=semantics, vmem_limit_bytes=VMEM_LIMIT_BYTES)


def _ffn_kernel(*refs, n_f, n_main, split_in, split_out, final_norm):
    n_x = 2 if split_in else 1
    n_o = 2 if split_out else 1
    x_refs = refs[:n_x]
    g_ref, wg_ref, wu_ref, wd_ref, fg_ref = refs[n_x:n_x + 5]
    o_refs = refs[n_x + 5:n_x + 5 + n_o]
    xn_ref, acc_ref = refs[n_x + 5 + n_o:]
    i = pl.program_id(0)
    j = pl.program_id(1)

    def load_x():
        if split_in:
            return jnp.where(i < n_main, x_refs[0][...], x_refs[1][...])
        return x_refs[0][...]

    @pl.when(j == 0)
    def _():
        xn_ref[...] = _rmsnorm(load_x(), g_ref[...]).astype(BF16)
        acc_ref[...] = jnp.zeros_like(acc_ref)

    xn = xn_ref[...]
    gate = _dot(xn, wg_ref[...])
    h = (gate * _sigmoid(gate) * _dot(xn, wu_ref[...])).astype(BF16)
    acc_ref[...] += _dot(h, wd_ref[...])

    def finish(o_ref):
        y = load_x() + 0.5 * acc_ref[...]
        if final_norm:
            y = _rmsnorm(y, fg_ref[...])
        o_ref[...] = y

    last = j == n_f - 1
    if split_out:
        pl.when(last & (i < n_main))(lambda: finish(o_refs[0]))
        pl.when(last & (i >= n_main))(lambda: finish(o_refs[1]))
    else:
        pl.when(last)(lambda: finish(o_refs[0]))


def _ffn(xs, norm_g, wg, wu, wd, final_g, *, split_out, final_norm):
    split_in = len(xs) == 2
    d = xs[-1].shape[-1]
    if split_in:
        b, t = xs[0].shape[:2]
        n = b * t + xs[1].shape[0]
    else:
        n = xs[0].shape[0]
        b, t = split_out if split_out else (1, n)
    n_f = wd.shape[0] // FF_TILE
    tpb = t // ROW_TILE
    nm = b * tpb
    kern = functools.partial(_ffn_kernel, n_f=n_f, n_main=nm, split_in=split_in,
                             split_out=bool(split_out), final_norm=final_norm)
    whole = pl.BlockSpec((ROW_TILE, d), lambda i, j: (i, 0))
    main = pl.BlockSpec((None, ROW_TILE, d),
                        lambda i, j: (jnp.minimum(i, nm - 1) // tpb, jnp.minimum(i, nm - 1) % tpb, 0))
    extra = pl.BlockSpec((ROW_TILE, d), lambda i, j: (jnp.maximum(i - nm, 0), 0))
    return pl.pallas_call(
        kern,
        grid=(n // ROW_TILE, n_f),
        in_specs=([main, extra] if split_in else [whole]) + [
            pl.BlockSpec((1, d), lambda i, j: (0, 0)),
            pl.BlockSpec((d, FF_TILE), lambda i, j: (0, j)),
            pl.BlockSpec((d, FF_TILE), lambda i, j: (0, j)),
            pl.BlockSpec((FF_TILE, d), lambda i, j: (j, 0)),
            pl.BlockSpec((1, d), lambda i, j: (0, 0)),
        ],
        out_specs=[main, extra] if split_out else whole,
        out_shape=([jax.ShapeDtypeStruct((b, t, d), F32), jax.ShapeDtypeStruct((n - b * t, d), F32)]
                   if split_out else jax.ShapeDtypeStruct((n, d), F32)),
        scratch_shapes=[pltpu.VMEM((ROW_TILE, d), BF16), pltpu.VMEM((ROW_TILE, d), F32)],
        compiler_params=_params("arbitrary", "arbitrary"),
        name="ffn_final" if final_norm else "ffn",
    )(*xs, norm_g, wg, wu, wd, final_g)


def _proj_kernel(x_ref, g_ref, w_ref, wm_ref, o_ref, om_ref, xn_ref):
    @pl.when(pl.program_id(1) == 0)
    def _():
        xn_ref[...] = _rmsnorm(x_ref[...], g_ref[...]).astype(BF16)
        om_ref[...] = _dot(xn_ref[...], wm_ref[...])

    o_ref[...] = _dot(xn_ref[...], w_ref[...])


def _proj(x, norm_g, w, w_misc):
    n, d = x.shape
    cols = w.shape[1]
    tm, tn = PROJ_ROW_TILE, PROJ_COL_TILE
    return pl.pallas_call(
        _proj_kernel,
        grid=(n // tm, cols // tn),
        in_specs=[
            pl.BlockSpec((tm, d), lambda i, j: (i, 0)),
            pl.BlockSpec((1, d), lambda i, j: (0, 0)),
            pl.BlockSpec((d, tn), lambda i, j: (0, j)),
            pl.BlockSpec((d, MISC_COLS), lambda i, j: (0, 0)),
        ],
        out_specs=[pl.BlockSpec((tm, tn), lambda i, j: (i, j)),
                   pl.BlockSpec((tm, MISC_COLS), lambda i, j: (i, 0))],
        out_shape=[jax.ShapeDtypeStruct((n, cols), F32), jax.ShapeDtypeStruct((n, MISC_COLS), F32)],
        scratch_shapes=[pltpu.VMEM((tm, d), BF16)],
        compiler_params=_params("parallel", "arbitrary"),
        name="in_proj",
    )(x, norm_g, w, w_misc)


def _merge_kernel(x_ref, yrw_ref, yml_ref, ga_ref, gb_ref, wrw_ref, wml_ref, wo_ref, o_ref):
    merged = (_sigmoid(ga_ref[...]) * _dot(yrw_ref[...], wrw_ref[...])
              + _sigmoid(gb_ref[...]) * _dot(yml_ref[...], wml_ref[...]))
    o_ref[...] = x_ref[...] + _dot(merged.astype(BF16), wo_ref[...])


def _merge(x, y_rw, y_ml, proj, w_rw, w_ml, w_out):
    n, d = x.shape
    w = y_rw.shape[1]
    tm = MERGE_ROW_TILE
    rows = lambda width, blk: pl.BlockSpec((tm, width), lambda i: (i, blk))
    resident = lambda shape: pl.BlockSpec(shape, lambda i: (0, 0), pipeline_mode=pl.Buffered(1))
    return pl.pallas_call(
        _merge_kernel,
        grid=(n // tm,),
        in_specs=[rows(d, 0), rows(w, 0), rows(w, 0), rows(d, 0), rows(d, 1),
                  resident((w, d)), resident((w, d)), resident((d, d))],
        out_specs=rows(d, 0),
        out_shape=jax.ShapeDtypeStruct((n, d), F32),
        compiler_params=_params("parallel"),
        name="merge",
    )(x, y_rw, y_ml, proj, proj, w_rw, w_ml, w_out)


def _each(f, *lists):
    return [f(*xs) for xs in zip(*lists)]


def _rwkv_pairs(r, k, v, a, lw, g, s0, kkw, kaw, rkw, lnw, lnb, cst, result):
    L = r[0].shape[0]
    tri, seg, head0, strict, incl, col_head0, eye2, blockdiag = cst
    bf = lambda x: x.astype(BF16)

    kk_ = _each(lambda k_, w_: k_ * w_, k, kkw)
    kn = _each(lambda x: _segsum(x * x, seg), kk_)
    cum = _each(lambda x: _cumsum_rows(tri, x), lw)
    yield
    kk = _each(lambda x, n_: x / jnp.maximum(jnp.sqrt(n_), 1e-12), kk_, kn)
    k2 = _each(lambda k_, a_, w_: k_ * (1.0 + (a_ - 1.0) * w_), k, a, kaw)
    kka = _each(lambda x, a_: x * a_, kk, a)
    c_end = [c[L - 1:L, :] for c in cum]
    e_inv = [jnp.exp(-c) for c in cum]
    e_rem = _each(lambda c, ce: jnp.exp(ce - c), cum, c_end)
    at = _each(lambda x, c, l_: bf(-x * jnp.exp(c - l_)), kk, cum, lw)
    rt = _each(lambda x, c: bf(x * jnp.exp(c)), r, cum)
    bt = _each(lambda x, e: bf(x * e), kka, e_inv)
    kt = _each(lambda x, e: bf(x * e), k2, e_inv)
    vb = [bf(x) for x in v]
    zero = jnp.zeros_like(vb[0])

    def by_head(x):
        return jnp.concatenate([jnp.where(head0, x, zero), jnp.where(head0, zero, x)], axis=0)

    lhs = _each(lambda x, y_: jnp.concatenate([x, y_], axis=0), at, rt)
    ab = _each(lambda l_, x: _dot_nt(l_, by_head(x)), lhs, bt)
    ak = _each(lambda l_, x: _dot_nt(l_, by_head(x)), lhs, kt)
    yield
    a_rb = [jnp.where(incl, x[L:], 0.0) for x in ab]
    a_ak = [jnp.where(strict, x[:L], 0.0) for x in ak]
    a_rk = [jnp.where(incl, x[L:], 0.0) for x in ak]

    def blockdiag2(x):
        n_cat = jnp.where(strict, x[:L], 0.0)
        return jnp.concatenate([jnp.where(col_head0, n_cat, 0.0), jnp.where(col_head0, 0.0, n_cat)], axis=0)

    p = [blockdiag2(x) for x in ab]
    t_inv = [eye2 + x for x in p]
    sb = [bf(x) for x in s0]
    vs = [by_head(x) for x in vb]
    x0 = _each(lambda a_, s_, m_, v_: _dot_nt(a_, s_) + _dot(bf(m_), v_), at, sb, a_ak, vs)
    for _ in range(L.bit_length() - 2):
        p = [_dot(bf(x), bf(x)) for x in p]
        yield
        t_inv = _each(lambda t_, x: t_ + _dot(bf(t_), bf(x)), t_inv, p)
    yield
    us = _each(lambda t_, x: _dot(bf(t_), by_head(bf(x))), t_inv, x0)
    yield
    y = _each(lambda r_, s_, m1, m2, u_, v_: _dot_nt(r_, s_) + _dot(
        bf(jnp.concatenate([m1, m2], axis=1)), jnp.concatenate([bf(u_), v_], axis=0)),
        rt, sb, a_rb, a_rk, us, vs)
    upd = _each(lambda u_, v_, x1, x2, e: _dot_tn(
        jnp.concatenate([bf(u_[:L] + u_[L:]), v_], axis=0),
        jnp.concatenate([bf(x1 * e), bf(x2 * e)], axis=0)), us, vb, kka, k2, e_rem)
    yield
    s_new = _each(lambda s_, ce, u_: s_ * jnp.exp(ce) + jnp.where(blockdiag, u_, 0.0), s0, c_end, upd)

    hd = float(LANE // 2)
    mean = [_segsum(x, seg) / hd for x in y]
    bsum = _each(lambda r_, k_, w_: _segsum(r_ * k_ * w_, seg), r, k2, rkw)
    yield
    yc = _each(lambda x, m_: x - m_, y, mean)
    var = [_segsum(x * x, seg) / hd for x in yc]
    yield
    out = _each(lambda x, v_, w_, b_, bs, vv, g_: (x * lax.rsqrt(v_ + RW_GN_EPS) * w_ + b_ + bs * vv) * g_,
                yc, var, lnw, lnb, bsum, v, g)
    result.extend([out, s_new])


def _rwkv_steps(c, refs, *, rows, lead, width):
    (rkv_ref, misc_ref, mu_rkv_ref, mu_misc_ref, w0_ref, w2_ref, a0_ref, a2_ref, g2_ref,
     kk_ref, ka_ref, rk_ref, lnw_ref, lnb_ref, tri_ref, seg_ref, eye_ref, y_ref, s_ref,
     carry_rkv, carry_misc, r_s, k_s, v_s, a_s, lw_s, g_s) = refs
    L = CHUNK
    p = rkv_ref[...]
    pm = misc_ref[:, :3 * LANE]
    if rows < L:
        p = jnp.concatenate([jnp.zeros((L - rows, p.shape[1]), F32), p], axis=0)
        pm = jnp.concatenate([jnp.zeros((L - rows, pm.shape[1]), F32), pm], axis=0)
    row = _iota((L, 1), 0)
    first = jnp.where(c == 0, lead, 0)
    valid = row >= first
    p = jnp.where(valid, p, 0.0)
    pm = jnp.where(valid, pm, 0.0)

    def shift_mix(cur, carry_ref, mu):
        prev = jnp.where(row == first, carry_ref[...], pltpu.roll(cur, 1, axis=0))
        carry_ref[...] = cur[L - 1:L, :]
        return jnp.where(valid, cur + mu * (prev - cur), 0.0)

    u = shift_mix(p, carry_rkv, mu_rkv_ref[...])
    um = shift_mix(pm, carry_misc, mu_misc_ref[...])

    lora = um[:, :LANE]
    wl = w0_ref[...] + _dot(jnp.tanh(lora).astype(BF16), w2_ref[...])
    lw_s[...] = jnp.where(valid, -DECAY_SCALE * _sigmoid(wl), 0.0)
    a_s[...] = _sigmoid(a0_ref[...] + _dot(lora.astype(BF16), a2_ref[...]))
    g_s[...] = _dot(_sigmoid(um[:, LANE:]).astype(BF16), g2_ref[...])
    r_s[...] = u[:, :width]
    k_s[...] = u[:, width:2 * width]
    v_s[...] = u[:, 2 * width:]
    yield

    lane = _iota((1, LANE), 1)
    half = LANE // 2
    col2 = _iota((L, 2 * L), 1)
    t_i = _iota((L, 2 * L), 0)
    s_i = col2 & (L - 1)
    cst = (
        tri_ref[...], seg_ref[...],
        lane < half,
        s_i < t_i, s_i <= t_i,
        col2 < L,
        eye_ref[...],
        (_iota((LANE, LANE), 0) < half) == (_iota((LANE, LANE), 1) < half),
    )
    sls = [slice(j * LANE, (j + 1) * LANE) for j in range(width // LANE)]
    pick = lambda ref: [ref[:, sl] for sl in sls]
    result = []
    yield from _rwkv_pairs(pick(r_s), pick(k_s), pick(v_s), pick(a_s), pick(lw_s), pick(g_s),
                           [s_ref[0, j] for j in range(len(sls))],
                           pick(kk_ref), pick(ka_ref), pick(rk_ref), pick(lnw_ref), pick(lnb_ref), cst, result)
    outs, s_new = result
    for j, sl in enumerate(sls):
        s_ref[0, j] = s_new[j]
        y_ref[:, sl] = outs[j][L - rows:, :].astype(y_ref.dtype)


def _mlstm_steps(c, refs, *, rows, lead, heads, hd):
    (qk_ref, v_ref, o_ref, misc_ref, conv0_ref, cw_ref, cb_ref, ifb_ref, nw_ref, tri_ref,
     y_ref, c_ref, n_ref, m_ref, ext) = refs
    L = CHUNK
    width = heads * hd

    def chunk_rows(x):
        if rows < L:
            return jnp.concatenate([jnp.zeros((L - rows, x.shape[1]), F32), x], axis=0)
        return x

    row = _iota((L, 1), 0)
    valid = row >= jnp.where(c == 0, lead, 0)

    carried = jnp.concatenate([jnp.zeros((lead - SUBLANE, 2 * width), F32), conv0_ref[0],
                               jnp.zeros((L - lead, 2 * width), F32)], axis=0)
    use_carried = (c == 0) & (row >= lead - SUBLANE) & (row < lead)
    ext[SUBLANE:SUBLANE + L, :] = jnp.where(valid, chunk_rows(qk_ref[...]), jnp.where(use_carried, carried, 0.0))
    conv = cb_ref[...]
    for j in range(4):
        conv = conv + cw_ref[j:j + 1, :] * ext[SUBLANE - 3 + j:SUBLANE - 3 + j + L, :]
    ext[0:SUBLANE, :] = ext[L:L + SUBLANE, :]
    qk = jnp.where(valid, conv * _sigmoid(conv), 0.0)
    v = jnp.where(valid, chunk_rows(v_ref[...]), 0.0)
    og = _sigmoid(chunk_rows(o_ref[...]))
    yield

    lane = _iota((1, LANE), 1)
    is_i = lane < heads
    z = chunk_rows(misc_ref[:, MISC_IF:MISC_IF + LANE]) + ifb_ref[...]
    logsig = jnp.minimum(z, 0.0) - jnp.log1p(jnp.exp(-jnp.abs(z)))
    gates = jnp.where(valid, jnp.where(is_i, z, logsig), jnp.where(is_i, NEG, 0.0))
    cum = _cumsum_rows(tri_ref[...], jnp.where(is_i, 0.0, gates))
    pad = jnp.zeros((LANE - L, LANE), F32)
    gates_t = jnp.concatenate([gates, pad], axis=0).T
    cum_t = jnp.concatenate([cum, pad], axis=0).T
    causal = _iota((L, L), 1) <= _iota((L, L), 0)
    m_all = m_ref[0]
    yield

    for h in range(heads):
        sl = slice(h * hd, (h + 1) * hd)
        q = qk[:, sl]
        kx = qk[:, width + h * hd:width + (h + 1) * hd] * (hd ** -0.5)
        vh = v[:, sl]
        b_col = cum[:, heads + h:heads + h + 1]
        i_col = gates[:, h:h + 1]
        b_row = cum_t[heads + h:heads + h + 1, :L]
        i_row = gates_t[h:h + 1, :L]
        b_end = b_col[L - 1:L, :]
        m_prev = jnp.sum(jnp.where(lane == h, m_all, 0.0), axis=1, keepdims=True)

        log_inter = b_col + m_prev
        dmat = jnp.where(causal, b_col - b_row + i_row, NEG)
        m_q = jnp.maximum(log_inter, jnp.max(dmat, axis=-1, keepdims=True))
        w_inter = jnp.exp(log_inter - m_q)
        qb = q.astype(BF16)
        s = _dot_nt(qb, kx.astype(BF16)) * jnp.exp(dmat - m_q)
        c_h = c_ref[0, h]
        n_h = n_ref[0, h:h + 1, :]
        yield
        num = w_inter * _dot(qb, c_h.astype(BF16)) + _dot(s.astype(BF16), vh.astype(BF16))
        den = w_inter * jnp.sum(q * n_h, axis=-1, keepdims=True) + jnp.sum(s, axis=-1, keepdims=True)
        hcell = num / jnp.maximum(jnp.abs(den), jnp.exp(-m_q))
        yield

        g_col = b_end - b_col + i_col
        m_new = jnp.maximum(b_end + m_prev, jnp.max(g_col, axis=0, keepdims=True))
        a_st = jnp.exp(b_end + m_prev - m_new)
        wkk = jnp.exp(g_col - m_new) * kx
        c_ref[0, h] = a_st * c_h + _dot_tn(wkk.astype(BF16), vh.astype(BF16))
        n_ref[0, h:h + 1, :] = a_st * n_h + jnp.sum(wkk, axis=0, keepdims=True)
        m_all = jnp.where(lane == h, m_new, m_all)

        mu = jnp.mean(hcell, axis=-1, keepdims=True)
        hc = hcell - mu
        var = jnp.mean(hc * hc, axis=-1, keepdims=True)
        yh = hc * lax.rsqrt(var + ML_LN_EPS) * nw_ref[:, sl] * og[:, sl]
        y_ref[:, sl] = yh[L - rows:, :].astype(y_ref.dtype)
        yield

    m_ref[0] = m_all


N_RW_IN, N_ML_IN = 20, 13


def _mixers_kernel(*refs, rows, lead, width, heads, hd):
    rw_in = refs[:N_RW_IN]
    ml_in = refs[N_RW_IN:N_RW_IN + N_ML_IN]
    outs = refs[N_RW_IN + N_ML_IN + 2:N_RW_IN + N_ML_IN + 8]
    scratch = refs[N_RW_IN + N_ML_IN + 8:]
    (rkv_ref, misc_ref, sh_rkv_ref, sh_misc_ref, s0_ref), rw_prm = rw_in[:5], rw_in[5:]
    (qk_ref, v_ref, o_ref, misc2_ref, conv0_ref, c0_ref, n0_ref, m0_ref), ml_prm = ml_in[:8], ml_in[8:]
    y_rw_ref, s_ref, y_ml_ref, c_ref, n_ref, m_ref = outs
    carry_rkv, carry_misc = scratch[:2]
    ext = scratch[-1]
    c = pl.program_id(1)

    @pl.when(c == 0)
    def _():
        carry_rkv[...] = sh_rkv_ref[0]
        carry_misc[...] = sh_misc_ref[0]
        s_ref[...] = s0_ref[...]
        c_ref[...] = c0_ref[...]
        n_ref[...] = n0_ref[...]
        m_ref[...] = m0_ref[...]
        ext[0:SUBLANE, :] = jnp.zeros((SUBLANE, ext.shape[1]), F32)

    parts = [
        _mlstm_steps(c, (qk_ref, v_ref, o_ref, misc2_ref, conv0_ref) + tuple(ml_prm)
                     + (y_ml_ref, c_ref, n_ref, m_ref, ext), rows=rows, lead=lead, heads=heads, hd=hd),
        _rwkv_steps(c, (rkv_ref, misc_ref) + tuple(rw_prm) + (y_rw_ref, s_ref) + tuple(scratch[:-1]),
                    rows=rows, lead=lead, width=width),
    ]
    while parts:
        for part in list(parts):
            if next(part, StopIteration) is StopIteration:
                parts.remove(part)


def _mixers(proj, proj_misc, y_rw_prev, y_ml_prev, rw_state, ml_state, rw_prm, ml_prm, *,
            batch, n_chunks, rows, seq_row, lead, width, heads, hd, d_model):
    n = proj.shape[0]
    n_pairs = width // LANE
    rkv_blk = (2 * d_model + 2 * width) // (3 * width)
    qk_blk = (2 * d_model) // (2 * width)
    v_blk = (2 * d_model + 2 * width + 3 * width) // width
    kern = functools.partial(_mixers_kernel, rows=rows, lead=lead, width=width, heads=heads, hd=hd)
    full = lambda shape: pl.BlockSpec(shape, lambda b, c: (0,) * len(shape))
    per_seq = lambda shape: pl.BlockSpec(shape, lambda b, c: (b,) + (0,) * (len(shape) - 1))
    cols = lambda w_, blk: pl.BlockSpec((rows, w_), lambda b, c: (seq_row(b, c), blk))
    state_specs = [per_seq((1, n_pairs, LANE, LANE)), per_seq((1, heads, hd, hd)),
                   per_seq((1, heads, hd)), per_seq((1, 1, LANE))]
    rw_specs = [
        cols(3 * width, rkv_blk), cols(MISC_COLS, 0),
        per_seq((1, 1, 3 * width)), per_seq((1, 1, 3 * LANE)), state_specs[0],
        full((1, 3 * width)), full((1, 3 * LANE)),
        full((1, width)), full((LANE, width)), full((1, width)), full((LANE, width)), full((2 * LANE, width)),
        full((1, width)), full((1, width)), full((1, width)), full((1, width)), full((1, width)),
        full((CHUNK, CHUNK)), full((LANE, LANE)), full((2 * CHUNK, 2 * CHUNK)),
    ]
    ml_specs = [
        cols(2 * width, qk_blk), cols(width, v_blk), cols(width, v_blk + 1), cols(MISC_COLS, 0),
        per_seq((1, SUBLANE, 2 * width)), *state_specs[1:],
        full((4, 2 * width)), full((1, 2 * width)), full((1, LANE)), full((1, width)), full((CHUNK, CHUNK)),
    ]
    assert len(rw_specs) == N_RW_IN and len(ml_specs) == N_ML_IN and lead >= SUBLANE
    in_specs = rw_specs + ml_specs + [pl.BlockSpec(memory_space=pl.ANY)] * 2
    y_spec = pl.BlockSpec((rows, width), lambda b, c: (seq_row(b, c), 0))
    out_specs = [y_spec, state_specs[0], y_spec] + state_specs[1:]
    sds = jax.ShapeDtypeStruct
    out_shape = [
        sds((n, width), BF16), sds((batch, n_pairs, LANE, LANE), F32),
        sds((n, width), BF16), sds((batch, heads, hd, hd), F32), sds((batch, heads, hd), F32),
        sds((batch, 1, LANE), F32),
    ]
    scratch = [pltpu.VMEM((1, 3 * width), F32), pltpu.VMEM((1, 3 * LANE), F32)]
    scratch += [pltpu.VMEM((CHUNK, width), F32) for _ in range(6)]
    scratch += [pltpu.VMEM((CHUNK + SUBLANE, 2 * width), F32)]
    n_in = len(in_specs)
    return pl.pallas_call(
        kern, grid=(batch, n_chunks), in_specs=in_specs, out_specs=out_specs, out_shape=out_shape,
        scratch_shapes=scratch, input_output_aliases={n_in - 2: 0, n_in - 1: 2},
        compiler_params=_params("arbitrary", "arbitrary"), name="mixers",
    )(proj, proj_misc, *rw_state, *rw_prm, proj, proj, proj, proj_misc, *ml_state, *ml_prm, y_rw_prev, y_ml_prev)


def _pad_rows(w, rows):
    return jnp.pad(w, ((0, rows - w.shape[0]), (0, 0)))


def _swiglu_weights(w_gate, w_up, w_down):
    f = w_gate.shape[1]
    fp = -(-f // FF_TILE) * FF_TILE
    pad_cols = lambda w: jnp.pad(w.astype(BF16), ((0, 0), (0, fp - f)))
    return pad_cols(w_gate), pad_cols(w_up), _pad_rows(w_down.astype(BF16), fp)


def kernel(x_prompt, x_sample, state_rwkv_shift, state_rwkv_wkv, state_mlstm_conv, state_mlstm_C,
           state_mlstm_n, state_mlstm_m, meta_tokens, ffn1_norm, ffn1_w_gate, ffn1_w_up, ffn1_w_down,
           mix_norm, w_in, rw_mu, rw_w0, rw_w2, rw_a0, rw_a2, rw_g2, rw_kk, rw_ka, rw_rk, rw_ln_w, rw_ln_b,
           ml_conv_w, ml_conv_b, ml_i_b, ml_f_b, ml_norm_w, w_br_rw, w_br_ml, w_out,
           ffn2_norm, ffn2_w_gate, ffn2_w_up, ffn2_w_down, final_norm):
    assert ffn1_norm.shape[0] == 1, "single-layer trunk"
    B, T, D = x_prompt.shape
    Bs, Ts, _ = x_sample.shape
    n_meta = meta_tokens.shape[0]
    W = rw_w0.shape[-1]
    dl, al, gl = rw_w2.shape[1], rw_a2.shape[1], rw_g2.shape[1]
    rw_heads, rw_hd = rw_rk.shape[1], rw_rk.shape[2]
    Wm = ml_norm_w.shape[-1]
    H = ml_i_b.shape[-1]
    hd = Wm // H
    K = ml_conv_w.shape[1]
    L = CHUNK
    assert rw_hd == LANE // 2 and dl == LANE // 2 and al == LANE // 2 and gl <= 2 * LANE
    assert K == 4 and hd % LANE == 0 and 2 * H <= LANE and W == Wm and 2 * W == D
    assert Ts <= L and Ts % (2 * SUBLANE) == 0 and (11 * D // 2) % PROJ_COL_TILE == 0

    lead = (-n_meta) % L
    head_rows = lead + n_meta
    n_head = head_rows // L
    n_chunks = n_head + T // L
    lead_s = L - Ts
    n_main = B * T
    sample0 = n_main + B * head_rows
    N = sample0 + Bs * Ts
    assert T % L == 0 and T % ROW_TILE == 0 and N % PROJ_ROW_TILE == 0
    head = jnp.concatenate([jnp.zeros((lead, D), F32), meta_tokens.astype(F32)], axis=0)
    x_extra = jnp.concatenate([jnp.broadcast_to(head[None], (B, head_rows, D)).reshape(B * head_rows, D),
                               x_sample.reshape(Bs * Ts, D)], axis=0)

    ffn1_w = _swiglu_weights(ffn1_w_gate[0], ffn1_w_up[0], ffn1_w_down[0])
    ffn2_w = _swiglu_weights(ffn2_w_gate[0], ffn2_w_up[0], ffn2_w_down[0])
    wi = w_in[0]
    o_lora = 3 * W
    o_qk = o_lora + dl + al + gl
    o_v = o_qk + 2 * Wm
    o_o = o_v + Wm
    o_i = o_o + Wm
    o_gate = o_i + 2 * H
    zc = lambda n: jnp.zeros((D, n), wi.dtype)
    misc_w = jnp.concatenate([wi[:, o_lora:o_qk], zc(MISC_IF - (dl + al + gl)),
                              wi[:, o_i:o_gate], zc(MISC_COLS - MISC_IF - 2 * H)], axis=1)
    w_proj = jnp.concatenate([wi[:, o_gate:], wi[:, o_qk:o_v], wi[:, :o_lora], wi[:, o_v:o_i]],
                             axis=1).astype(BF16)

    def misc_vec(v_lora, fill=0.0):
        return jnp.pad(v_lora, [(0, 0)] * (v_lora.ndim - 1) + [(0, 3 * LANE - v_lora.shape[-1])],
                       constant_values=fill)

    row = lambda v: v.reshape(1, -1).astype(F32)
    rw_prm = (
        row(rw_mu[0, :o_lora]), misc_vec(row(rw_mu[0, o_lora:])),
        row(rw_w0[0]), _pad_rows(rw_w2[0], LANE).astype(BF16),
        row(rw_a0[0]), jnp.concatenate([jnp.zeros((dl, W), F32), rw_a2[0]], axis=0).astype(BF16),
        _pad_rows(rw_g2[0], 2 * LANE).astype(BF16),
        row(rw_kk[0]), row(rw_ka[0]), row(rw_rk[0]), row(rw_ln_w[0]), row(rw_ln_b[0]),
    )
    ifb = jnp.pad(jnp.concatenate([ml_i_b[0], ml_f_b[0]]).reshape(1, 2 * H), ((0, 0), (0, LANE - 2 * H)))
    tri = jnp.tril(jnp.ones((L, L), BF16))
    same_head = jnp.kron(jnp.eye(LANE // rw_hd, dtype=BF16), jnp.ones((rw_hd, rw_hd), BF16))
    rw_prm = rw_prm + (tri, same_head, jnp.eye(2 * L, dtype=F32))
    ml_prm = (ml_conv_w[0].astype(F32), row(ml_conv_b[0]), ifb.astype(F32), row(ml_norm_w[0]), tri)

    def rw_state_in(shift, wkv):
        b = shift.shape[0]
        s = wkv.reshape(b, rw_heads // 2, 2, rw_hd, rw_hd)
        z = jnp.zeros_like(s[:, :, 0])
        bd = jnp.concatenate([jnp.concatenate([s[:, :, 0], z], axis=-1),
                              jnp.concatenate([z, s[:, :, 1]], axis=-1)], axis=-2)
        return shift[:, None, :o_lora], misc_vec(shift[:, None, o_lora:]), bd

    def rw_state_out(bd):
        h = rw_hd
        return jnp.stack([bd[:, :, :h, :h], bd[:, :, h:, h:]], axis=2).reshape(bd.shape[0], rw_heads, h, h)

    def conv_in(buf):
        return jnp.pad(buf, ((0, 0), (SUBLANE - (K - 1), 0), (0, 0)))

    def m_in(m):
        return jnp.pad(m, ((0, 0), (0, LANE - H)))[:, None, :]

    x1 = _ffn((x_prompt, x_extra), row(ffn1_norm[0]), *ffn1_w, row(final_norm),
              split_out=None, final_norm=False)
    proj, proj_misc = _proj(x1, row(mix_norm[0]), w_proj, misc_w.astype(BF16))

    zeros = lambda *s: jnp.zeros(s, F32)
    main_chunks = T // L

    def prompt_row(b, c):
        return jnp.where(c < n_head, n_main // L + b * n_head + c, b * main_chunks + c - n_head)

    seqs = (
        dict(batch=B, n_chunks=n_chunks, rows=L, seq_row=prompt_row, lead=lead),
        dict(batch=Bs, n_chunks=1, rows=Ts, seq_row=lambda b, c: sample0 // Ts + b, lead=lead_s),
    )
    rw_states = (
        rw_state_in(zeros(B, o_qk), zeros(B, rw_heads, rw_hd, rw_hd)),
        rw_state_in(state_rwkv_shift[0], state_rwkv_wkv[0]),
    )
    ml_states = (
        (conv_in(zeros(B, K - 1, 2 * Wm)), zeros(B, H, hd, hd), zeros(B, H, hd), m_in(zeros(B, H))),
        (conv_in(state_mlstm_conv[0]), state_mlstm_C[0], state_mlstm_n[0], m_in(state_mlstm_m[0])),
    )
    y_rw = jnp.zeros((N, W), BF16)
    y_ml = jnp.zeros((N, Wm), BF16)
    rw_out, ml_out = [], []
    for seq, rws, mls in zip(seqs, rw_states, ml_states):
        y_rw, s_end, y_ml, c_end, n_end, m_end = _mixers(proj, proj_misc, y_rw, y_ml, rws, mls, rw_prm, ml_prm,
                                                         width=W, heads=H, hd=hd, d_model=D, **seq)
        rw_out.append(rw_state_out(s_end))
        ml_out.append((c_end, n_end, m_end[:, 0, :H]))

    x2 = _merge(x1, y_rw, y_ml, proj, w_br_rw[0].astype(BF16), w_br_ml[0].astype(BF16), w_out[0].astype(BF16))
    y_prompt, y_extra = _ffn((x2,), row(ffn2_norm[0]), *ffn2_w, row(final_norm),
                             split_out=(B, T), final_norm=True)

    c_rkv = 2 * D + 2 * Wm

    def seq_states(row0, batch, t_len):
        last = row0 + (jnp.arange(batch)[:, None] + 1) * t_len - (K - 1) + jnp.arange(K - 1)[None, :]
        tail = proj[last.reshape(-1)].reshape(batch, K - 1, -1)
        tail_misc = proj_misc[last.reshape(-1)].reshape(batch, K - 1, -1)
        shift = jnp.concatenate([tail[:, -1, c_rkv:c_rkv + 3 * W], tail_misc[:, -1, :dl + al + gl]], axis=-1)
        return shift, tail[:, :, 2 * D:2 * D + 2 * Wm]

    p_shift, p_conv = seq_states(0, B, T)
    s_shift, s_conv = seq_states(sample0, Bs, Ts)
    y_sample = y_extra[B * head_rows:].reshape(Bs, Ts, D)
    d1 = lambda a: a[None]
    return (y_prompt, y_sample,
            d1(p_shift), d1(rw_out[0]), d1(p_conv), d1(ml_out[0][0]), d1(ml_out[0][1]), d1(ml_out[0][2]),
            d1(s_shift), d1(rw_out[1]), d1(s_conv), d1(ml_out[1][0]), d1(ml_out[1][1]), d1(ml_out[1][2]))
```

```python
import functools

import jax
import jax.numpy as jnp
from jax import lax
from jax.experimental import pallas as pl
from jax.experimental.pallas import tpu as pltpu

F32 = jnp.float32
BF16 = jnp.bfloat16

LANE = 128
SUBLANE = 8
VMEM_LIMIT_BYTES = 56 * 1024 * 1024
CHUNK = 64
ROW_TILE = 512
PROJ_ROW_TILE = 1024
FF_TILE = 512
PROJ_COL_TILE = 1024
MERGE_ROW_TILE = 256
SEQS_PER_STEP = 2
MISC_COLS = 512
MISC_IF = 384
RMS_EPS = 1e-6
RW_GN_EPS = 64e-5
ML_LN_EPS = 1e-5
NEG = -1e30
DECAY_SCALE = 0.6065306597126334


def _dot(a, b):
    return jnp.dot(a, b, preferred_element_type=F32)


def _dot_nt(a, b):
    return lax.dot_general(a, b, (((1,), (1,)), ((), ())), preferred_element_type=F32)


def _dot_tn(a, b):
    return lax.dot_general(a, b, (((0,), (0,)), ((), ())), preferred_element_type=F32)


def _split3(x):
    h1 = x.astype(BF16)
    r1 = x - h1.astype(F32)
    h2 = r1.astype(BF16)
    r2 = r1 - h2.astype(F32)
    return h1, h2, r2.astype(BF16)


def _cumsum_rows(tri, x):
    n = x.shape[1]
    y = _dot(tri, jnp.concatenate(_split3(x), axis=1))
    return y[:, :n] + y[:, n:2 * n] + y[:, 2 * n:]


def _segsum(x, seg):
    rows = x.shape[0]
    h1, h2, _ = _split3(x)
    y = _dot(jnp.concatenate([h1, h2], axis=0), seg)
    return y[:rows] + y[rows:]


def _sigmoid(z):
    return jax.nn.sigmoid(z)


def _rmsnorm(x, g):
    return x * lax.rsqrt(jnp.mean(x * x, axis=-1, keepdims=True) + RMS_EPS) * g


def _iota(shape, dim):
    return lax.broadcasted_iota(jnp.int32, shape, dim)


def _params(*semantics):
    return pltpu.CompilerParams(dimension_semantics=semantics, vmem_limit_bytes=VMEM_LIMIT_BYTES)


def _ffn_kernel(*refs, n_f, n_main, split_in, split_out, final_norm):
    n_x = 2 if split_in else 1
    n_o = 2 if split_out else 1
    x_refs = refs[:n_x]
    g_ref, wg_ref, wu_ref, wd_ref, fg_ref = refs[n_x:n_x + 5]
    o_refs = refs[n_x + 5:n_x + 5 + n_o]
    xn_ref, acc_ref = refs[n_x + 5 + n_o:]
    i = pl.program_id(0)
    j = pl.program_id(1)

    def load_x():
        if split_in:
            return jnp.where(i < n_main, x_refs[0][...], x_refs[1][...])
        return x_refs[0][...]

    @pl.when(j == 0)
    def _():
        xn_ref[...] = _rmsnorm(load_x(), g_ref[...]).astype(BF16)
        acc_ref[...] = jnp.zeros_like(acc_ref)

    xn = xn_ref[...]
    gate = _dot(xn, wg_ref[...])
    h = (gate * _sigmoid(gate) * _dot(xn, wu_ref[...])).astype(BF16)
    acc_ref[...] += _dot(h, wd_ref[...])

    def finish(o_ref):
        y = load_x() + 0.5 * acc_ref[...]
        if final_norm:
            y = _rmsnorm(y, fg_ref[...])
        o_ref[...] = y

    last = j == n_f - 1
    if split_out:
        pl.when(last & (i < n_main))(lambda: finish(o_refs[0]))
        pl.when(last & (i >= n_main))(lambda: finish(o_refs[1]))
    else:
        pl.when(last)(lambda: finish(o_refs[0]))


def _ffn(xs, norm_g, wg, wu, wd, final_g, *, split_out, final_norm):
    split_in = len(xs) == 2
    d = xs[-1].shape[-1]
    if split_in:
        b, t = xs[0].shape[:2]
        n = b * t + xs[1].shape[0]
    else:
        n = xs[0].shape[0]
        b, t = split_out if split_out else (1, n)
    n_f = wd.shape[0] // FF_TILE
    tpb = t // ROW_TILE
    nm = b * tpb
    kern = functools.partial(_ffn_kernel, n_f=n_f, n_main=nm, split_in=split_in,
                             split_out=bool(split_out), final_norm=final_norm)
    whole = pl.BlockSpec((ROW_TILE, d), lambda i, j: (i, 0))
    main = pl.BlockSpec((None, ROW_TILE, d),
                        lambda i, j: (jnp.minimum(i, nm - 1) // tpb, jnp.minimum(i, nm - 1) % tpb, 0))
    extra = pl.BlockSpec((ROW_TILE, d), lambda i, j: (jnp.maximum(i - nm, 0), 0))
    return pl.pallas_call(
        kern,
        grid=(n // ROW_TILE, n_f),
        in_specs=([main, extra] if split_in else [whole]) + [
            pl.BlockSpec((1, d), lambda i, j: (0, 0)),
            pl.BlockSpec((d, FF_TILE), lambda i, j: (0, j)),
            pl.BlockSpec((d, FF_TILE), lambda i, j: (0, j)),
            pl.BlockSpec((FF_TILE, d), lambda i, j: (j, 0)),
            pl.BlockSpec((1, d), lambda i, j: (0, 0)),
        ],
        out_specs=[main, extra] if split_out else whole,
        out_shape=([jax.ShapeDtypeStruct((b, t, d), F32), jax.ShapeDtypeStruct((n - b * t, d), F32)]
                   if split_out else jax.ShapeDtypeStruct((n, d), F32)),
        scratch_shapes=[pltpu.VMEM((ROW_TILE, d), BF16), pltpu.VMEM((ROW_TILE, d), F32)],
        compiler_params=_params("arbitrary", "arbitrary"),
        name="ffn_final" if final_norm else "ffn",
    )(*xs, norm_g, wg, wu, wd, final_g)


def _proj_kernel(x_ref, g_ref, w_ref, wm_ref, o_ref, om_ref, xn_ref):
    @pl.when(pl.program_id(1) == 0)
    def _():
        xn_ref[...] = _rmsnorm(x_ref[...], g_ref[...]).astype(BF16)
        om_ref[...] = _dot(xn_ref[...], wm_ref[...])

    o_ref[...] = _dot(xn_ref[...], w_ref[...])


def _proj(x, norm_g, w, w_misc):
    n, d = x.shape
    cols = w.shape[1]
    tm, tn = PROJ_ROW_TILE, PROJ_COL_TILE
    return pl.pallas_call(
        _proj_kernel,
        grid=(n // tm, cols // tn),
        in_specs=[
            pl.BlockSpec((tm, d), lambda i, j: (i, 0)),
            pl.BlockSpec((1, d), lambda i, j: (0, 0)),
            pl.BlockSpec((d, tn), lambda i, j: (0, j)),
            pl.BlockSpec((d, MISC_COLS), lambda i, j: (0, 0)),
        ],
        out_specs=[pl.BlockSpec((tm, tn), lambda i, j: (i, j)),
                   pl.BlockSpec((tm, MISC_COLS), lambda i, j: (i, 0))],
        out_shape=[jax.ShapeDtypeStruct((n, cols), F32), jax.ShapeDtypeStruct((n, MISC_COLS), F32)],
        scratch_shapes=[pltpu.VMEM((tm, d), BF16)],
        compiler_params=_params("parallel", "arbitrary"),
        name="in_proj",
    )(x, norm_g, w, w_misc)


def _merge_kernel(*refs, n_seq, regions):
    x_ref = refs[0]
    yrw_refs, yml_refs = refs[1:1 + n_seq], refs[1 + n_seq:1 + 2 * n_seq]
    ga_ref, gb_ref, wrw_ref, wml_ref, wo_ref, o_ref = refs[1 + 2 * n_seq:]
    tm = x_ref.shape[0]
    row = pl.program_id(0) * tm + _iota((tm, 1), 0)

    def pick(y_refs):
        y = y_refs[0][...]
        for q in range(1, n_seq):
            in_q = functools.reduce(jnp.logical_or, [(row >= lo) & (row < hi) for lo, hi in regions[q]])
            y = jnp.where(in_q, y_refs[q][...], y)
        return y

    merged = (_sigmoid(ga_ref[...]) * _dot(pick(yrw_refs), wrw_ref[...])
              + _sigmoid(gb_ref[...]) * _dot(pick(yml_refs), wml_ref[...]))
    o_ref[...] = x_ref[...] + _dot(merged.astype(BF16), wo_ref[...])


def _merge(x, y_rw, y_ml, proj, w_rw, w_ml, w_out, regions):
    n, d = x.shape
    n_seq = len(y_rw)
    w = y_rw[0].shape[1]
    tm = MERGE_ROW_TILE
    rows = lambda width, blk: pl.BlockSpec((tm, width), lambda i: (i, blk))
    resident = lambda shape: pl.BlockSpec(shape, lambda i: (0, 0), pipeline_mode=pl.Buffered(1))
    return pl.pallas_call(
        functools.partial(_merge_kernel, n_seq=n_seq, regions=regions),
        grid=(n // tm,),
        in_specs=[rows(d, 0)] + [rows(w, 0)] * (2 * n_seq) + [rows(d, 0), rows(d, 1),
                  resident((w, d)), resident((w, d)), resident((d, d))],
        out_specs=rows(d, 0),
        out_shape=jax.ShapeDtypeStruct((n, d), F32),
        compiler_params=_params("parallel"),
        name="merge",
    )(x, *y_rw, *y_ml, proj, proj, w_rw, w_ml, w_out)


def _each(f, *lists):
    return [f(*xs) for xs in zip(*lists)]


def _rwkv_pairs(r, k, v, a, lw, g, s0, kkw, kaw, rkw, lnw, lnb, cst, result):
    L = r[0].shape[0]
    tri, seg, head0, strict, incl, col_head0, eye2, blockdiag = cst
    bf = lambda x: x.astype(BF16)

    kk_ = _each(lambda k_, w_: k_ * w_, k, kkw)
    kn = _each(lambda x: _segsum(x * x, seg), kk_)
    cum = _each(lambda x: _cumsum_rows(tri, x), lw)
    yield
    kk = _each(lambda x, n_: x / jnp.maximum(jnp.sqrt(n_), 1e-12), kk_, kn)
    k2 = _each(lambda k_, a_, w_: k_ * (1.0 + (a_ - 1.0) * w_), k, a, kaw)
    kka = _each(lambda x, a_: x * a_, kk, a)
    c_end = [c[L - 1:L, :] for c in cum]
    e_inv = [jnp.exp(-c) for c in cum]
    e_rem = _each(lambda c, ce: jnp.exp(ce - c), cum, c_end)
    at = _each(lambda x, c, l_: bf(-x * jnp.exp(c - l_)), kk, cum, lw)
    rt = _each(lambda x, c: bf(x * jnp.exp(c)), r, cum)
    bt = _each(lambda x, e: bf(x * e), kka, e_inv)
    kt = _each(lambda x, e: bf(x * e), k2, e_inv)
    vb = [bf(x) for x in v]
    zero = jnp.zeros_like(vb[0])

    def by_head(x):
        return jnp.concatenate([jnp.where(head0, x, zero), jnp.where(head0, zero, x)], axis=0)

    lhs = _each(lambda x, y_: jnp.concatenate([x, y_], axis=0), at, rt)
    ab = _each(lambda l_, x: _dot_nt(l_, by_head(x)), lhs, bt)
    ak = _each(lambda l_, x: _dot_nt(l_, by_head(x)), lhs, kt)
    yield
    a_rb = [jnp.where(incl, x[L:], 0.0) for x in ab]
    a_ak = [jnp.where(strict, x[:L], 0.0) for x in ak]
    a_rk = [jnp.where(incl, x[L:], 0.0) for x in ak]

    def blockdiag2(x):
        n_cat = jnp.where(strict, x[:L], 0.0)
        return jnp.concatenate([jnp.where(col_head0, n_cat, 0.0), jnp.where(col_head0, 0.0, n_cat)], axis=0)

    p = [blockdiag2(x) for x in ab]
    t_inv = [eye2 + x for x in p]
    sb = [bf(x) for x in s0]
    vs = [by_head(x) for x in vb]
    x0 = _each(lambda a_, s_, m_, v_: _dot_nt(a_, s_) + _dot(bf(m_), v_), at, sb, a_ak, vs)
    for _ in range(L.bit_length() - 2):
        p = [_dot(bf(x), bf(x)) for x in p]
        yield
        t_inv = _each(lambda t_, x: t_ + _dot(bf(t_), bf(x)), t_inv, p)
    yield
    us = _each(lambda t_, x: _dot(bf(t_), by_head(bf(x))), t_inv, x0)
    yield
    y = _each(lambda r_, s_, m1, m2, u_, v_: _dot_nt(r_, s_) + _dot(
        bf(jnp.concatenate([m1, m2], axis=1)), jnp.concatenate([bf(u_), v_], axis=0)),
        rt, sb, a_rb, a_rk, us, vs)
    upd = _each(lambda u_, v_, x1, x2, e: _dot_tn(
        jnp.concatenate([bf(u_[:L] + u_[L:]), v_], axis=0),
        jnp.concatenate([bf(x1 * e), bf(x2 * e)], axis=0)), us, vb, kka, k2, e_rem)
    yield
    s_new = _each(lambda s_, ce, u_: s_ * jnp.exp(ce) + jnp.where(blockdiag, u_, 0.0), s0, c_end, upd)

    hd = float(LANE // 2)
    mean = [_segsum(x, seg) / hd for x in y]
    bsum = _each(lambda r_, k_, w_: _segsum(r_ * k_ * w_, seg), r, k2, rkw)
    yield
    yc = _each(lambda x, m_: x - m_, y, mean)
    var = [_segsum(x * x, seg) / hd for x in yc]
    yield
    out = _each(lambda x, v_, w_, b_, bs, vv, g_: (x * lax.rsqrt(v_ + RW_GN_EPS) * w_ + b_ + bs * vv) * g_,
                yc, var, lnw, lnb, bsum, v, g)
    result.extend([out, s_new])


def _rwkv_steps(c, seqs, prm, *, rows, lead, width):
    (mu_rkv_ref, mu_misc_ref, w0_ref, w2_ref, a0_ref, a2_ref, g2_ref,
     kk_ref, ka_ref, rk_ref, lnw_ref, lnb_ref, tri_ref, seg_ref, eye_ref) = prm
    L = CHUNK
    row = _iota((L, 1), 0)
    first = jnp.where(c == 0, lead, 0)
    valid = row >= first
    for (rkv_ref, misc_ref, _, _, carry_rkv, carry_misc, r_s, k_s, v_s, a_s, lw_s, g_s) in seqs:
        p = rkv_ref[...]
        pm = misc_ref[:, :3 * LANE]
        if rows < L:
            p = jnp.concatenate([jnp.zeros((L - rows, p.shape[1]), F32), p], axis=0)
            pm = jnp.concatenate([jnp.zeros((L - rows, pm.shape[1]), F32), pm], axis=0)
        p = jnp.where(valid, p, 0.0)
        pm = jnp.where(valid, pm, 0.0)

        def shift_mix(cur, carry_ref, mu):
            prev = jnp.where(row == first, carry_ref[...], pltpu.roll(cur, 1, axis=0))
            carry_ref[...] = cur[L - 1:L, :]
            return jnp.where(valid, cur + mu * (prev - cur), 0.0)

        u = shift_mix(p, carry_rkv, mu_rkv_ref[...])
        um = shift_mix(pm, carry_misc, mu_misc_ref[...])

        lora = um[:, :LANE]
        wl = w0_ref[...] + _dot(jnp.tanh(lora).astype(BF16), w2_ref[...])
        lw_s[...] = jnp.where(valid, -DECAY_SCALE * _sigmoid(wl), 0.0)
        a_s[...] = _sigmoid(a0_ref[...] + _dot(lora.astype(BF16), a2_ref[...]))
        g_s[...] = _dot(_sigmoid(um[:, LANE:]).astype(BF16), g2_ref[...])
        r_s[...] = u[:, :width]
        k_s[...] = u[:, width:2 * width]
        v_s[...] = u[:, 2 * width:]
        yield

    lane = _iota((1, LANE), 1)
    half = LANE // 2
    col2 = _iota((L, 2 * L), 1)
    t_i = _iota((L, 2 * L), 0)
    s_i = col2 & (L - 1)
    cst = (
        tri_ref[...], seg_ref[...],
        lane < half,
        s_i < t_i, s_i <= t_i,
        col2 < L,
        eye_ref[...],
        (_iota((LANE, LANE), 0) < half) == (_iota((LANE, LANE), 1) < half),
    )
    sls = [slice(j * LANE, (j + 1) * LANE) for j in range(width // LANE)]
    units = [(seq, j, sl) for seq in seqs for j, sl in enumerate(sls)]
    from_seq = lambda k: [seq[k][:, sl] for seq, _, sl in units]
    from_prm = lambda ref: [ref[:, sl] for _, _, sl in units]
    result = []
    yield from _rwkv_pairs(from_seq(6), from_seq(7), from_seq(8), from_seq(9), from_seq(10), from_seq(11),
                           [seq[3][0, j] for seq, j, _ in units],
                           from_prm(kk_ref), from_prm(ka_ref), from_prm(rk_ref), from_prm(lnw_ref),
                           from_prm(lnb_ref), cst, result)
    outs, s_new = result
    for (seq, j, sl), out, s_end in zip(units, outs, s_new):
        seq[3][0, j] = s_end
        seq[2][:, sl] = out[L - rows:, :].astype(seq[2].dtype)


def _mlstm_steps(c, refs, *, rows, lead, heads, hd):
    (qk_ref, v_ref, o_ref, misc_ref, conv0_ref, cw_ref, cb_ref, ifb_ref, nw_ref, tri_ref,
     y_ref, c_ref, n_ref, m_ref, ext) = refs
    L = CHUNK
    width = heads * hd

    def chunk_rows(x):
        if rows < L:
            return jnp.concatenate([jnp.zeros((L - rows, x.shape[1]), F32), x], axis=0)
        return x

    row = _iota((L, 1), 0)
    valid = row >= jnp.where(c == 0, lead, 0)

    carried = jnp.concatenate([jnp.zeros((lead - SUBLANE, 2 * width), F32), conv0_ref[0],
                               jnp.zeros((L - lead, 2 * width), F32)], axis=0)
    use_carried = (c == 0) & (row >= lead - SUBLANE) & (row < lead)
    ext[SUBLANE:SUBLANE + L, :] = jnp.where(valid, chunk_rows(qk_ref[...]), jnp.where(use_carried, carried, 0.0))
    conv = cb_ref[...]
    for j in range(4):
        conv = conv + cw_ref[j:j + 1, :] * ext[SUBLANE - 3 + j:SUBLANE - 3 + j + L, :]
    ext[0:SUBLANE, :] = ext[L:L + SUBLANE, :]
    qk = jnp.where(valid, conv * _sigmoid(conv), 0.0)
    v = jnp.where(valid, chunk_rows(v_ref[...]), 0.0)
    og = _sigmoid(chunk_rows(o_ref[...]))
    yield

    lane = _iota((1, LANE), 1)
    is_i = lane < heads
    z = chunk_rows(misc_ref[:, MISC_IF:MISC_IF + LANE]) + ifb_ref[...]
    logsig = jnp.minimum(z, 0.0) - jnp.log1p(jnp.exp(-jnp.abs(z)))
    gates = jnp.where(valid, jnp.where(is_i, z, logsig), jnp.where(is_i, NEG, 0.0))
    cum = _cumsum_rows(tri_ref[...], jnp.where(is_i, 0.0, gates))
    pad = jnp.zeros((LANE - L, LANE), F32)
    gates_t = jnp.concatenate([gates, pad], axis=0).T
    cum_t = jnp.concatenate([cum, pad], axis=0).T
    causal = _iota((L, L), 1) <= _iota((L, L), 0)
    m_all = m_ref[0]
    yield

    for h in range(heads):
        sl = slice(h * hd, (h + 1) * hd)
        q = qk[:, sl]
        kx = qk[:, width + h * hd:width + (h + 1) * hd] * (hd ** -0.5)
        vh = v[:, sl]
        b_col = cum[:, heads + h:heads + h + 1]
        i_col = gates[:, h:h + 1]
        b_row = cum_t[heads + h:heads + h + 1, :L]
        i_row = gates_t[h:h + 1, :L]
        b_end = b_col[L - 1:L, :]
        m_prev = jnp.sum(jnp.where(lane == h, m_all, 0.0), axis=1, keepdims=True)

        log_inter = b_col + m_prev
        dmat = jnp.where(causal, b_col - b_row + i_row, NEG)
        m_q = jnp.maximum(log_inter, jnp.max(dmat, axis=-1, keepdims=True))
        w_inter = jnp.exp(log_inter - m_q)
        qb = q.astype(BF16)
        s = _dot_nt(qb, kx.astype(BF16)) * jnp.exp(dmat - m_q)
        c_h = c_ref[0, h]
        n_h = n_ref[0, h:h + 1, :]
        yield
        num = w_inter * _dot(qb, c_h.astype(BF16)) + _dot(s.astype(BF16), vh.astype(BF16))
        den = w_inter * jnp.sum(q * n_h, axis=-1, keepdims=True) + jnp.sum(s, axis=-1, keepdims=True)
        hcell = num / jnp.maximum(jnp.abs(den), jnp.exp(-m_q))
        yield

        g_col = b_end - b_col + i_col
        m_new = jnp.maximum(b_end + m_prev, jnp.max(g_col, axis=0, keepdims=True))
        a_st = jnp.exp(b_end + m_prev - m_new)
        wkk = jnp.exp(g_col - m_new) * kx
        c_ref[0, h] = a_st * c_h + _dot_tn(wkk.astype(BF16), vh.astype(BF16))
        n_ref[0, h:h + 1, :] = a_st * n_h + jnp.sum(wkk, axis=0, keepdims=True)
        m_all = jnp.where(lane == h, m_new, m_all)

        mu = jnp.mean(hcell, axis=-1, keepdims=True)
        hc = hcell - mu
        var = jnp.mean(hc * hc, axis=-1, keepdims=True)
        yh = hc * lax.rsqrt(var + ML_LN_EPS) * nw_ref[:, sl] * og[:, sl]
        y_ref[:, sl] = yh[L - rows:, :].astype(y_ref.dtype)
        yield

    m_ref[0] = m_all


N_SEQ_COLS, N_STATE, N_RW_PRM, N_ML_PRM = 6, 7, 15, 5


def _mixers_kernel(*refs, n_seq, rows, lead, width, heads, hd):
    pos = 0

    def take(n):
        nonlocal pos
        pos += n
        return refs[pos - n:pos]

    seq_cols = [take(N_SEQ_COLS) for _ in range(n_seq)]
    sh_rkv_ref, sh_misc_ref, s0_ref, conv0_ref, c0_ref, n0_ref, m0_ref = take(N_STATE)
    rw_prm, ml_prm = take(N_RW_PRM), take(N_ML_PRM)
    take(2 * n_seq)
    y_rw_refs, y_ml_refs = take(n_seq), take(n_seq)
    s_ref, c_ref, n_ref, m_ref = take(4)
    carry_rkv, carry_misc, r_s, k_s, v_s, a_s, lw_s, g_s, ext = take(9)
    c = pl.program_id(1)

    @pl.when(c == 0)
    def _():
        carry_rkv[...] = sh_rkv_ref[:, 0]
        carry_misc[...] = sh_misc_ref[:, 0]
        s_ref[...] = s0_ref[...]
        c_ref[...] = c0_ref[...]
        n_ref[...] = n0_ref[...]
        m_ref[...] = m0_ref[...]
        ext[:, 0:SUBLANE, :] = jnp.zeros((n_seq, SUBLANE, ext.shape[2]), F32)

    parts = [
        _mlstm_steps(c, tuple(seq_cols[q][2:]) + (conv0_ref.at[q],) + tuple(ml_prm)
                     + (y_ml_refs[q], c_ref.at[q], n_ref.at[q], m_ref.at[q], ext.at[q]),
                     rows=rows, lead=lead, heads=heads, hd=hd)
        for q in range(n_seq)
    ]
    parts.append(_rwkv_steps(
        c, [tuple(seq_cols[q][:2]) + (y_rw_refs[q], s_ref.at[q]) + tuple(
            ref.at[q] for ref in (carry_rkv, carry_misc, r_s, k_s, v_s, a_s, lw_s, g_s)) for q in range(n_seq)],
        rw_prm, rows=rows, lead=lead, width=width))
    while parts:
        for part in list(parts):
            if next(part, StopIteration) is StopIteration:
                parts.remove(part)


def _mixers(proj, proj_misc, y_prev, rw_state, ml_state, rw_prm, ml_prm, *,
            n_seq, batch, n_chunks, rows, seq_row, lead, width, heads, hd, d_model):
    n = proj.shape[0]
    n_pairs = width // LANE
    per = batch // n_seq
    rkv_blk = (2 * d_model + 2 * width) // (3 * width)
    qk_blk = (2 * d_model) // (2 * width)
    v_blk = (2 * d_model + 2 * width + 3 * width) // width
    kern = functools.partial(_mixers_kernel, n_seq=n_seq, rows=rows, lead=lead, width=width, heads=heads, hd=hd)
    full = lambda shape: pl.BlockSpec(shape, lambda p, c: (0,) * len(shape))
    state = lambda *shape: pl.BlockSpec((n_seq, 1) + shape, lambda p, c: (0, p) + (0,) * len(shape))

    def cols(q, w_, blk):
        return pl.BlockSpec((rows, w_), lambda p, c: (seq_row(q * per + p, c), blk))

    seq_specs = []
    for q in range(n_seq):
        seq_specs += [cols(q, 3 * width, rkv_blk), cols(q, MISC_COLS, 0), cols(q, 2 * width, qk_blk),
                      cols(q, width, v_blk), cols(q, width, v_blk + 1), cols(q, MISC_COLS, 0)]
    s_spec, c_spec, n_spec, m_spec = state(n_pairs, LANE, LANE), state(heads, hd, hd), state(heads, hd), state(1, LANE)
    state_specs = [state(1, 3 * width), state(1, 3 * LANE), s_spec, state(SUBLANE, 2 * width), c_spec, n_spec, m_spec]
    rw_specs = [
        full((1, 3 * width)), full((1, 3 * LANE)),
        full((1, width)), full((LANE, width)), full((1, width)), full((LANE, width)), full((2 * LANE, width)),
        full((1, width)), full((1, width)), full((1, width)), full((1, width)), full((1, width)),
        full((CHUNK, CHUNK)), full((LANE, LANE)), full((2 * CHUNK, 2 * CHUNK)),
    ]
    ml_specs = [full((4, 2 * width)), full((1, 2 * width)), full((1, LANE)), full((1, width)), full((CHUNK, CHUNK))]
    assert (len(state_specs), len(rw_specs), len(ml_specs)) == (N_STATE, N_RW_PRM, N_ML_PRM)
    assert lead >= SUBLANE and batch % n_seq == 0
    in_specs = seq_specs + state_specs + rw_specs + ml_specs + [pl.BlockSpec(memory_space=pl.ANY)] * (2 * n_seq)
    y_specs = [pl.BlockSpec((rows, width), (lambda p, c, q=q: (seq_row(q * per + p, c), 0))) for q in range(n_seq)]
    out_specs = y_specs + y_specs + [s_spec, c_spec, n_spec, m_spec]
    sds = jax.ShapeDtypeStruct
    st = lambda *shape: sds((n_seq, per) + shape, F32)
    out_shape = [sds((n, width), BF16)] * (2 * n_seq) + [st(n_pairs, LANE, LANE), st(heads, hd, hd),
                                                         st(heads, hd), st(1, LANE)]
    vmem = pltpu.VMEM
    scratch = [vmem((n_seq, 1, 3 * width), F32), vmem((n_seq, 1, 3 * LANE), F32)]
    scratch += [vmem((n_seq, CHUNK, width), F32) for _ in range(6)]
    scratch += [vmem((n_seq, CHUNK + SUBLANE, 2 * width), F32)]
    n_in = len(in_specs)
    seq_args = []
    for q in range(n_seq):
        seq_args += [proj, proj_misc, proj, proj, proj, proj_misc]
    shift_rkv, shift_misc, s0 = rw_state
    conv0, c0, n0, m0 = ml_state
    outs = pl.pallas_call(
        kern, grid=(per, n_chunks), in_specs=in_specs, out_specs=out_specs, out_shape=out_shape,
        scratch_shapes=scratch, input_output_aliases={n_in - 2 * n_seq + k: k for k in range(2 * n_seq)},
        compiler_params=_params("arbitrary", "arbitrary"), name="mixers",
    )(*seq_args, shift_rkv, shift_misc, s0, conv0, c0, n0, m0, *rw_prm, *ml_prm, *y_prev)
    return outs[:2 * n_seq], outs[2 * n_seq:]


def _pad_rows(w, rows):
    return jnp.pad(w, ((0, rows - w.shape[0]), (0, 0)))


def _swiglu_weights(w_gate, w_up, w_down):
    f = w_gate.shape[1]
    fp = -(-f // FF_TILE) * FF_TILE
    pad_cols = lambda w: jnp.pad(w.astype(BF16), ((0, 0), (0, fp - f)))
    return pad_cols(w_gate), pad_cols(w_up), _pad_rows(w_down.astype(BF16), fp)


def kernel(x_prompt, x_sample, state_rwkv_shift, state_rwkv_wkv, state_mlstm_conv, state_mlstm_C,
           state_mlstm_n, state_mlstm_m, meta_tokens, ffn1_norm, ffn1_w_gate, ffn1_w_up, ffn1_w_down,
           mix_norm, w_in, rw_mu, rw_w0, rw_w2, rw_a0, rw_a2, rw_g2, rw_kk, rw_ka, rw_rk, rw_ln_w, rw_ln_b,
           ml_conv_w, ml_conv_b, ml_i_b, ml_f_b, ml_norm_w, w_br_rw, w_br_ml, w_out,
           ffn2_norm, ffn2_w_gate, ffn2_w_up, ffn2_w_down, final_norm):
    assert ffn1_norm.shape[0] == 1, "single-layer trunk"
    B, T, D = x_prompt.shape
    Bs, Ts, _ = x_sample.shape
    n_meta = meta_tokens.shape[0]
    W = rw_w0.shape[-1]
    dl, al, gl = rw_w2.shape[1], rw_a2.shape[1], rw_g2.shape[1]
    rw_heads, rw_hd = rw_rk.shape[1], rw_rk.shape[2]
    Wm = ml_norm_w.shape[-1]
    H = ml_i_b.shape[-1]
    hd = Wm // H
    K = ml_conv_w.shape[1]
    L = CHUNK
    assert rw_hd == LANE // 2 and dl == LANE // 2 and al == LANE // 2 and gl <= 2 * LANE
    assert K == 4 and hd % LANE == 0 and 2 * H <= LANE and W == Wm and 2 * W == D
    assert Ts <= L and Ts % (2 * SUBLANE) == 0 and (11 * D // 2) % PROJ_COL_TILE == 0

    lead = (-n_meta) % L
    head_rows = lead + n_meta
    n_head = head_rows // L
    n_chunks = n_head + T // L
    lead_s = L - Ts
    n_main = B * T
    sample0 = n_main + B * head_rows
    N = sample0 + Bs * Ts
    assert T % L == 0 and T % ROW_TILE == 0 and N % PROJ_ROW_TILE == 0
    head = jnp.concatenate([jnp.zeros((lead, D), F32), meta_tokens.astype(F32)], axis=0)
    x_extra = jnp.concatenate([jnp.broadcast_to(head[None], (B, head_rows, D)).reshape(B * head_rows, D),
                               x_sample.reshape(Bs * Ts, D)], axis=0)

    ffn1_w = _swiglu_weights(ffn1_w_gate[0], ffn1_w_up[0], ffn1_w_down[0])
    ffn2_w = _swiglu_weights(ffn2_w_gate[0], ffn2_w_up[0], ffn2_w_down[0])
    wi = w_in[0]
    o_lora = 3 * W
    o_qk = o_lora + dl + al + gl
    o_v = o_qk + 2 * Wm
    o_o = o_v + Wm
    o_i = o_o + Wm
    o_gate = o_i + 2 * H
    zc = lambda n: jnp.zeros((D, n), wi.dtype)
    misc_w = jnp.concatenate([wi[:, o_lora:o_qk], zc(MISC_IF - (dl + al + gl)),
                              wi[:, o_i:o_gate], zc(MISC_COLS - MISC_IF - 2 * H)], axis=1)
    w_proj = jnp.concatenate([wi[:, o_gate:], wi[:, o_qk:o_v], wi[:, :o_lora], wi[:, o_v:o_i]],
                             axis=1).astype(BF16)

    def misc_vec(v_lora, fill=0.0):
        return jnp.pad(v_lora, [(0, 0)] * (v_lora.ndim - 1) + [(0, 3 * LANE - v_lora.shape[-1])],
                       constant_values=fill)

    row = lambda v: v.reshape(1, -1).astype(F32)
    rw_prm = (
        row(rw_mu[0, :o_lora]), misc_vec(row(rw_mu[0, o_lora:])),
        row(rw_w0[0]), _pad_rows(rw_w2[0], LANE).astype(BF16),
        row(rw_a0[0]), jnp.concatenate([jnp.zeros((dl, W), F32), rw_a2[0]], axis=0).astype(BF16),
        _pad_rows(rw_g2[0], 2 * LANE).astype(BF16),
        row(rw_kk[0]), row(rw_ka[0]), row(rw_rk[0]), row(rw_ln_w[0]), row(rw_ln_b[0]),
    )
    ifb = jnp.pad(jnp.concatenate([ml_i_b[0], ml_f_b[0]]).reshape(1, 2 * H), ((0, 0), (0, LANE - 2 * H)))
    tri = jnp.tril(jnp.ones((L, L), BF16))
    same_head = jnp.kron(jnp.eye(LANE // rw_hd, dtype=BF16), jnp.ones((rw_hd, rw_hd), BF16))
    rw_prm = rw_prm + (tri, same_head, jnp.eye(2 * L, dtype=F32))
    ml_prm = (ml_conv_w[0].astype(F32), row(ml_conv_b[0]), ifb.astype(F32), row(ml_norm_w[0]), tri)

    def rw_state_in(shift, wkv):
        b = shift.shape[0]
        s = wkv.reshape(b, rw_heads // 2, 2, rw_hd, rw_hd)
        z = jnp.zeros_like(s[:, :, 0])
        bd = jnp.concatenate([jnp.concatenate([s[:, :, 0], z], axis=-1),
                              jnp.concatenate([z, s[:, :, 1]], axis=-1)], axis=-2)
        return shift[:, None, :o_lora], misc_vec(shift[:, None, o_lora:]), bd

    def rw_state_out(bd):
        h = rw_hd
        return jnp.stack([bd[:, :, :h, :h], bd[:, :, h:, h:]], axis=2).reshape(bd.shape[0], rw_heads, h, h)

    def conv_in(buf):
        return jnp.pad(buf, ((0, 0), (SUBLANE - (K - 1), 0), (0, 0)))

    def m_in(m):
        return jnp.pad(m, ((0, 0), (0, LANE - H)))[:, None, :]

    x1 = _ffn((x_prompt, x_extra), row(ffn1_norm[0]), *ffn1_w, row(final_norm),
              split_out=None, final_norm=False)
    proj, proj_misc = _proj(x1, row(mix_norm[0]), w_proj, misc_w.astype(BF16))

    zeros = lambda *s: jnp.zeros(s, F32)
    main_chunks = T // L

    def prompt_row(b, c):
        return jnp.where(c < n_head, n_main // L + b * n_head + c, b * main_chunks + c - n_head)

    S = SEQS_PER_STEP
    assert B % S == 0 and Bs % S == 0
    seqs = (
        dict(batch=B, n_chunks=n_chunks, rows=L, seq_row=prompt_row, lead=lead),
        dict(batch=Bs, n_chunks=1, rows=Ts, seq_row=lambda b, c: sample0 // Ts + b, lead=lead_s),
    )
    grouped = lambda a: a.reshape((S, a.shape[0] // S) + a.shape[1:])
    flat = lambda a: a.reshape((a.shape[0] * a.shape[1],) + a.shape[2:])
    rw_states = (
        rw_state_in(zeros(B, o_qk), zeros(B, rw_heads, rw_hd, rw_hd)),
        rw_state_in(state_rwkv_shift[0], state_rwkv_wkv[0]),
    )
    ml_states = (
        (conv_in(zeros(B, K - 1, 2 * Wm)), zeros(B, H, hd, hd), zeros(B, H, hd), m_in(zeros(B, H))),
        (conv_in(state_mlstm_conv[0]), state_mlstm_C[0], state_mlstm_n[0], m_in(state_mlstm_m[0])),
    )
    ys = [jnp.zeros((N, W), BF16)] * (2 * S)
    rw_out, ml_out = [], []
    for seq, rws, mls in zip(seqs, rw_states, ml_states):
        ys, (s_end, c_end, n_end, m_end) = _mixers(
            proj, proj_misc, ys, tuple(map(grouped, rws)), tuple(map(grouped, mls)), rw_prm, ml_prm,
            n_seq=S, width=W, heads=H, hd=hd, d_model=D, **seq)
        rw_out.append(rw_state_out(flat(s_end)))
        ml_out.append((flat(c_end), flat(n_end), flat(m_end)[:, 0, :H]))

    half_rows = lambda lo, per_seq, count: [(lo + q * (count // S) * per_seq, lo + (q + 1) * (count // S) * per_seq)
                                            for q in range(S)]
    regions = list(zip(half_rows(0, T, B), half_rows(n_main, head_rows, B), half_rows(sample0, Ts, Bs)))
    x2 = _merge(x1, ys[:S], ys[S:], proj, w_br_rw[0].astype(BF16), w_br_ml[0].astype(BF16),
                w_out[0].astype(BF16), regions)
    y_prompt, y_extra = _ffn((x2,), row(ffn2_norm[0]), *ffn2_w, row(final_norm),
                             split_out=(B, T), final_norm=True)

    c_rkv = 2 * D + 2 * Wm

    def seq_states(row0, batch, t_len):
        last = row0 + (jnp.arange(batch)[:, None] + 1) * t_len - (K - 1) + jnp.arange(K - 1)[None, :]
        tail = proj[last.reshape(-1)].reshape(batch, K - 1, -1)
        tail_misc = proj_misc[last.reshape(-1)].reshape(batch, K - 1, -1)
        shift = jnp.concatenate([tail[:, -1, c_rkv:c_rkv + 3 * W], tail_misc[:, -1, :dl + al + gl]], axis=-1)
        return shift, tail[:, :, 2 * D:2 * D + 2 * Wm]

    p_shift, p_conv = seq_states(0, B, T)
    s_shift, s_conv = seq_states(sample0, Bs, Ts)
    y_sample = y_extra[B * head_rows:].reshape(Bs, Ts, D)
    d1 = lambda a: a[None]
    return (y_prompt, y_sample,
            d1(p_shift), d1(rw_out[0]), d1(p_conv), d1(ml_out[0][0]), d1(ml_out[0][1]), d1(ml_out[0][2]),
            d1(s_shift), d1(rw_out[1]), d1(s_conv), d1(ml_out[1][0]), d1(ml_out[1][1]), d1(ml_out[1][2]))
```

```python
import functools

import jax
import jax.numpy as jnp
from jax import lax
from jax.experimental import pallas as pl
from jax.experimental.pallas import tpu as pltpu

F32 = jnp.float32
BF16 = jnp.bfloat16

LANE = 128
SUBLANE = 8
VMEM_LIMIT_BYTES = 56 * 1024 * 1024
CHUNK = 64
ROW_TILE = 512
PROJ_ROW_TILE = 1024
FF_TILE = 512
PROJ_COL_TILE = 1024
MERGE_ROW_TILE = 256
SEQS_PER_STEP = 2
MISC_COLS = 512
MISC_IF = 384
RMS_EPS = 1e-6
RW_GN_EPS = 64e-5
ML_LN_EPS = 1e-5
NEG = -1e30
DECAY_SCALE = 0.6065306597126334


def _dot(a, b):
    return jnp.dot(a, b, preferred_element_type=F32)


def _dot_nt(a, b):
    return lax.dot_general(a, b, (((1,), (1,)), ((), ())), preferred_element_type=F32)


def _dot_tn(a, b):
    return lax.dot_general(a, b, (((0,), (0,)), ((), ())), preferred_element_type=F32)


def _split3(x):
    h1 = x.astype(BF16)
    r1 = x - h1.astype(F32)
    h2 = r1.astype(BF16)
    r2 = r1 - h2.astype(F32)
    return h1, h2, r2.astype(BF16)


def _cumsum_rows(tri, x):
    n = x.shape[1]
    y = _dot(tri, jnp.concatenate(_split3(x), axis=1))
    return y[:, :n] + y[:, n:2 * n] + y[:, 2 * n:]


def _segsum(x, seg):
    rows = x.shape[0]
    h1, h2, _ = _split3(x)
    y = _dot(jnp.concatenate([h1, h2], axis=0), seg)
    return y[:rows] + y[rows:]


def _sigmoid(z):
    return jax.nn.sigmoid(z)


def _rmsnorm(x, g):
    return x * lax.rsqrt(jnp.mean(x * x, axis=-1, keepdims=True) + RMS_EPS) * g


def _iota(shape, dim):
    return lax.broadcasted_iota(jnp.int32, shape, dim)


def _params(*semantics):
    return pltpu.CompilerParams(dimension_semantics=semantics, vmem_limit_bytes=VMEM_LIMIT_BYTES)


def _ffn_kernel(*refs, n_f, tiles, split_in, split_out, final_norm):
    n_x = 2 if split_in else 1
    n_o = 2 if split_out else 1
    x_refs = refs[:n_x]
    g_ref, wg_ref, wu_ref, wd_ref, fg_ref = refs[n_x:n_x + 5]
    o_refs = refs[n_x + 5:n_x + 5 + n_o]
    xn_ref, acc_ref = refs[n_x + 5 + n_o:]
    in_main = pl.program_id(0) % tiles[0] < tiles[1]
    j = pl.program_id(1)

    def load_x():
        if split_in:
            return jnp.where(in_main, x_refs[0][...], x_refs[1][...])
        return x_refs[0][...]

    @pl.when(j == 0)
    def _():
        xn_ref[...] = _rmsnorm(load_x(), g_ref[...]).astype(BF16)
        acc_ref[...] = jnp.zeros_like(acc_ref)

    xn = xn_ref[...]
    gate = _dot(xn, wg_ref[...])
    h = (gate * _sigmoid(gate) * _dot(xn, wu_ref[...])).astype(BF16)
    acc_ref[...] += _dot(h, wd_ref[...])

    def finish(o_ref):
        y = load_x() + 0.5 * acc_ref[...]
        if final_norm:
            y = _rmsnorm(y, fg_ref[...])
        o_ref[...] = y

    last = j == n_f - 1
    if split_out:
        pl.when(last & in_main)(lambda: finish(o_refs[0]))
        pl.when(last & jnp.logical_not(in_main))(lambda: finish(o_refs[1]))
    else:
        pl.when(last)(lambda: finish(o_refs[0]))


def _ffn(xs, norm_g, wg, wu, wd, final_g, *, groups, split_out, final_norm):
    split_in = len(xs) == 2
    d = xs[-1].shape[-1]
    if split_in:
        b, t = xs[0].shape[:2]
        n = b * t + xs[1].shape[0]
    else:
        n = xs[0].shape[0]
        b, t = split_out if split_out else (groups, n // groups)
    n_f = wd.shape[0] // FF_TILE
    tpb = t // ROW_TILE
    mg = (b // groups) * tpb
    tg = n // ROW_TILE // groups
    eg = tg - mg
    kern = functools.partial(_ffn_kernel, n_f=n_f, tiles=(tg, mg), split_in=split_in,
                             split_out=bool(split_out), final_norm=final_norm)
    whole = pl.BlockSpec((ROW_TILE, d), lambda i, j: (i, 0))

    def main_index(i, j):
        m = (i // tg) * mg + jnp.minimum(i % tg, mg - 1)
        return m // tpb, m % tpb, 0

    def extra_index(i, j):
        return (i // tg) * eg + jnp.maximum(i % tg - mg, 0), 0

    main = pl.BlockSpec((None, ROW_TILE, d), main_index)
    extra = pl.BlockSpec((ROW_TILE, d), extra_index)
    return pl.pallas_call(
        kern,
        grid=(n // ROW_TILE, n_f),
        in_specs=([main, extra] if split_in else [whole]) + [
            pl.BlockSpec((1, d), lambda i, j: (0, 0)),
            pl.BlockSpec((d, FF_TILE), lambda i, j: (0, j)),
            pl.BlockSpec((d, FF_TILE), lambda i, j: (0, j)),
            pl.BlockSpec((FF_TILE, d), lambda i, j: (j, 0)),
            pl.BlockSpec((1, d), lambda i, j: (0, 0)),
        ],
        out_specs=[main, extra] if split_out else whole,
        out_shape=([jax.ShapeDtypeStruct((b, t, d), F32), jax.ShapeDtypeStruct((n - b * t, d), F32)]
                   if split_out else jax.ShapeDtypeStruct((n, d), F32)),
        scratch_shapes=[pltpu.VMEM((ROW_TILE, d), BF16), pltpu.VMEM((ROW_TILE, d), F32)],
        compiler_params=_params("arbitrary", "arbitrary"),
        name="ffn_final" if final_norm else "ffn",
    )(*xs, norm_g, wg, wu, wd, final_g)


def _proj_kernel(x_ref, g_ref, w_ref, wm_ref, o_ref, om_ref, xn_ref):
    @pl.when(pl.program_id(1) == 0)
    def _():
        xn_ref[...] = _rmsnorm(x_ref[...], g_ref[...]).astype(BF16)
        om_ref[...] = _dot(xn_ref[...], wm_ref[...])

    o_ref[...] = _dot(xn_ref[...], w_ref[...])


def _proj(x, norm_g, w, w_misc):
    n, d = x.shape
    cols = w.shape[1]
    tm, tn = PROJ_ROW_TILE, PROJ_COL_TILE
    return pl.pallas_call(
        _proj_kernel,
        grid=(n // tm, cols // tn),
        in_specs=[
            pl.BlockSpec((tm, d), lambda i, j: (i, 0)),
            pl.BlockSpec((1, d), lambda i, j: (0, 0)),
            pl.BlockSpec((d, tn), lambda i, j: (0, j)),
            pl.BlockSpec((d, MISC_COLS), lambda i, j: (0, 0)),
        ],
        out_specs=[pl.BlockSpec((tm, tn), lambda i, j: (i, j)),
                   pl.BlockSpec((tm, MISC_COLS), lambda i, j: (i, 0))],
        out_shape=[jax.ShapeDtypeStruct((n, cols), F32), jax.ShapeDtypeStruct((n, MISC_COLS), F32)],
        scratch_shapes=[pltpu.VMEM((tm, d), BF16)],
        compiler_params=_params("parallel", "arbitrary"),
        name="in_proj",
    )(x, norm_g, w, w_misc)


def _merge_kernel(x_ref, yrw_ref, yml_ref, ga_ref, gb_ref, wrw_ref, wml_ref, wo_ref, o_ref):
    merged = (_sigmoid(ga_ref[...]) * _dot(yrw_ref[...], wrw_ref[...])
              + _sigmoid(gb_ref[...]) * _dot(yml_ref[...], wml_ref[...]))
    o_ref[...] = x_ref[...] + _dot(merged.astype(BF16), wo_ref[...])


def _merge(x, y_rw, y_ml, proj, w_rw, w_ml, w_out):
    n, d = x.shape
    w = y_rw.shape[1]
    tm = MERGE_ROW_TILE
    rows = lambda width, blk: pl.BlockSpec((tm, width), lambda i: (i, blk))
    resident = lambda shape: pl.BlockSpec(shape, lambda i: (0, 0), pipeline_mode=pl.Buffered(1))
    return pl.pallas_call(
        _merge_kernel,
        grid=(n // tm,),
        in_specs=[rows(d, 0), rows(w, 0), rows(w, 0), rows(d, 0), rows(d, 1),
                  resident((w, d)), resident((w, d)), resident((d, d))],
        out_specs=rows(d, 0),
        out_shape=jax.ShapeDtypeStruct((n, d), F32),
        compiler_params=_params("parallel"),
        name="merge",
    )(x, y_rw, y_ml, proj, proj, w_rw, w_ml, w_out)


def _each(f, *lists):
    return [f(*xs) for xs in zip(*lists)]


def _rwkv_pairs(r, k, v, a, lw, g, s0, kkw, kaw, rkw, lnw, lnb, cst, result):
    L = r[0].shape[0]
    tri, seg, head0, strict, incl, col_head0, eye2, blockdiag = cst
    bf = lambda x: x.astype(BF16)

    kk_ = _each(lambda k_, w_: k_ * w_, k, kkw)
    kn = _each(lambda x: _segsum(x * x, seg), kk_)
    cum = _each(lambda x: _cumsum_rows(tri, x), lw)
    yield
    kk = _each(lambda x, n_: x / jnp.maximum(jnp.sqrt(n_), 1e-12), kk_, kn)
    k2 = _each(lambda k_, a_, w_: k_ * (1.0 + (a_ - 1.0) * w_), k, a, kaw)
    kka = _each(lambda x, a_: x * a_, kk, a)
    c_end = [c[L - 1:L, :] for c in cum]
    e_inv = [jnp.exp(-c) for c in cum]
    e_rem = _each(lambda c, ce: jnp.exp(ce - c), cum, c_end)
    at = _each(lambda x, c, l_: bf(-x * jnp.exp(c - l_)), kk, cum, lw)
    rt = _each(lambda x, c: bf(x * jnp.exp(c)), r, cum)
    bt = _each(lambda x, e: bf(x * e), kka, e_inv)
    kt = _each(lambda x, e: bf(x * e), k2, e_inv)
    vb = [bf(x) for x in v]
    zero = jnp.zeros_like(vb[0])

    def by_head(x):
        return jnp.concatenate([jnp.where(head0, x, zero), jnp.where(head0, zero, x)], axis=0)

    lhs = _each(lambda x, y_: jnp.concatenate([x, y_], axis=0), at, rt)
    ab = _each(lambda l_, x: _dot_nt(l_, by_head(x)), lhs, bt)
    ak = _each(lambda l_, x: _dot_nt(l_, by_head(x)), lhs, kt)
    yield
    a_rb = [jnp.where(incl, x[L:], 0.0) for x in ab]
    a_ak = [jnp.where(strict, x[:L], 0.0) for x in ak]
    a_rk = [jnp.where(incl, x[L:], 0.0) for x in ak]

    def blockdiag2(x):
        n_cat = jnp.where(strict, x[:L], 0.0)
        return jnp.concatenate([jnp.where(col_head0, n_cat, 0.0), jnp.where(col_head0, 0.0, n_cat)], axis=0)

    p = [blockdiag2(x) for x in ab]
    t_inv = [eye2 + x for x in p]
    sb = [bf(x) for x in s0]
    vs = [by_head(x) for x in vb]
    x0 = _each(lambda a_, s_, m_, v_: _dot_nt(a_, s_) + _dot(bf(m_), v_), at, sb, a_ak, vs)
    for _ in range(L.bit_length() - 2):
        p = [_dot(bf(x), bf(x)) for x in p]
        yield
        t_inv = _each(lambda t_, x: t_ + _dot(bf(t_), bf(x)), t_inv, p)
    yield
    us = _each(lambda t_, x: _dot(bf(t_), by_head(bf(x))), t_inv, x0)
    yield
    y = _each(lambda r_, s_, m1, m2, u_, v_: _dot_nt(r_, s_) + _dot(
        bf(jnp.concatenate([m1, m2], axis=1)), jnp.concatenate([bf(u_), v_], axis=0)),
        rt, sb, a_rb, a_rk, us, vs)
    upd = _each(lambda u_, v_, x1, x2, e: _dot_tn(
        jnp.concatenate([bf(u_[:L] + u_[L:]), v_], axis=0),
        jnp.concatenate([bf(x1 * e), bf(x2 * e)], axis=0)), us, vb, kka, k2, e_rem)
    yield
    s_new = _each(lambda s_, ce, u_: s_ * jnp.exp(ce) + jnp.where(blockdiag, u_, 0.0), s0, c_end, upd)

    hd = float(LANE // 2)
    mean = [_segsum(x, seg) / hd for x in y]
    bsum = _each(lambda r_, k_, w_: _segsum(r_ * k_ * w_, seg), r, k2, rkw)
    yield
    yc = _each(lambda x, m_: x - m_, y, mean)
    var = [_segsum(x * x, seg) / hd for x in yc]
    yield
    out = _each(lambda x, v_, w_, b_, bs, vv, g_: (x * lax.rsqrt(v_ + RW_GN_EPS) * w_ + b_ + bs * vv) * g_,
                yc, var, lnw, lnb, bsum, v, g)
    result.extend([out, s_new])


def _rwkv_steps(c, seqs, prm, *, rows, lead, width):
    (mu_rkv_ref, mu_misc_ref, w0_ref, w2_ref, a0_ref, a2_ref, g2_ref,
     kk_ref, ka_ref, rk_ref, lnw_ref, lnb_ref, tri_ref, seg_ref, eye_ref) = prm
    L = CHUNK
    row = _iota((L, 1), 0)
    first = jnp.where(c == 0, lead, 0)
    valid = row >= first
    for (rkv_ref, misc_ref, _, _, carry_rkv, carry_misc, r_s, k_s, v_s, a_s, lw_s, g_s) in seqs:
        p = rkv_ref[...]
        pm = misc_ref[:, :3 * LANE]
        if rows < L:
            p = jnp.concatenate([jnp.zeros((L - rows, p.shape[1]), F32), p], axis=0)
            pm = jnp.concatenate([jnp.zeros((L - rows, pm.shape[1]), F32), pm], axis=0)
        p = jnp.where(valid, p, 0.0)
        pm = jnp.where(valid, pm, 0.0)

        def shift_mix(cur, carry_ref, mu):
            prev = jnp.where(row == first, carry_ref[...], pltpu.roll(cur, 1, axis=0))
            carry_ref[...] = cur[L - 1:L, :]
            return jnp.where(valid, cur + mu * (prev - cur), 0.0)

        u = shift_mix(p, carry_rkv, mu_rkv_ref[...])
        um = shift_mix(pm, carry_misc, mu_misc_ref[...])

        lora = um[:, :LANE]
        wl = w0_ref[...] + _dot(jnp.tanh(lora).astype(BF16), w2_ref[...])
        lw_s[...] = jnp.where(valid, -DECAY_SCALE * _sigmoid(wl), 0.0)
        a_s[...] = _sigmoid(a0_ref[...] + _dot(lora.astype(BF16), a2_ref[...]))
        g_s[...] = _dot(_sigmoid(um[:, LANE:]).astype(BF16), g2_ref[...])
        r_s[...] = u[:, :width]
        k_s[...] = u[:, width:2 * width]
        v_s[...] = u[:, 2 * width:]
        yield

    lane = _iota((1, LANE), 1)
    half = LANE // 2
    col2 = _iota((L, 2 * L), 1)
    t_i = _iota((L, 2 * L), 0)
    s_i = col2 & (L - 1)
    cst = (
        tri_ref[...], seg_ref[...],
        lane < half,
        s_i < t_i, s_i <= t_i,
        col2 < L,
        eye_ref[...],
        (_iota((LANE, LANE), 0) < half) == (_iota((LANE, LANE), 1) < half),
    )
    sls = [slice(j * LANE, (j + 1) * LANE) for j in range(width // LANE)]
    units = [(seq, j, sl) for seq in seqs for j, sl in enumerate(sls)]
    from_seq = lambda k: [seq[k][:, sl] for seq, _, sl in units]
    from_prm = lambda ref: [ref[:, sl] for _, _, sl in units]
    result = []
    yield from _rwkv_pairs(from_seq(6), from_seq(7), from_seq(8), from_seq(9), from_seq(10), from_seq(11),
                           [seq[3][0, j] for seq, j, _ in units],
                           from_prm(kk_ref), from_prm(ka_ref), from_prm(rk_ref), from_prm(lnw_ref),
                           from_prm(lnb_ref), cst, result)
    outs, s_new = result
    for (seq, j, sl), out, s_end in zip(units, outs, s_new):
        seq[3][0, j] = s_end
        seq[2][:, sl] = out[L - rows:, :].astype(seq[2].dtype)


def _mlstm_steps(c, refs, *, rows, lead, heads, hd):
    (qk_ref, v_ref, o_ref, misc_ref, conv0_ref, cw_ref, cb_ref, ifb_ref, nw_ref, tri_ref,
     y_ref, c_ref, n_ref, m_ref, ext) = refs
    L = CHUNK
    width = heads * hd

    def chunk_rows(x):
        if rows < L:
            return jnp.concatenate([jnp.zeros((L - rows, x.shape[1]), F32), x], axis=0)
        return x

    row = _iota((L, 1), 0)
    valid = row >= jnp.where(c == 0, lead, 0)

    carried = jnp.concatenate([jnp.zeros((lead - SUBLANE, 2 * width), F32), conv0_ref[0],
                               jnp.zeros((L - lead, 2 * width), F32)], axis=0)
    use_carried = (c == 0) & (row >= lead - SUBLANE) & (row < lead)
    ext[SUBLANE:SUBLANE + L, :] = jnp.where(valid, chunk_rows(qk_ref[...]), jnp.where(use_carried, carried, 0.0))
    conv = cb_ref[...]
    for j in range(4):
        conv = conv + cw_ref[j:j + 1, :] * ext[SUBLANE - 3 + j:SUBLANE - 3 + j + L, :]
    ext[0:SUBLANE, :] = ext[L:L + SUBLANE, :]
    qk = jnp.where(valid, conv * _sigmoid(conv), 0.0)
    v = jnp.where(valid, chunk_rows(v_ref[...]), 0.0)
    og = _sigmoid(chunk_rows(o_ref[...]))
    yield

    lane = _iota((1, LANE), 1)
    is_i = lane < heads
    z = chunk_rows(misc_ref[:, MISC_IF:MISC_IF + LANE]) + ifb_ref[...]
    logsig = jnp.minimum(z, 0.0) - jnp.log1p(jnp.exp(-jnp.abs(z)))
    gates = jnp.where(valid, jnp.where(is_i, z, logsig), jnp.where(is_i, NEG, 0.0))
    cum = _cumsum_rows(tri_ref[...], jnp.where(is_i, 0.0, gates))
    pad = jnp.zeros((LANE - L, LANE), F32)
    gates_t = jnp.concatenate([gates, pad], axis=0).T
    cum_t = jnp.concatenate([cum, pad], axis=0).T
    causal = _iota((L, L), 1) <= _iota((L, L), 0)
    m_all = m_ref[0]
    yield

    for h in range(heads):
        sl = slice(h * hd, (h + 1) * hd)
        q = qk[:, sl]
        kx = qk[:, width + h * hd:width + (h + 1) * hd] * (hd ** -0.5)
        vh = v[:, sl]
        b_col = cum[:, heads + h:heads + h + 1]
        i_col = gates[:, h:h + 1]
        b_row = cum_t[heads + h:heads + h + 1, :L]
        i_row = gates_t[h:h + 1, :L]
        b_end = b_col[L - 1:L, :]
        m_prev = jnp.sum(jnp.where(lane == h, m_all, 0.0), axis=1, keepdims=True)

        log_inter = b_col + m_prev
        dmat = jnp.where(causal, b_col - b_row + i_row, NEG)
        m_q = jnp.maximum(log_inter, jnp.max(dmat, axis=-1, keepdims=True))
        w_inter = jnp.exp(log_inter - m_q)
        qb = q.astype(BF16)
        s = _dot_nt(qb, kx.astype(BF16)) * jnp.exp(dmat - m_q)
        c_h = c_ref[0, h]
        n_h = n_ref[0, h:h + 1, :]
        yield
        num = w_inter * _dot(qb, c_h.astype(BF16)) + _dot(s.astype(BF16), vh.astype(BF16))
        den = w_inter * jnp.sum(q * n_h, axis=-1, keepdims=True) + jnp.sum(s, axis=-1, keepdims=True)
        hcell = num / jnp.maximum(jnp.abs(den), jnp.exp(-m_q))
        yield

        g_col = b_end - b_col + i_col
        m_new = jnp.maximum(b_end + m_prev, jnp.max(g_col, axis=0, keepdims=True))
        a_st = jnp.exp(b_end + m_prev - m_new)
        wkk = jnp.exp(g_col - m_new) * kx
        c_ref[0, h] = a_st * c_h + _dot_tn(wkk.astype(BF16), vh.astype(BF16))
        n_ref[0, h:h + 1, :] = a_st * n_h + jnp.sum(wkk, axis=0, keepdims=True)
        m_all = jnp.where(lane == h, m_new, m_all)

        mu = jnp.mean(hcell, axis=-1, keepdims=True)
        hc = hcell - mu
        var = jnp.mean(hc * hc, axis=-1, keepdims=True)
        yh = hc * lax.rsqrt(var + ML_LN_EPS) * nw_ref[:, sl] * og[:, sl]
        y_ref[:, sl] = yh[L - rows:, :].astype(y_ref.dtype)
        yield

    m_ref[0] = m_all


N_COLS, N_STATE, N_RW_PRM, N_ML_PRM = 6, 7, 15, 5


def _mixers_kernel(*refs, n_seq, aliased, rows, lead, width, heads, hd):
    pos = 0

    def take(n):
        nonlocal pos
        pos += n
        return refs[pos - n:pos]

    rkv_ref, misc_ref, qk_ref, v_ref, o_ref, misc2_ref = take(N_COLS)
    sh_rkv_ref, sh_misc_ref, s0_ref, conv0_ref, c0_ref, n0_ref, m0_ref = take(N_STATE)
    rw_prm, ml_prm = take(N_RW_PRM), take(N_ML_PRM)
    take(2 if aliased else 0)
    y_rw_ref, y_ml_ref, s_ref, c_ref, n_ref, m_ref = take(6)
    carry_rkv, carry_misc, r_s, k_s, v_s, a_s, lw_s, g_s, ext = take(9)
    c = pl.program_id(1)

    @pl.when(c == 0)
    def _():
        carry_rkv[...] = sh_rkv_ref[:, 0]
        carry_misc[...] = sh_misc_ref[:, 0]
        s_ref[...] = s0_ref[...]
        c_ref[...] = c0_ref[...]
        n_ref[...] = n0_ref[...]
        m_ref[...] = m0_ref[...]
        ext[:, 0:SUBLANE, :] = jnp.zeros((n_seq, SUBLANE, ext.shape[2]), F32)

    parts = [
        _mlstm_steps(c, tuple(ref.at[q] for ref in (qk_ref, v_ref, o_ref, misc2_ref, conv0_ref)) + tuple(ml_prm)
                     + tuple(ref.at[q] for ref in (y_ml_ref, c_ref, n_ref, m_ref, ext)),
                     rows=rows, lead=lead, heads=heads, hd=hd)
        for q in range(n_seq)
    ]
    parts.append(_rwkv_steps(
        c, [tuple(ref.at[q] for ref in (rkv_ref, misc_ref, y_rw_ref, s_ref, carry_rkv, carry_misc,
                                        r_s, k_s, v_s, a_s, lw_s, g_s)) for q in range(n_seq)],
        rw_prm, rows=rows, lead=lead, width=width))
    while parts:
        for part in list(parts):
            if next(part, StopIteration) is StopIteration:
                parts.remove(part)


def _mixers(proj, proj_misc, y_prev, rw_state, ml_state, rw_prm, ml_prm, *,
            n_seq, batch, n_chunks, rows, seq_row, lead, width, heads, hd, d_model):
    g_rows = proj.shape[1]
    n_pairs = width // LANE
    per = batch // n_seq
    rkv_blk = (2 * d_model + 2 * width) // (3 * width)
    qk_blk = (2 * d_model) // (2 * width)
    v_blk = (2 * d_model + 2 * width + 3 * width) // width
    aliased = y_prev is not None
    kern = functools.partial(_mixers_kernel, n_seq=n_seq, aliased=aliased, rows=rows, lead=lead, width=width,
                             heads=heads, hd=hd)
    full = lambda shape: pl.BlockSpec(shape, lambda p, c: (0,) * len(shape))
    state = lambda *shape: pl.BlockSpec((n_seq, 1) + shape, lambda p, c: (0, p) + (0,) * len(shape))
    cols = lambda w_, blk: pl.BlockSpec((n_seq, rows, w_), lambda p, c: (0, seq_row(p, c), blk))
    col_specs = [cols(3 * width, rkv_blk), cols(MISC_COLS, 0), cols(2 * width, qk_blk),
                 cols(width, v_blk), cols(width, v_blk + 1), cols(MISC_COLS, 0)]
    s_spec, c_spec, n_spec, m_spec = state(n_pairs, LANE, LANE), state(heads, hd, hd), state(heads, hd), state(1, LANE)
    state_specs = [state(1, 3 * width), state(1, 3 * LANE), s_spec, state(SUBLANE, 2 * width), c_spec, n_spec, m_spec]
    rw_specs = [
        full((1, 3 * width)), full((1, 3 * LANE)),
        full((1, width)), full((LANE, width)), full((1, width)), full((LANE, width)), full((2 * LANE, width)),
        full((1, width)), full((1, width)), full((1, width)), full((1, width)), full((1, width)),
        full((CHUNK, CHUNK)), full((LANE, LANE)), full((2 * CHUNK, 2 * CHUNK)),
    ]
    ml_specs = [full((4, 2 * width)), full((1, 2 * width)), full((1, LANE)), full((1, width)), full((CHUNK, CHUNK))]
    assert (len(col_specs), len(state_specs), len(rw_specs), len(ml_specs)) == (N_COLS, N_STATE, N_RW_PRM, N_ML_PRM)
    assert lead >= SUBLANE and batch % n_seq == 0
    in_specs = col_specs + state_specs + rw_specs + ml_specs + [pl.BlockSpec(memory_space=pl.ANY)] * (2 * aliased)
    y_spec = cols(width, 0)
    out_specs = [y_spec, y_spec, s_spec, c_spec, n_spec, m_spec]
    sds = jax.ShapeDtypeStruct
    st = lambda *shape: sds((n_seq, per) + shape, F32)
    out_shape = [sds((n_seq, g_rows, width), BF16)] * 2 + [st(n_pairs, LANE, LANE), st(heads, hd, hd),
                                                           st(heads, hd), st(1, LANE)]
    vmem = pltpu.VMEM
    scratch = [vmem((n_seq, 1, 3 * width), F32), vmem((n_seq, 1, 3 * LANE), F32)]
    scratch += [vmem((n_seq, CHUNK, width), F32) for _ in range(6)]
    scratch += [vmem((n_seq, CHUNK + SUBLANE, 2 * width), F32)]
    n_in = len(in_specs)
    shift_rkv, shift_misc, s0 = rw_state
    conv0, c0, n0, m0 = ml_state
    outs = pl.pallas_call(
        kern, grid=(per, n_chunks), in_specs=in_specs, out_specs=out_specs, out_shape=out_shape,
        scratch_shapes=scratch, input_output_aliases={n_in - 2: 0, n_in - 1: 1} if aliased else {},
        compiler_params=_params("arbitrary", "arbitrary"), name="mixers",
    )(proj, proj_misc, proj, proj, proj, proj_misc, shift_rkv, shift_misc, s0, conv0, c0, n0, m0,
      *rw_prm, *ml_prm, *(y_prev or ()))
    return outs[:2], outs[2:]


def _pad_rows(w, rows):
    return jnp.pad(w, ((0, rows - w.shape[0]), (0, 0)))


def _swiglu_weights(w_gate, w_up, w_down):
    f = w_gate.shape[1]
    fp = -(-f // FF_TILE) * FF_TILE
    pad_cols = lambda w: jnp.pad(w.astype(BF16), ((0, 0), (0, fp - f)))
    return pad_cols(w_gate), pad_cols(w_up), _pad_rows(w_down.astype(BF16), fp)


def kernel(x_prompt, x_sample, state_rwkv_shift, state_rwkv_wkv, state_mlstm_conv, state_mlstm_C,
           state_mlstm_n, state_mlstm_m, meta_tokens, ffn1_norm, ffn1_w_gate, ffn1_w_up, ffn1_w_down,
           mix_norm, w_in, rw_mu, rw_w0, rw_w2, rw_a0, rw_a2, rw_g2, rw_kk, rw_ka, rw_rk, rw_ln_w, rw_ln_b,
           ml_conv_w, ml_conv_b, ml_i_b, ml_f_b, ml_norm_w, w_br_rw, w_br_ml, w_out,
           ffn2_norm, ffn2_w_gate, ffn2_w_up, ffn2_w_down, final_norm):
    assert ffn1_norm.shape[0] == 1, "single-layer trunk"
    B, T, D = x_prompt.shape
    Bs, Ts, _ = x_sample.shape
    n_meta = meta_tokens.shape[0]
    W = rw_w0.shape[-1]
    dl, al, gl = rw_w2.shape[1], rw_a2.shape[1], rw_g2.shape[1]
    rw_heads, rw_hd = rw_rk.shape[1], rw_rk.shape[2]
    Wm = ml_norm_w.shape[-1]
    H = ml_i_b.shape[-1]
    hd = Wm // H
    K = ml_conv_w.shape[1]
    L = CHUNK
    assert rw_hd == LANE // 2 and dl == LANE // 2 and al == LANE // 2 and gl <= 2 * LANE
    assert K == 4 and hd % LANE == 0 and 2 * H <= LANE and W == Wm and 2 * W == D
    assert Ts <= L and Ts % (2 * SUBLANE) == 0 and (11 * D // 2) % PROJ_COL_TILE == 0

    S = SEQS_PER_STEP
    lead = (-n_meta) % L
    head_rows = lead + n_meta
    n_head = head_rows // L
    n_chunks = n_head + T // L
    lead_s = L - Ts
    pb, ps = B // S, Bs // S
    main_rows, extra_rows = pb * T, pb * head_rows + ps * Ts
    G = main_rows + extra_rows
    N = S * G
    assert B % S == 0 and Bs % S == 0 and T % L == 0 and T % ROW_TILE == 0
    assert extra_rows % ROW_TILE == 0 and N % PROJ_ROW_TILE == 0 and N % MERGE_ROW_TILE == 0
    head = jnp.concatenate([jnp.zeros((lead, D), F32), meta_tokens.astype(F32)], axis=0)
    x_extra = jnp.concatenate([jnp.broadcast_to(head[None, None], (S, pb, head_rows, D)).reshape(S, -1, D),
                               x_sample.reshape(S, ps * Ts, D)], axis=1).reshape(S * extra_rows, D)

    ffn1_w = _swiglu_weights(ffn1_w_gate[0], ffn1_w_up[0], ffn1_w_down[0])
    ffn2_w = _swiglu_weights(ffn2_w_gate[0], ffn2_w_up[0], ffn2_w_down[0])
    wi = w_in[0]
    o_lora = 3 * W
    o_qk = o_lora + dl + al + gl
    o_v = o_qk + 2 * Wm
    o_o = o_v + Wm
    o_i = o_o + Wm
    o_gate = o_i + 2 * H
    zc = lambda n: jnp.zeros((D, n), wi.dtype)
    misc_w = jnp.concatenate([wi[:, o_lora:o_qk], zc(MISC_IF - (dl + al + gl)),
                              wi[:, o_i:o_gate], zc(MISC_COLS - MISC_IF - 2 * H)], axis=1)
    w_proj = jnp.concatenate([wi[:, o_gate:], wi[:, o_qk:o_v], wi[:, :o_lora], wi[:, o_v:o_i]],
                             axis=1).astype(BF16)

    def misc_vec(v_lora, fill=0.0):
        return jnp.pad(v_lora, [(0, 0)] * (v_lora.ndim - 1) + [(0, 3 * LANE - v_lora.shape[-1])],
                       constant_values=fill)

    row = lambda v: v.reshape(1, -1).astype(F32)
    rw_prm = (
        row(rw_mu[0, :o_lora]), misc_vec(row(rw_mu[0, o_lora:])),
        row(rw_w0[0]), _pad_rows(rw_w2[0], LANE).astype(BF16),
        row(rw_a0[0]), jnp.concatenate([jnp.zeros((dl, W), F32), rw_a2[0]], axis=0).astype(BF16),
        _pad_rows(rw_g2[0], 2 * LANE).astype(BF16),
        row(rw_kk[0]), row(rw_ka[0]), row(rw_rk[0]), row(rw_ln_w[0]), row(rw_ln_b[0]),
    )
    ifb = jnp.pad(jnp.concatenate([ml_i_b[0], ml_f_b[0]]).reshape(1, 2 * H), ((0, 0), (0, LANE - 2 * H)))
    tri = jnp.tril(jnp.ones((L, L), BF16))
    same_head = jnp.kron(jnp.eye(LANE // rw_hd, dtype=BF16), jnp.ones((rw_hd, rw_hd), BF16))
    rw_prm = rw_prm + (tri, same_head, jnp.eye(2 * L, dtype=F32))
    ml_prm = (ml_conv_w[0].astype(F32), row(ml_conv_b[0]), ifb.astype(F32), row(ml_norm_w[0]), tri)

    def rw_state_in(shift, wkv):
        b = shift.shape[0]
        s = wkv.reshape(b, rw_heads // 2, 2, rw_hd, rw_hd)
        z = jnp.zeros_like(s[:, :, 0])
        bd = jnp.concatenate([jnp.concatenate([s[:, :, 0], z], axis=-1),
                              jnp.concatenate([z, s[:, :, 1]], axis=-1)], axis=-2)
        return shift[:, None, :o_lora], misc_vec(shift[:, None, o_lora:]), bd

    def rw_state_out(bd):
        h = rw_hd
        return jnp.stack([bd[:, :, :h, :h], bd[:, :, h:, h:]], axis=2).reshape(bd.shape[0], rw_heads, h, h)

    def conv_in(buf):
        return jnp.pad(buf, ((0, 0), (SUBLANE - (K - 1), 0), (0, 0)))

    def m_in(m):
        return jnp.pad(m, ((0, 0), (0, LANE - H)))[:, None, :]

    x1 = _ffn((x_prompt, x_extra), row(ffn1_norm[0]), *ffn1_w, row(final_norm),
              groups=S, split_out=None, final_norm=False)
    proj, proj_misc = _proj(x1, row(mix_norm[0]), w_proj, misc_w.astype(BF16))
    proj_g, proj_misc_g = proj.reshape(S, G, -1), proj_misc.reshape(S, G, -1)

    zeros = lambda *s: jnp.zeros(s, F32)
    main_chunks = T // L

    def prompt_row(p, c):
        return jnp.where(c < n_head, main_rows // L + p * n_head + c, p * main_chunks + c - n_head)

    sample_base = main_rows + pb * head_rows
    seqs = (
        dict(batch=B, n_chunks=n_chunks, rows=L, seq_row=prompt_row, lead=lead),
        dict(batch=Bs, n_chunks=1, rows=Ts, seq_row=lambda p, c: sample_base // Ts + p, lead=lead_s),
    )
    grouped = lambda a: a.reshape((S, a.shape[0] // S) + a.shape[1:])
    flat = lambda a: a.reshape((a.shape[0] * a.shape[1],) + a.shape[2:])
    rw_states = (
        rw_state_in(zeros(B, o_qk), zeros(B, rw_heads, rw_hd, rw_hd)),
        rw_state_in(state_rwkv_shift[0], state_rwkv_wkv[0]),
    )
    ml_states = (
        (conv_in(zeros(B, K - 1, 2 * Wm)), zeros(B, H, hd, hd), zeros(B, H, hd), m_in(zeros(B, H))),
        (conv_in(state_mlstm_conv[0]), state_mlstm_C[0], state_mlstm_n[0], m_in(state_mlstm_m[0])),
    )
    ys = None
    rw_out, ml_out = [], []
    for seq, rws, mls in zip(seqs, rw_states, ml_states):
        ys, (s_end, c_end, n_end, m_end) = _mixers(
            proj_g, proj_misc_g, ys, tuple(map(grouped, rws)), tuple(map(grouped, mls)), rw_prm, ml_prm,
            n_seq=S, width=W, heads=H, hd=hd, d_model=D, **seq)
        rw_out.append(rw_state_out(flat(s_end)))
        ml_out.append((flat(c_end), flat(n_end), flat(m_end)[:, 0, :H]))

    x2 = _merge(x1, ys[0].reshape(N, W), ys[1].reshape(N, Wm), proj, w_br_rw[0].astype(BF16),
                w_br_ml[0].astype(BF16), w_out[0].astype(BF16))
    y_prompt, y_extra = _ffn((x2,), row(ffn2_norm[0]), *ffn2_w, row(final_norm),
                             groups=S, split_out=(B, T), final_norm=True)

    c_rkv = 2 * D + 2 * Wm

    def seq_states(base, per_group, t_len):
        seq = jnp.arange(S * per_group)
        end = (seq // per_group) * G + base + (seq % per_group + 1) * t_len
        last = end[:, None] - (K - 1) + jnp.arange(K - 1)[None, :]
        tail = proj[last.reshape(-1)].reshape(S * per_group, K - 1, -1)
        tail_misc = proj_misc[last.reshape(-1)].reshape(S * per_group, K - 1, -1)
        shift = jnp.concatenate([tail[:, -1, c_rkv:c_rkv + 3 * W], tail_misc[:, -1, :dl + al + gl]], axis=-1)
        return shift, tail[:, :, 2 * D:2 * D + 2 * Wm]

    p_shift, p_conv = seq_states(0, pb, T)
    s_shift, s_conv = seq_states(sample_base, ps, Ts)
    y_sample = y_extra.reshape(S, extra_rows, D)[:, pb * head_rows:].reshape(Bs, Ts, D)
    d1 = lambda a: a[None]
    return (y_prompt, y_sample,
            d1(p_shift), d1(rw_out[0]), d1(p_conv), d1(ml_out[0][0]), d1(ml_out[0][1]), d1(ml_out[0][2]),
            d1(s_shift), d1(rw_out[1]), d1(s_conv), d1(ml_out[1][0]), d1(ml_out[1][1]), d1(ml_out[1][2]))
```

```python
import functools

import jax
import jax.numpy as jnp
from jax import lax
from jax.experimental import pallas as pl
from jax.experimental.pallas import tpu as pltpu

F32 = jnp.float32
BF16 = jnp.bfloat16

LANE = 128
SUBLANE = 8
VMEM_LIMIT_BYTES = 56 * 1024 * 1024
CHUNK = 64
ROW_TILE = 512
PROJ_ROW_TILE = 1024
FF_TILE = 512
PROJ_COL_TILE = 1024
MERGE_ROW_TILE = 256
SEQS_PER_STEP = 2
MISC_COLS = 512
MISC_IF = 384
RMS_EPS = 1e-6
RW_GN_EPS = 64e-5
ML_LN_EPS = 1e-5
NEG = -1e30
DECAY_SCALE = 0.6065306597126334


def _dot(a, b):
    return jnp.dot(a, b, preferred_element_type=F32)


def _dot_nt(a, b):
    return lax.dot_general(a, b, (((1,), (1,)), ((), ())), preferred_element_type=F32)


def _dot_tn(a, b):
    return lax.dot_general(a, b, (((0,), (0,)), ((), ())), preferred_element_type=F32)


def _split3(x):
    h1 = x.astype(BF16)
    r1 = x - h1.astype(F32)
    h2 = r1.astype(BF16)
    r2 = r1 - h2.astype(F32)
    return h1, h2, r2.astype(BF16)


def _cumsum_rows(tri, x):
    n = x.shape[1]
    y = _dot(tri, jnp.concatenate(_split3(x), axis=1))
    return y[:, :n] + y[:, n:2 * n] + y[:, 2 * n:]


def _segsum(x, seg):
    rows = x.shape[0]
    h1, h2, _ = _split3(x)
    y = _dot(jnp.concatenate([h1, h2], axis=0), seg)
    return y[:rows] + y[rows:]


def _sigmoid(z):
    return jax.nn.sigmoid(z)


def _rmsnorm(x, g):
    return x * lax.rsqrt(jnp.mean(x * x, axis=-1, keepdims=True) + RMS_EPS) * g


def _iota(shape, dim):
    return lax.broadcasted_iota(jnp.int32, shape, dim)


def _params(*semantics):
    return pltpu.CompilerParams(dimension_semantics=semantics, vmem_limit_bytes=VMEM_LIMIT_BYTES)


def _ffn_kernel(*refs, n_f, tiles, split_in, split_out, final_norm):
    n_x = 2 if split_in else 1
    n_o = 2 if split_out else 1
    x_refs = refs[:n_x]
    g_ref, wg_ref, wu_ref, wd_ref, fg_ref = refs[n_x:n_x + 5]
    o_refs = refs[n_x + 5:n_x + 5 + n_o]
    xn_ref, acc_ref = refs[n_x + 5 + n_o:]
    in_main = pl.program_id(1) < tiles[1]
    j = pl.program_id(2)

    def load_x():
        if split_in:
            return jnp.where(in_main, x_refs[0][...], x_refs[1][...])
        return x_refs[0][...]

    @pl.when(j == 0)
    def _():
        xn_ref[...] = _rmsnorm(load_x(), g_ref[...]).astype(BF16)
        acc_ref[...] = jnp.zeros_like(acc_ref)

    xn = xn_ref[...]
    gate = _dot(xn, wg_ref[...])
    h = (gate * _sigmoid(gate) * _dot(xn, wu_ref[...])).astype(BF16)
    acc_ref[...] += _dot(h, wd_ref[...])

    def finish(o_ref):
        y = load_x() + 0.5 * acc_ref[...]
        if final_norm:
            y = _rmsnorm(y, fg_ref[...])
        o_ref[...] = y

    last = j == n_f - 1
    if split_out:
        pl.when(last & in_main)(lambda: finish(o_refs[0]))
        pl.when(last & jnp.logical_not(in_main))(lambda: finish(o_refs[1]))
    else:
        pl.when(last)(lambda: finish(o_refs[0]))


def _ffn(xs, norm_g, wg, wu, wd, final_g, *, groups, split_out, final_norm):
    split_in = len(xs) == 2
    d = xs[-1].shape[-1]
    if split_in:
        b, t = xs[0].shape[:2]
        n = b * t + xs[1].shape[0]
    else:
        n = xs[0].shape[0]
        b, t = split_out if split_out else (groups, n // groups)
    n_f = wd.shape[0] // FF_TILE
    tpb = t // ROW_TILE
    mg = (b // groups) * tpb
    tg = n // ROW_TILE // groups
    eg = tg - mg
    kern = functools.partial(_ffn_kernel, n_f=n_f, tiles=(tg, mg), split_in=split_in,
                             split_out=bool(split_out), final_norm=final_norm)
    whole = pl.BlockSpec((ROW_TILE, d), lambda g, r, j: (g * tg + r, 0))

    def main_index(g, r, j):
        m = g * mg + jnp.minimum(r, mg - 1)
        return m // tpb, m % tpb, 0

    def extra_index(g, r, j):
        return g * eg + jnp.maximum(r - mg, 0), 0

    main = pl.BlockSpec((None, ROW_TILE, d), main_index)
    extra = pl.BlockSpec((ROW_TILE, d), extra_index)
    return pl.pallas_call(
        kern,
        grid=(groups, tg, n_f),
        in_specs=([main, extra] if split_in else [whole]) + [
            pl.BlockSpec((1, d), lambda g, r, j: (0, 0)),
            pl.BlockSpec((d, FF_TILE), lambda g, r, j: (0, j)),
            pl.BlockSpec((d, FF_TILE), lambda g, r, j: (0, j)),
            pl.BlockSpec((FF_TILE, d), lambda g, r, j: (j, 0)),
            pl.BlockSpec((1, d), lambda g, r, j: (0, 0)),
        ],
        out_specs=[main, extra] if split_out else whole,
        out_shape=([jax.ShapeDtypeStruct((b, t, d), F32), jax.ShapeDtypeStruct((n - b * t, d), F32)]
                   if split_out else jax.ShapeDtypeStruct((n, d), F32)),
        scratch_shapes=[pltpu.VMEM((ROW_TILE, d), BF16), pltpu.VMEM((ROW_TILE, d), F32)],
        compiler_params=_params("arbitrary", "arbitrary", "arbitrary"),
        name="ffn_final" if final_norm else "ffn",
    )(*xs, norm_g, wg, wu, wd, final_g)


def _proj_kernel(x_ref, g_ref, w_ref, wm_ref, o_ref, om_ref, xn_ref):
    @pl.when(pl.program_id(1) == 0)
    def _():
        xn_ref[...] = _rmsnorm(x_ref[...], g_ref[...]).astype(BF16)
        om_ref[...] = _dot(xn_ref[...], wm_ref[...])

    o_ref[...] = _dot(xn_ref[...], w_ref[...])


def _proj(x, norm_g, w, w_misc):
    n, d = x.shape
    cols = w.shape[1]
    tm, tn = PROJ_ROW_TILE, PROJ_COL_TILE
    return pl.pallas_call(
        _proj_kernel,
        grid=(n // tm, cols // tn),
        in_specs=[
            pl.BlockSpec((tm, d), lambda i, j: (i, 0)),
            pl.BlockSpec((1, d), lambda i, j: (0, 0)),
            pl.BlockSpec((d, tn), lambda i, j: (0, j)),
            pl.BlockSpec((d, MISC_COLS), lambda i, j: (0, 0)),
        ],
        out_specs=[pl.BlockSpec((tm, tn), lambda i, j: (i, j)),
                   pl.BlockSpec((tm, MISC_COLS), lambda i, j: (i, 0))],
        out_shape=[jax.ShapeDtypeStruct((n, cols), F32), jax.ShapeDtypeStruct((n, MISC_COLS), F32)],
        scratch_shapes=[pltpu.VMEM((tm, d), BF16)],
        compiler_params=_params("parallel", "arbitrary"),
        name="in_proj",
    )(x, norm_g, w, w_misc)


def _merge_kernel(x_ref, yrw_ref, yml_ref, ga_ref, gb_ref, wrw_ref, wml_ref, wo_ref, o_ref):
    merged = (_sigmoid(ga_ref[...]) * _dot(yrw_ref[...], wrw_ref[...])
              + _sigmoid(gb_ref[...]) * _dot(yml_ref[...], wml_ref[...]))
    o_ref[...] = x_ref[...] + _dot(merged.astype(BF16), wo_ref[...])


def _merge(x, y_rw, y_ml, proj, w_rw, w_ml, w_out):
    n, d = x.shape
    w = y_rw.shape[1]
    tm = MERGE_ROW_TILE
    rows = lambda width, blk: pl.BlockSpec((tm, width), lambda i: (i, blk))
    resident = lambda shape: pl.BlockSpec(shape, lambda i: (0, 0), pipeline_mode=pl.Buffered(1))
    return pl.pallas_call(
        _merge_kernel,
        grid=(n // tm,),
        in_specs=[rows(d, 0), rows(w, 0), rows(w, 0), rows(d, 0), rows(d, 1),
                  resident((w, d)), resident((w, d)), resident((d, d))],
        out_specs=rows(d, 0),
        out_shape=jax.ShapeDtypeStruct((n, d), F32),
        compiler_params=_params("parallel"),
        name="merge",
    )(x, y_rw, y_ml, proj, proj, w_rw, w_ml, w_out)


def _each(f, *lists):
    return [f(*xs) for xs in zip(*lists)]


def _rwkv_pairs(r, k, v, a, lw, g, s0, kkw, kaw, rkw, lnw, lnb, cst, result):
    L = r[0].shape[0]
    tri, seg, head0, strict, incl, col_head0, eye2, blockdiag = cst
    bf = lambda x: x.astype(BF16)

    kk_ = _each(lambda k_, w_: k_ * w_, k, kkw)
    kn = _each(lambda x: _segsum(x * x, seg), kk_)
    cum = _each(lambda x: _cumsum_rows(tri, x), lw)
    yield
    kk = _each(lambda x, n_: x / jnp.maximum(jnp.sqrt(n_), 1e-12), kk_, kn)
    k2 = _each(lambda k_, a_, w_: k_ * (1.0 + (a_ - 1.0) * w_), k, a, kaw)
    kka = _each(lambda x, a_: x * a_, kk, a)
    c_end = [c[L - 1:L, :] for c in cum]
    e_inv = [jnp.exp(-c) for c in cum]
    e_rem = _each(lambda c, ce: jnp.exp(ce - c), cum, c_end)
    at = _each(lambda x, c, l_: bf(-x * jnp.exp(c - l_)), kk, cum, lw)
    rt = _each(lambda x, c: bf(x * jnp.exp(c)), r, cum)
    bt = _each(lambda x, e: bf(x * e), kka, e_inv)
    kt = _each(lambda x, e: bf(x * e), k2, e_inv)
    vb = [bf(x) for x in v]
    zero = jnp.zeros_like(vb[0])

    def by_head(x):
        return jnp.concatenate([jnp.where(head0, x, zero), jnp.where(head0, zero, x)], axis=0)

    lhs = _each(lambda x, y_: jnp.concatenate([x, y_], axis=0), at, rt)
    ab = _each(lambda l_, x: _dot_nt(l_, by_head(x)), lhs, bt)
    ak = _each(lambda l_, x: _dot_nt(l_, by_head(x)), lhs, kt)
    yield
    a_rb = [jnp.where(incl, x[L:], 0.0) for x in ab]
    a_ak = [jnp.where(strict, x[:L], 0.0) for x in ak]
    a_rk = [jnp.where(incl, x[L:], 0.0) for x in ak]

    def blockdiag2(x):
        n_cat = jnp.where(strict, x[:L], 0.0)
        return jnp.concatenate([jnp.where(col_head0, n_cat, 0.0), jnp.where(col_head0, 0.0, n_cat)], axis=0)

    p = [blockdiag2(x) for x in ab]
    t_inv = [eye2 + x for x in p]
    sb = [bf(x) for x in s0]
    vs = [by_head(x) for x in vb]
    x0 = _each(lambda a_, s_, m_, v_: _dot_nt(a_, s_) + _dot(bf(m_), v_), at, sb, a_ak, vs)
    for _ in range(L.bit_length() - 2):
        p = [_dot(bf(x), bf(x)) for x in p]
        yield
        t_inv = _each(lambda t_, x: t_ + _dot(bf(t_), bf(x)), t_inv, p)
    yield
    us = _each(lambda t_, x: _dot(bf(t_), by_head(bf(x))), t_inv, x0)
    yield
    y = _each(lambda r_, s_, m1, m2, u_, v_: _dot_nt(r_, s_) + _dot(
        bf(jnp.concatenate([m1, m2], axis=1)), jnp.concatenate([bf(u_), v_], axis=0)),
        rt, sb, a_rb, a_rk, us, vs)
    upd = _each(lambda u_, v_, x1, x2, e: _dot_tn(
        jnp.concatenate([bf(u_[:L] + u_[L:]), v_], axis=0),
        jnp.concatenate([bf(x1 * e), bf(x2 * e)], axis=0)), us, vb, kka, k2, e_rem)
    yield
    s_new = _each(lambda s_, ce, u_: s_ * jnp.exp(ce) + jnp.where(blockdiag, u_, 0.0), s0, c_end, upd)

    hd = float(LANE // 2)
    mean = [_segsum(x, seg) / hd for x in y]
    bsum = _each(lambda r_, k_, w_: _segsum(r_ * k_ * w_, seg), r, k2, rkw)
    yield
    yc = _each(lambda x, m_: x - m_, y, mean)
    var = [_segsum(x * x, seg) / hd for x in yc]
    yield
    out = _each(lambda x, v_, w_, b_, bs, vv, g_: (x * lax.rsqrt(v_ + RW_GN_EPS) * w_ + b_ + bs * vv) * g_,
                yc, var, lnw, lnb, bsum, v, g)
    result.extend([out, s_new])


def _rwkv_steps(c, seqs, prm, *, rows, lead, width):
    (mu_rkv_ref, mu_misc_ref, w0_ref, w2_ref, a0_ref, a2_ref, g2_ref,
     kk_ref, ka_ref, rk_ref, lnw_ref, lnb_ref, tri_ref, seg_ref, eye_ref) = prm
    L = CHUNK
    row = _iota((L, 1), 0)
    first = jnp.where(c == 0, lead, 0)
    valid = row >= first
    for (rkv_ref, misc_ref, _, _, carry_rkv, carry_misc, r_s, k_s, v_s, a_s, lw_s, g_s) in seqs:
        p = rkv_ref[...]
        pm = misc_ref[:, :3 * LANE]
        if rows < L:
            p = jnp.concatenate([jnp.zeros((L - rows, p.shape[1]), F32), p], axis=0)
            pm = jnp.concatenate([jnp.zeros((L - rows, pm.shape[1]), F32), pm], axis=0)
        p = jnp.where(valid, p, 0.0)
        pm = jnp.where(valid, pm, 0.0)

        def shift_mix(cur, carry_ref, mu):
            prev = jnp.where(row == first, carry_ref[...], pltpu.roll(cur, 1, axis=0))
            carry_ref[...] = cur[L - 1:L, :]
            return jnp.where(valid, cur + mu * (prev - cur), 0.0)

        u = shift_mix(p, carry_rkv, mu_rkv_ref[...])
        um = shift_mix(pm, carry_misc, mu_misc_ref[...])

        lora = um[:, :LANE]
        wl = w0_ref[...] + _dot(jnp.tanh(lora).astype(BF16), w2_ref[...])
        lw_s[...] = jnp.where(valid, -DECAY_SCALE * _sigmoid(wl), 0.0)
        a_s[...] = _sigmoid(a0_ref[...] + _dot(lora.astype(BF16), a2_ref[...]))
        g_s[...] = _dot(_sigmoid(um[:, LANE:]).astype(BF16), g2_ref[...])
        r_s[...] = u[:, :width]
        k_s[...] = u[:, width:2 * width]
        v_s[...] = u[:, 2 * width:]
        yield

    lane = _iota((1, LANE), 1)
    half = LANE // 2
    col2 = _iota((L, 2 * L), 1)
    t_i = _iota((L, 2 * L), 0)
    s_i = col2 & (L - 1)
    cst = (
        tri_ref[...], seg_ref[...],
        lane < half,
        s_i < t_i, s_i <= t_i,
        col2 < L,
        eye_ref[...],
        (_iota((LANE, LANE), 0) < half) == (_iota((LANE, LANE), 1) < half),
    )
    sls = [slice(j * LANE, (j + 1) * LANE) for j in range(width // LANE)]
    units = [(seq, j, sl) for seq in seqs for j, sl in enumerate(sls)]
    from_seq = lambda k: [seq[k][:, sl] for seq, _, sl in units]
    from_prm = lambda ref: [ref[:, sl] for _, _, sl in units]
    result = []
    yield from _rwkv_pairs(from_seq(6), from_seq(7), from_seq(8), from_seq(9), from_seq(10), from_seq(11),
                           [seq[3][0, j] for seq, j, _ in units],
                           from_prm(kk_ref), from_prm(ka_ref), from_prm(rk_ref), from_prm(lnw_ref),
                           from_prm(lnb_ref), cst, result)
    outs, s_new = result
    for (seq, j, sl), out, s_end in zip(units, outs, s_new):
        seq[3][0, j] = s_end
        seq[2][:, sl] = out[L - rows:, :].astype(seq[2].dtype)


def _mlstm_steps(c, refs, *, rows, lead, heads, hd):
    (qk_ref, v_ref, o_ref, misc_ref, conv0_ref, cw_ref, cb_ref, ifb_ref, nw_ref, tri_ref,
     y_ref, c_ref, n_ref, m_ref, ext) = refs
    L = CHUNK
    width = heads * hd

    def chunk_rows(x):
        if rows < L:
            return jnp.concatenate([jnp.zeros((L - rows, x.shape[1]), F32), x], axis=0)
        return x

    row = _iota((L, 1), 0)
    valid = row >= jnp.where(c == 0, lead, 0)

    carried = jnp.concatenate([jnp.zeros((lead - SUBLANE, 2 * width), F32), conv0_ref[0],
                               jnp.zeros((L - lead, 2 * width), F32)], axis=0)
    use_carried = (c == 0) & (row >= lead - SUBLANE) & (row < lead)
    ext[SUBLANE:SUBLANE + L, :] = jnp.where(valid, chunk_rows(qk_ref[...]), jnp.where(use_carried, carried, 0.0))
    conv = cb_ref[...]
    for j in range(4):
        conv = conv + cw_ref[j:j + 1, :] * ext[SUBLANE - 3 + j:SUBLANE - 3 + j + L, :]
    ext[0:SUBLANE, :] = ext[L:L + SUBLANE, :]
    qk = jnp.where(valid, conv * _sigmoid(conv), 0.0)
    v = jnp.where(valid, chunk_rows(v_ref[...]), 0.0)
    og = _sigmoid(chunk_rows(o_ref[...]))
    yield

    lane = _iota((1, LANE), 1)
    is_i = lane < heads
    z = chunk_rows(misc_ref[:, MISC_IF:MISC_IF + LANE]) + ifb_ref[...]
    logsig = jnp.minimum(z, 0.0) - jnp.log1p(jnp.exp(-jnp.abs(z)))
    gates = jnp.where(valid, jnp.where(is_i, z, logsig), jnp.where(is_i, NEG, 0.0))
    cum = _cumsum_rows(tri_ref[...], jnp.where(is_i, 0.0, gates))
    pad = jnp.zeros((LANE - L, LANE), F32)
    gates_t = jnp.concatenate([gates, pad], axis=0).T
    cum_t = jnp.concatenate([cum, pad], axis=0).T
    causal = _iota((L, L), 1) <= _iota((L, L), 0)
    m_all = m_ref[0]
    yield

    for h in range(heads):
        sl = slice(h * hd, (h + 1) * hd)
        q = qk[:, sl]
        kx = qk[:, width + h * hd:width + (h + 1) * hd] * (hd ** -0.5)
        vh = v[:, sl]
        b_col = cum[:, heads + h:heads + h + 1]
        i_col = gates[:, h:h + 1]
        b_row = cum_t[heads + h:heads + h + 1, :L]
        i_row = gates_t[h:h + 1, :L]
        b_end = b_col[L - 1:L, :]
        m_prev = jnp.sum(jnp.where(lane == h, m_all, 0.0), axis=1, keepdims=True)

        log_inter = b_col + m_prev
        dmat = jnp.where(causal, b_col - b_row + i_row, NEG)
        m_q = jnp.maximum(log_inter, jnp.max(dmat, axis=-1, keepdims=True))
        w_inter = jnp.exp(log_inter - m_q)
        qb = q.astype(BF16)
        s = _dot_nt(qb, kx.astype(BF16)) * jnp.exp(dmat - m_q)
        c_h = c_ref[0, h]
        n_h = n_ref[0, h:h + 1, :]
        yield
        num = w_inter * _dot(qb, c_h.astype(BF16)) + _dot(s.astype(BF16), vh.astype(BF16))
        den = w_inter * jnp.sum(q * n_h, axis=-1, keepdims=True) + jnp.sum(s, axis=-1, keepdims=True)
        hcell = num / jnp.maximum(jnp.abs(den), jnp.exp(-m_q))
        yield

        g_col = b_end - b_col + i_col
        m_new = jnp.maximum(b_end + m_prev, jnp.max(g_col, axis=0, keepdims=True))
        a_st = jnp.exp(b_end + m_prev - m_new)
        wkk = jnp.exp(g_col - m_new) * kx
        c_ref[0, h] = a_st * c_h + _dot_tn(wkk.astype(BF16), vh.astype(BF16))
        n_ref[0, h:h + 1, :] = a_st * n_h + jnp.sum(wkk, axis=0, keepdims=True)
        m_all = jnp.where(lane == h, m_new, m_all)

        mu = jnp.mean(hcell, axis=-1, keepdims=True)
        hc = hcell - mu
        var = jnp.mean(hc * hc, axis=-1, keepdims=True)
        yh = hc * lax.rsqrt(var + ML_LN_EPS) * nw_ref[:, sl] * og[:, sl]
        y_ref[:, sl] = yh[L - rows:, :].astype(y_ref.dtype)
        yield

    m_ref[0] = m_all


N_COLS, N_STATE, N_RW_PRM, N_ML_PRM = 6, 7, 15, 5


def _mixers_kernel(*refs, n_seq, aliased, rows, lead, width, heads, hd):
    pos = 0

    def take(n):
        nonlocal pos
        pos += n
        return refs[pos - n:pos]

    rkv_ref, misc_ref, qk_ref, v_ref, o_ref, misc2_ref = take(N_COLS)
    sh_rkv_ref, sh_misc_ref, s0_ref, conv0_ref, c0_ref, n0_ref, m0_ref = take(N_STATE)
    rw_prm, ml_prm = take(N_RW_PRM), take(N_ML_PRM)
    take(2 if aliased else 0)
    y_rw_ref, y_ml_ref, s_ref, c_ref, n_ref, m_ref = take(6)
    carry_rkv, carry_misc, r_s, k_s, v_s, a_s, lw_s, g_s, ext = take(9)
    c = pl.program_id(1)

    @pl.when(c == 0)
    def _():
        carry_rkv[...] = sh_rkv_ref[:, 0]
        carry_misc[...] = sh_misc_ref[:, 0]
        s_ref[...] = s0_ref[...]
        c_ref[...] = c0_ref[...]
        n_ref[...] = n0_ref[...]
        m_ref[...] = m0_ref[...]
        ext[:, 0:SUBLANE, :] = jnp.zeros((n_seq, SUBLANE, ext.shape[2]), F32)

    parts = [
        _mlstm_steps(c, tuple(ref.at[q] for ref in (qk_ref, v_ref, o_ref, misc2_ref, conv0_ref)) + tuple(ml_prm)
                     + tuple(ref.at[q] for ref in (y_ml_ref, c_ref, n_ref, m_ref, ext)),
                     rows=rows, lead=lead, heads=heads, hd=hd)
        for q in range(n_seq)
    ]
    parts.append(_rwkv_steps(
        c, [tuple(ref.at[q] for ref in (rkv_ref, misc_ref, y_rw_ref, s_ref, carry_rkv, carry_misc,
                                        r_s, k_s, v_s, a_s, lw_s, g_s)) for q in range(n_seq)],
        rw_prm, rows=rows, lead=lead, width=width))
    while parts:
        for part in list(parts):
            if next(part, StopIteration) is StopIteration:
                parts.remove(part)


def _mixers(proj, proj_misc, y_prev, rw_state, ml_state, rw_prm, ml_prm, *,
            n_seq, batch, n_chunks, rows, seq_row, lead, width, heads, hd, d_model):
    g_rows = proj.shape[1]
    n_pairs = width // LANE
    per = batch // n_seq
    rkv_blk = (2 * d_model + 2 * width) // (3 * width)
    qk_blk = (2 * d_model) // (2 * width)
    v_blk = (2 * d_model + 2 * width + 3 * width) // width
    aliased = y_prev is not None
    kern = functools.partial(_mixers_kernel, n_seq=n_seq, aliased=aliased, rows=rows, lead=lead, width=width,
                             heads=heads, hd=hd)
    full = lambda shape: pl.BlockSpec(shape, lambda p, c: (0,) * len(shape))
    state = lambda *shape: pl.BlockSpec((n_seq, 1) + shape, lambda p, c: (0, p) + (0,) * len(shape))
    cols = lambda w_, blk: pl.BlockSpec((n_seq, rows, w_), lambda p, c: (0, seq_row(p, c), blk))
    col_specs = [cols(3 * width, rkv_blk), cols(MISC_COLS, 0), cols(2 * width, qk_blk),
                 cols(width, v_blk), cols(width, v_blk + 1), cols(MISC_COLS, 0)]
    s_spec, c_spec, n_spec, m_spec = state(n_pairs, LANE, LANE), state(heads, hd, hd), state(heads, hd), state(1, LANE)
    state_specs = [state(1, 3 * width), state(1, 3 * LANE), s_spec, state(SUBLANE, 2 * width), c_spec, n_spec, m_spec]
    rw_specs = [
        full((1, 3 * width)), full((1, 3 * LANE)),
        full((1, width)), full((LANE, width)), full((1, width)), full((LANE, width)), full((2 * LANE, width)),
        full((1, width)), full((1, width)), full((1, width)), full((1, width)), full((1, width)),
        full((CHUNK, CHUNK)), full((LANE, LANE)), full((2 * CHUNK, 2 * CHUNK)),
    ]
    ml_specs = [full((4, 2 * width)), full((1, 2 * width)), full((1, LANE)), full((1, width)), full((CHUNK, CHUNK))]
    assert (len(col_specs), len(state_specs), len(rw_specs), len(ml_specs)) == (N_COLS, N_STATE, N_RW_PRM, N_ML_PRM)
    assert lead >= SUBLANE and batch % n_seq == 0
    in_specs = col_specs + state_specs + rw_specs + ml_specs + [pl.BlockSpec(memory_space=pl.ANY)] * (2 * aliased)
    y_spec = cols(width, 0)
    out_specs = [y_spec, y_spec, s_spec, c_spec, n_spec, m_spec]
    sds = jax.ShapeDtypeStruct
    st = lambda *shape: sds((n_seq, per) + shape, F32)
    out_shape = [sds((n_seq, g_rows, width), BF16)] * 2 + [st(n_pairs, LANE, LANE), st(heads, hd, hd),
                                                           st(heads, hd), st(1, LANE)]
    vmem = pltpu.VMEM
    scratch = [vmem((n_seq, 1, 3 * width), F32), vmem((n_seq, 1, 3 * LANE), F32)]
    scratch += [vmem((n_seq, CHUNK, width), F32) for _ in range(6)]
    scratch += [vmem((n_seq, CHUNK + SUBLANE, 2 * width), F32)]
    n_in = len(in_specs)
    shift_rkv, shift_misc, s0 = rw_state
    conv0, c0, n0, m0 = ml_state
    outs = pl.pallas_call(
        kern, grid=(per, n_chunks), in_specs=in_specs, out_specs=out_specs, out_shape=out_shape,
        scratch_shapes=scratch, input_output_aliases={n_in - 2: 0, n_in - 1: 1} if aliased else {},
        compiler_params=_params("arbitrary", "arbitrary"), name="mixers",
    )(proj, proj_misc, proj, proj, proj, proj_misc, shift_rkv, shift_misc, s0, conv0, c0, n0, m0,
      *rw_prm, *ml_prm, *(y_prev or ()))
    return outs[:2], outs[2:]


def _pad_rows(w, rows):
    return jnp.pad(w, ((0, rows - w.shape[0]), (0, 0)))


def _swiglu_weights(w_gate, w_up, w_down):
    f = w_gate.shape[1]
    fp = -(-f // FF_TILE) * FF_TILE
    pad_cols = lambda w: jnp.pad(w.astype(BF16), ((0, 0), (0, fp - f)))
    return pad_cols(w_gate), pad_cols(w_up), _pad_rows(w_down.astype(BF16), fp)


def kernel(x_prompt, x_sample, state_rwkv_shift, state_rwkv_wkv, state_mlstm_conv, state_mlstm_C,
           state_mlstm_n, state_mlstm_m, meta_tokens, ffn1_norm, ffn1_w_gate, ffn1_w_up, ffn1_w_down,
           mix_norm, w_in, rw_mu, rw_w0, rw_w2, rw_a0, rw_a2, rw_g2, rw_kk, rw_ka, rw_rk, rw_ln_w, rw_ln_b,
           ml_conv_w, ml_conv_b, ml_i_b, ml_f_b, ml_norm_w, w_br_rw, w_br_ml, w_out,
           ffn2_norm, ffn2_w_gate, ffn2_w_up, ffn2_w_down, final_norm):
    assert ffn1_norm.shape[0] == 1, "single-layer trunk"
    B, T, D = x_prompt.shape
    Bs, Ts, _ = x_sample.shape
    n_meta = meta_tokens.shape[0]
    W = rw_w0.shape[-1]
    dl, al, gl = rw_w2.shape[1], rw_a2.shape[1], rw_g2.shape[1]
    rw_heads, rw_hd = rw_rk.shape[1], rw_rk.shape[2]
    Wm = ml_norm_w.shape[-1]
    H = ml_i_b.shape[-1]
    hd = Wm // H
    K = ml_conv_w.shape[1]
    L = CHUNK
    assert rw_hd == LANE // 2 and dl == LANE // 2 and al == LANE // 2 and gl <= 2 * LANE
    assert K == 4 and hd % LANE == 0 and 2 * H <= LANE and W == Wm and 2 * W == D
    assert Ts <= L and Ts % (2 * SUBLANE) == 0 and (11 * D // 2) % PROJ_COL_TILE == 0

    S = SEQS_PER_STEP
    lead = (-n_meta) % L
    head_rows = lead + n_meta
    n_head = head_rows // L
    n_chunks = n_head + T // L
    lead_s = L - Ts
    pb, ps = B // S, Bs // S
    main_rows, extra_rows = pb * T, pb * head_rows + ps * Ts
    G = main_rows + extra_rows
    N = S * G
    assert B % S == 0 and Bs % S == 0 and T % L == 0 and T % ROW_TILE == 0
    assert extra_rows % ROW_TILE == 0 and N % PROJ_ROW_TILE == 0 and N % MERGE_ROW_TILE == 0
    head = jnp.concatenate([jnp.zeros((lead, D), F32), meta_tokens.astype(F32)], axis=0)
    x_extra = jnp.concatenate([jnp.broadcast_to(head[None, None], (S, pb, head_rows, D)).reshape(S, -1, D),
                               x_sample.reshape(S, ps * Ts, D)], axis=1).reshape(S * extra_rows, D)

    ffn1_w = _swiglu_weights(ffn1_w_gate[0], ffn1_w_up[0], ffn1_w_down[0])
    ffn2_w = _swiglu_weights(ffn2_w_gate[0], ffn2_w_up[0], ffn2_w_down[0])
    wi = w_in[0]
    o_lora = 3 * W
    o_qk = o_lora + dl + al + gl
    o_v = o_qk + 2 * Wm
    o_o = o_v + Wm
    o_i = o_o + Wm
    o_gate = o_i + 2 * H
    zc = lambda n: jnp.zeros((D, n), wi.dtype)
    misc_w = jnp.concatenate([wi[:, o_lora:o_qk], zc(MISC_IF - (dl + al + gl)),
                              wi[:, o_i:o_gate], zc(MISC_COLS - MISC_IF - 2 * H)], axis=1)
    w_proj = jnp.concatenate([wi[:, o_gate:], wi[:, o_qk:o_v], wi[:, :o_lora], wi[:, o_v:o_i]],
                             axis=1).astype(BF16)

    def misc_vec(v_lora, fill=0.0):
        return jnp.pad(v_lora, [(0, 0)] * (v_lora.ndim - 1) + [(0, 3 * LANE - v_lora.shape[-1])],
                       constant_values=fill)

    row = lambda v: v.reshape(1, -1).astype(F32)
    rw_prm = (
        row(rw_mu[0, :o_lora]), misc_vec(row(rw_mu[0, o_lora:])),
        row(rw_w0[0]), _pad_rows(rw_w2[0], LANE).astype(BF16),
        row(rw_a0[0]), jnp.concatenate([jnp.zeros((dl, W), F32), rw_a2[0]], axis=0).astype(BF16),
        _pad_rows(rw_g2[0], 2 * LANE).astype(BF16),
        row(rw_kk[0]), row(rw_ka[0]), row(rw_rk[0]), row(rw_ln_w[0]), row(rw_ln_b[0]),
    )
    ifb = jnp.pad(jnp.concatenate([ml_i_b[0], ml_f_b[0]]).reshape(1, 2 * H), ((0, 0), (0, LANE - 2 * H)))
    tri = jnp.tril(jnp.ones((L, L), BF16))
    same_head = jnp.kron(jnp.eye(LANE // rw_hd, dtype=BF16), jnp.ones((rw_hd, rw_hd), BF16))
    rw_prm = rw_prm + (tri, same_head, jnp.eye(2 * L, dtype=F32))
    ml_prm = (ml_conv_w[0].astype(F32), row(ml_conv_b[0]), ifb.astype(F32), row(ml_norm_w[0]), tri)

    def rw_state_in(shift, wkv):
        b = shift.shape[0]
        s = wkv.reshape(b, rw_heads // 2, 2, rw_hd, rw_hd)
        z = jnp.zeros_like(s[:, :, 0])
        bd = jnp.concatenate([jnp.concatenate([s[:, :, 0], z], axis=-1),
                              jnp.concatenate([z, s[:, :, 1]], axis=-1)], axis=-2)
        return shift[:, None, :o_lora], misc_vec(shift[:, None, o_lora:]), bd

    def rw_state_out(bd):
        h = rw_hd
        return jnp.stack([bd[:, :, :h, :h], bd[:, :, h:, h:]], axis=2).reshape(bd.shape[0], rw_heads, h, h)

    def conv_in(buf):
        return jnp.pad(buf, ((0, 0), (SUBLANE - (K - 1), 0), (0, 0)))

    def m_in(m):
        return jnp.pad(m, ((0, 0), (0, LANE - H)))[:, None, :]

    x1 = _ffn((x_prompt, x_extra), row(ffn1_norm[0]), *ffn1_w, row(final_norm),
              groups=S, split_out=None, final_norm=False)
    proj, proj_misc = _proj(x1, row(mix_norm[0]), w_proj, misc_w.astype(BF16))
    proj_g, proj_misc_g = proj.reshape(S, G, -1), proj_misc.reshape(S, G, -1)

    zeros = lambda *s: jnp.zeros(s, F32)
    main_chunks = T // L

    def prompt_row(p, c):
        return jnp.where(c < n_head, main_rows // L + p * n_head + c, p * main_chunks + c - n_head)

    sample_base = main_rows + pb * head_rows
    seqs = (
        dict(batch=B, n_chunks=n_chunks, rows=L, seq_row=prompt_row, lead=lead),
        dict(batch=Bs, n_chunks=1, rows=Ts, seq_row=lambda p, c: sample_base // Ts + p, lead=lead_s),
    )
    grouped = lambda a: a.reshape((S, a.shape[0] // S) + a.shape[1:])
    flat = lambda a: a.reshape((a.shape[0] * a.shape[1],) + a.shape[2:])
    rw_states = (
        rw_state_in(zeros(B, o_qk), zeros(B, rw_heads, rw_hd, rw_hd)),
        rw_state_in(state_rwkv_shift[0], state_rwkv_wkv[0]),
    )
    ml_states = (
        (conv_in(zeros(B, K - 1, 2 * Wm)), zeros(B, H, hd, hd), zeros(B, H, hd), m_in(zeros(B, H))),
        (conv_in(state_mlstm_conv[0]), state_mlstm_C[0], state_mlstm_n[0], m_in(state_mlstm_m[0])),
    )
    ys = None
    rw_out, ml_out = [], []
    for seq, rws, mls in zip(seqs, rw_states, ml_states):
        ys, (s_end, c_end, n_end, m_end) = _mixers(
            proj_g, proj_misc_g, ys, tuple(map(grouped, rws)), tuple(map(grouped, mls)), rw_prm, ml_prm,
            n_seq=S, width=W, heads=H, hd=hd, d_model=D, **seq)
        rw_out.append(rw_state_out(flat(s_end)))
        ml_out.append((flat(c_end), flat(n_end), flat(m_end)[:, 0, :H]))

    x2 = _merge(x1, ys[0].reshape(N, W), ys[1].reshape(N, Wm), proj, w_br_rw[0].astype(BF16),
                w_br_ml[0].astype(BF16), w_out[0].astype(BF16))
    y_prompt, y_extra = _ffn((x2,), row(ffn2_norm[0]), *ffn2_w, row(final_norm),
                             groups=S, split_out=(B, T), final_norm=True)

    c_rkv = 2 * D + 2 * Wm

    def seq_states(base, per_group, t_len):
        seq = jnp.arange(S * per_group)
        end = (seq // per_group) * G + base + (seq % per_group + 1) * t_len
        last = end[:, None] - (K - 1) + jnp.arange(K - 1)[None, :]
        tail = proj[last.reshape(-1)].reshape(S * per_group, K - 1, -1)
        tail_misc = proj_misc[last.reshape(-1)].reshape(S * per_group, K - 1, -1)
        shift = jnp.concatenate([tail[:, -1, c_rkv:c_rkv + 3 * W], tail_misc[:, -1, :dl + al + gl]], axis=-1)
        return shift, tail[:, :, 2 * D:2 * D + 2 * Wm]

    p_shift, p_conv = seq_states(0, pb, T)
    s_shift, s_conv = seq_states(sample_base, ps, Ts)
    y_sample = y_extra.reshape(S, extra_rows, D)[:, pb * head_rows:].reshape(Bs, Ts, D)
    d1 = lambda a: a[None]
    return (y_prompt, y_sample,
            d1(p_shift), d1(rw_out[0]), d1(p_conv), d1(ml_out[0][0]), d1(ml_out[0][1]), d1(ml_out[0][2]),
            d1(s_shift), d1(rw_out[1]), d1(s_conv), d1(ml_out[1][0]), d1(ml_out[1][1]), d1(ml_out[1][2]))
```

```python
import functools

import jax
import jax.numpy as jnp
from jax import lax
from jax.experimental import pallas as pl
from jax.experimental.pallas import tpu as pltpu

F32 = jnp.float32
BF16 = jnp.bfloat16

LANE = 128
SUBLANE = 8
VMEM_LIMIT_BYTES = 56 * 1024 * 1024
CHUNK = 64
ROW_TILE = 512
PROJ_ROW_TILE = 1024
FF_TILE = 512
PROJ_COL_TILE = 1024
MERGE_ROW_TILE = 256
SEQS_PER_STEP = 2
MISC_COLS = 512
MISC_IF = 384
RMS_EPS = 1e-6
RW_GN_EPS = 64e-5
ML_LN_EPS = 1e-5
KK_NORM_FLOOR = 1e-12
NEG = -1e30
DECAY_SCALE = 0.6065306597126334


def _dot(a, b):
    return jnp.dot(a, b, preferred_element_type=F32)


def _dot_nt(a, b):
    return lax.dot_general(a, b, (((1,), (1,)), ((), ())), preferred_element_type=F32)


def _dot_tn(a, b):
    return lax.dot_general(a, b, (((0,), (0,)), ((), ())), preferred_element_type=F32)


def _split3(x):
    h1 = x.astype(BF16)
    r1 = x - h1.astype(F32)
    h2 = r1.astype(BF16)
    r2 = r1 - h2.astype(F32)
    return h1, h2, r2.astype(BF16)


def _cumsum_rows(tri, x):
    n = x.shape[1]
    y = _dot(tri, jnp.concatenate(_split3(x), axis=1))
    return y[:, :n] + y[:, n:2 * n] + y[:, 2 * n:]


def _segsum(x, seg):
    rows = x.shape[0]
    h1, h2, _ = _split3(x)
    y = _dot(jnp.concatenate([h1, h2], axis=0), seg)
    return y[:rows] + y[rows:]


def _sigmoid(z):
    return jax.nn.sigmoid(z)


def _rmsnorm(x, g):
    return x * lax.rsqrt(jnp.mean(x * x, axis=-1, keepdims=True) + RMS_EPS) * g


def _iota(shape, dim):
    return lax.broadcasted_iota(jnp.int32, shape, dim)


def _params(*semantics):
    return pltpu.CompilerParams(dimension_semantics=semantics, vmem_limit_bytes=VMEM_LIMIT_BYTES)


def _ffn_kernel(*refs, n_f, tiles, split_in, split_out, final_norm):
    n_x = 2 if split_in else 1
    n_o = 2 if split_out else 1
    x_refs = refs[:n_x]
    g_ref, wg_ref, wu_ref, wd_ref, fg_ref = refs[n_x:n_x + 5]
    o_refs = refs[n_x + 5:n_x + 5 + n_o]
    xn_ref, acc_ref = refs[n_x + 5 + n_o:]
    in_main = pl.program_id(1) < tiles[1]
    j = pl.program_id(2)

    def load_x():
        if split_in:
            return jnp.where(in_main, x_refs[0][...], x_refs[1][...])
        return x_refs[0][...]

    @pl.when(j == 0)
    def _():
        xn_ref[...] = _rmsnorm(load_x(), g_ref[...]).astype(BF16)
        acc_ref[...] = jnp.zeros_like(acc_ref)

    xn = xn_ref[...]
    gate = _dot(xn, wg_ref[...])
    h = (gate * _sigmoid(gate) * _dot(xn, wu_ref[...])).astype(BF16)
    acc_ref[...] += _dot(h, wd_ref[...])

    def finish(o_ref):
        y = load_x() + 0.5 * acc_ref[...]
        if final_norm:
            y = _rmsnorm(y, fg_ref[...])
        o_ref[...] = y

    last = j == n_f - 1
    if split_out:
        pl.when(last & in_main)(lambda: finish(o_refs[0]))
        pl.when(last & jnp.logical_not(in_main))(lambda: finish(o_refs[1]))
    else:
        pl.when(last)(lambda: finish(o_refs[0]))


def _ffn(xs, norm_g, wg, wu, wd, final_g, *, groups, split_out, final_norm):
    split_in = len(xs) == 2
    d = xs[-1].shape[-1]
    if split_in:
        b, t = xs[0].shape[:2]
        n = b * t + xs[1].shape[0]
    else:
        n = xs[0].shape[0]
        b, t = split_out if split_out else (groups, n // groups)
    n_f = wd.shape[0] // FF_TILE
    tpb = t // ROW_TILE
    mg = (b // groups) * tpb
    tg = n // ROW_TILE // groups
    eg = tg - mg
    kern = functools.partial(_ffn_kernel, n_f=n_f, tiles=(tg, mg), split_in=split_in,
                             split_out=bool(split_out), final_norm=final_norm)
    whole = pl.BlockSpec((ROW_TILE, d), lambda g, r, j: (g * tg + r, 0))

    def main_index(g, r, j):
        m = g * mg + jnp.minimum(r, mg - 1)
        return m // tpb, m % tpb, 0

    def extra_index(g, r, j):
        return g * eg + jnp.maximum(r - mg, 0), 0

    main = pl.BlockSpec((None, ROW_TILE, d), main_index)
    extra = pl.BlockSpec((ROW_TILE, d), extra_index)
    return pl.pallas_call(
        kern,
        grid=(groups, tg, n_f),
        in_specs=([main, extra] if split_in else [whole]) + [
            pl.BlockSpec((1, d), lambda g, r, j: (0, 0)),
            pl.BlockSpec((d, FF_TILE), lambda g, r, j: (0, j)),
            pl.BlockSpec((d, FF_TILE), lambda g, r, j: (0, j)),
            pl.BlockSpec((FF_TILE, d), lambda g, r, j: (j, 0)),
            pl.BlockSpec((1, d), lambda g, r, j: (0, 0)),
        ],
        out_specs=[main, extra] if split_out else whole,
        out_shape=([jax.ShapeDtypeStruct((b, t, d), F32), jax.ShapeDtypeStruct((n - b * t, d), F32)]
                   if split_out else jax.ShapeDtypeStruct((n, d), F32)),
        scratch_shapes=[pltpu.VMEM((ROW_TILE, d), BF16), pltpu.VMEM((ROW_TILE, d), F32)],
        compiler_params=_params("arbitrary", "arbitrary", "arbitrary"),
        name="ffn_final" if final_norm else "ffn",
    )(*xs, norm_g, wg, wu, wd, final_g)


def _proj_kernel(x_ref, g_ref, w_ref, wm_ref, o_ref, om_ref, xn_ref):
    @pl.when(pl.program_id(1) == 0)
    def _():
        xn_ref[...] = _rmsnorm(x_ref[...], g_ref[...]).astype(BF16)
        om_ref[...] = _dot(xn_ref[...], wm_ref[...])

    o_ref[...] = _dot(xn_ref[...], w_ref[...])


def _proj(x, norm_g, w, w_misc):
    n, d = x.shape
    cols = w.shape[1]
    tm, tn = PROJ_ROW_TILE, PROJ_COL_TILE
    return pl.pallas_call(
        _proj_kernel,
        grid=(n // tm, cols // tn),
        in_specs=[
            pl.BlockSpec((tm, d), lambda i, j: (i, 0)),
            pl.BlockSpec((1, d), lambda i, j: (0, 0)),
            pl.BlockSpec((d, tn), lambda i, j: (0, j)),
            pl.BlockSpec((d, MISC_COLS), lambda i, j: (0, 0)),
        ],
        out_specs=[pl.BlockSpec((tm, tn), lambda i, j: (i, j)),
                   pl.BlockSpec((tm, MISC_COLS), lambda i, j: (i, 0))],
        out_shape=[jax.ShapeDtypeStruct((n, cols), F32), jax.ShapeDtypeStruct((n, MISC_COLS), F32)],
        scratch_shapes=[pltpu.VMEM((tm, d), BF16)],
        compiler_params=_params("parallel", "arbitrary"),
        name="in_proj",
    )(x, norm_g, w, w_misc)


def _merge_kernel(x_ref, yrw_ref, yml_ref, ga_ref, gb_ref, wrw_ref, wml_ref, wo_ref, o_ref):
    merged = (_sigmoid(ga_ref[...]) * _dot(yrw_ref[...], wrw_ref[...])
              + _sigmoid(gb_ref[...]) * _dot(yml_ref[...], wml_ref[...]))
    o_ref[...] = x_ref[...] + _dot(merged.astype(BF16), wo_ref[...])


def _merge(x, y_rw, y_ml, proj, w_rw, w_ml, w_out):
    n, d = x.shape
    w = y_rw.shape[1]
    tm = MERGE_ROW_TILE
    rows = lambda width, blk: pl.BlockSpec((tm, width), lambda i: (i, blk))
    resident = lambda shape: pl.BlockSpec(shape, lambda i: (0, 0), pipeline_mode=pl.Buffered(1))
    return pl.pallas_call(
        _merge_kernel,
        grid=(n // tm,),
        in_specs=[rows(d, 0), rows(w, 0), rows(w, 0), rows(d, 0), rows(d, 1),
                  resident((w, d)), resident((w, d)), resident((d, d))],
        out_specs=rows(d, 0),
        out_shape=jax.ShapeDtypeStruct((n, d), F32),
        compiler_params=_params("parallel"),
        name="merge",
    )(x, y_rw, y_ml, proj, proj, w_rw, w_ml, w_out)


def _each(f, *lists):
    return [f(*xs) for xs in zip(*lists)]


def _rwkv_pairs(r, k, v, a, lw, g, s0, kkw, kaw, rkw, lnw, lnb, cst, result):
    L = r[0].shape[0]
    tri, seg, head0, strict, incl, col_head0, eye2, blockdiag = cst
    bf = lambda x: x.astype(BF16)

    kk_ = _each(lambda k_, w_: k_ * w_, k, kkw)
    kn = _each(lambda x: _segsum(x * x, seg), kk_)
    cum = _each(lambda x: _cumsum_rows(tri, x), lw)
    yield
    kk = _each(lambda x, n_: x / jnp.maximum(jnp.sqrt(n_), KK_NORM_FLOOR), kk_, kn)
    k2 = _each(lambda k_, a_, w_: k_ * (1.0 + (a_ - 1.0) * w_), k, a, kaw)
    kka = _each(lambda x, a_: x * a_, kk, a)
    c_end = [c[L - 1:L, :] for c in cum]
    e_inv = [jnp.exp(-c) for c in cum]
    e_rem = _each(lambda c, ce: jnp.exp(ce - c), cum, c_end)
    at = _each(lambda x, c, l_: bf(-x * jnp.exp(c - l_)), kk, cum, lw)
    rt = _each(lambda x, c: bf(x * jnp.exp(c)), r, cum)
    bt = _each(lambda x, e: bf(x * e), kka, e_inv)
    kt = _each(lambda x, e: bf(x * e), k2, e_inv)
    vb = [bf(x) for x in v]
    zero = jnp.zeros_like(vb[0])

    def by_head(x):
        return jnp.concatenate([jnp.where(head0, x, zero), jnp.where(head0, zero, x)], axis=0)

    lhs = _each(lambda x, y_: jnp.concatenate([x, y_], axis=0), at, rt)
    ab = _each(lambda l_, x: _dot_nt(l_, by_head(x)), lhs, bt)
    ak = _each(lambda l_, x: _dot_nt(l_, by_head(x)), lhs, kt)
    yield
    a_rb = [jnp.where(incl, x[L:], 0.0) for x in ab]
    a_ak = [jnp.where(strict, x[:L], 0.0) for x in ak]
    a_rk = [jnp.where(incl, x[L:], 0.0) for x in ak]

    def blockdiag2(x):
        n_cat = jnp.where(strict, x[:L], 0.0)
        return jnp.concatenate([jnp.where(col_head0, n_cat, 0.0), jnp.where(col_head0, 0.0, n_cat)], axis=0)

    p = [blockdiag2(x) for x in ab]
    t_inv = [eye2 + x for x in p]
    sb = [bf(x) for x in s0]
    vs = [by_head(x) for x in vb]
    x0 = _each(lambda a_, s_, m_, v_: _dot_nt(a_, s_) + _dot(bf(m_), v_), at, sb, a_ak, vs)
    for _ in range(L.bit_length() - 2):
        p = [_dot(bf(x), bf(x)) for x in p]
        yield
        t_inv = _each(lambda t_, x: t_ + _dot(bf(t_), bf(x)), t_inv, p)
    yield
    us = _each(lambda t_, x: _dot(bf(t_), by_head(bf(x))), t_inv, x0)
    yield
    y = _each(lambda r_, s_, m1, m2, u_, v_: _dot_nt(r_, s_) + _dot(
        bf(jnp.concatenate([m1, m2], axis=1)), jnp.concatenate([bf(u_), v_], axis=0)),
        rt, sb, a_rb, a_rk, us, vs)
    upd = _each(lambda u_, v_, x1, x2, e: _dot_tn(
        jnp.concatenate([bf(u_[:L] + u_[L:]), v_], axis=0),
        jnp.concatenate([bf(x1 * e), bf(x2 * e)], axis=0)), us, vb, kka, k2, e_rem)
    yield
    s_new = _each(lambda s_, ce, u_: s_ * jnp.exp(ce) + jnp.where(blockdiag, u_, 0.0), s0, c_end, upd)

    hd = float(LANE // 2)
    mean = [_segsum(x, seg) / hd for x in y]
    bsum = _each(lambda r_, k_, w_: _segsum(r_ * k_ * w_, seg), r, k2, rkw)
    yield
    yc = _each(lambda x, m_: x - m_, y, mean)
    var = [_segsum(x * x, seg) / hd for x in yc]
    yield
    out = _each(lambda x, v_, w_, b_, bs, vv, g_: (x * lax.rsqrt(v_ + RW_GN_EPS) * w_ + b_ + bs * vv) * g_,
                yc, var, lnw, lnb, bsum, v, g)
    result.extend([out, s_new])


def _rwkv_steps(c, seqs, prm, *, rows, lead, width):
    (mu_rkv_ref, mu_misc_ref, w0_ref, w2_ref, a0_ref, a2_ref, g2_ref,
     kk_ref, ka_ref, rk_ref, lnw_ref, lnb_ref, tri_ref, seg_ref, eye_ref) = prm
    L = CHUNK
    row = _iota((L, 1), 0)
    first = jnp.where(c == 0, lead, 0)
    valid = row >= first
    for (rkv_ref, misc_ref, _, _, carry_rkv, carry_misc, r_s, k_s, v_s, a_s, lw_s, g_s) in seqs:
        p = rkv_ref[...]
        pm = misc_ref[:, :3 * LANE]
        if rows < L:
            p = jnp.concatenate([jnp.zeros((L - rows, p.shape[1]), F32), p], axis=0)
            pm = jnp.concatenate([jnp.zeros((L - rows, pm.shape[1]), F32), pm], axis=0)
        p = jnp.where(valid, p, 0.0)
        pm = jnp.where(valid, pm, 0.0)

        def shift_mix(cur, carry_ref, mu):
            prev = jnp.where(row == first, carry_ref[...], pltpu.roll(cur, 1, axis=0))
            carry_ref[...] = cur[L - 1:L, :]
            return jnp.where(valid, cur + mu * (prev - cur), 0.0)

        u = shift_mix(p, carry_rkv, mu_rkv_ref[...])
        um = shift_mix(pm, carry_misc, mu_misc_ref[...])

        lora = um[:, :LANE]
        wl = w0_ref[...] + _dot(jnp.tanh(lora).astype(BF16), w2_ref[...])
        lw_s[...] = jnp.where(valid, -DECAY_SCALE * _sigmoid(wl), 0.0)
        a_s[...] = _sigmoid(a0_ref[...] + _dot(lora.astype(BF16), a2_ref[...]))
        g_s[...] = _dot(_sigmoid(um[:, LANE:]).astype(BF16), g2_ref[...])
        r_s[...] = u[:, :width]
        k_s[...] = u[:, width:2 * width]
        v_s[...] = u[:, 2 * width:]
        yield

    lane = _iota((1, LANE), 1)
    half = LANE // 2
    col2 = _iota((L, 2 * L), 1)
    t_i = _iota((L, 2 * L), 0)
    s_i = col2 & (L - 1)
    cst = (
        tri_ref[...], seg_ref[...],
        lane < half,
        s_i < t_i, s_i <= t_i,
        col2 < L,
        eye_ref[...],
        (_iota((LANE, LANE), 0) < half) == (_iota((LANE, LANE), 1) < half),
    )
    sls = [slice(j * LANE, (j + 1) * LANE) for j in range(width // LANE)]
    units = [(seq, j, sl) for seq in seqs for j, sl in enumerate(sls)]
    from_seq = lambda k: [seq[k][:, sl] for seq, _, sl in units]
    from_prm = lambda ref: [ref[:, sl] for _, _, sl in units]
    result = []
    yield from _rwkv_pairs(from_seq(6), from_seq(7), from_seq(8), from_seq(9), from_seq(10), from_seq(11),
                           [seq[3][0, j] for seq, j, _ in units],
                           from_prm(kk_ref), from_prm(ka_ref), from_prm(rk_ref), from_prm(lnw_ref),
                           from_prm(lnb_ref), cst, result)
    outs, s_new = result
    for (seq, j, sl), out, s_end in zip(units, outs, s_new):
        seq[3][0, j] = s_end
        seq[2][:, sl] = out[L - rows:, :].astype(seq[2].dtype)


def _mlstm_steps(c, refs, *, rows, lead, heads, hd):
    (qk_ref, v_ref, o_ref, misc_ref, conv0_ref, cw_ref, cb_ref, ifb_ref, nw_ref, tri_ref,
     y_ref, c_ref, n_ref, m_ref, ext) = refs
    L = CHUNK
    width = heads * hd

    def chunk_rows(x):
        if rows < L:
            return jnp.concatenate([jnp.zeros((L - rows, x.shape[1]), F32), x], axis=0)
        return x

    row = _iota((L, 1), 0)
    valid = row >= jnp.where(c == 0, lead, 0)

    carried = jnp.concatenate([jnp.zeros((lead - SUBLANE, 2 * width), F32), conv0_ref[0],
                               jnp.zeros((L - lead, 2 * width), F32)], axis=0)
    use_carried = (c == 0) & (row >= lead - SUBLANE) & (row < lead)
    ext[SUBLANE:SUBLANE + L, :] = jnp.where(valid, chunk_rows(qk_ref[...]), jnp.where(use_carried, carried, 0.0))
    conv = cb_ref[...]
    taps = cw_ref.shape[0]
    for j in range(taps):
        back = taps - 1 - j
        conv = conv + cw_ref[j:j + 1, :] * ext[SUBLANE - back:SUBLANE - back + L, :]
    ext[0:SUBLANE, :] = ext[L:L + SUBLANE, :]
    qk = jnp.where(valid, conv * _sigmoid(conv), 0.0)
    v = jnp.where(valid, chunk_rows(v_ref[...]), 0.0)
    og = _sigmoid(chunk_rows(o_ref[...]))
    yield

    lane = _iota((1, LANE), 1)
    is_i = lane < heads
    z = chunk_rows(misc_ref[:, MISC_IF:MISC_IF + LANE]) + ifb_ref[...]
    logsig = jnp.minimum(z, 0.0) - jnp.log1p(jnp.exp(-jnp.abs(z)))
    gates = jnp.where(valid, jnp.where(is_i, z, logsig), jnp.where(is_i, NEG, 0.0))
    cum = _cumsum_rows(tri_ref[...], jnp.where(is_i, 0.0, gates))
    pad = jnp.zeros((LANE - L, LANE), F32)
    gates_t = jnp.concatenate([gates, pad], axis=0).T
    cum_t = jnp.concatenate([cum, pad], axis=0).T
    causal = _iota((L, L), 1) <= _iota((L, L), 0)
    m_all = m_ref[0]
    yield

    for h in range(heads):
        sl = slice(h * hd, (h + 1) * hd)
        q = qk[:, sl]
        kx = qk[:, width + h * hd:width + (h + 1) * hd] * (hd ** -0.5)
        vh = v[:, sl]
        b_col = cum[:, heads + h:heads + h + 1]
        i_col = gates[:, h:h + 1]
        b_row = cum_t[heads + h:heads + h + 1, :L]
        i_row = gates_t[h:h + 1, :L]
        b_end = b_col[L - 1:L, :]
        m_prev = jnp.sum(jnp.where(lane == h, m_all, 0.0), axis=1, keepdims=True)

        log_inter = b_col + m_prev
        dmat = jnp.where(causal, b_col - b_row + i_row, NEG)
        m_q = jnp.maximum(log_inter, jnp.max(dmat, axis=-1, keepdims=True))
        w_inter = jnp.exp(log_inter - m_q)
        qb = q.astype(BF16)
        s = _dot_nt(qb, kx.astype(BF16)) * jnp.exp(dmat - m_q)
        c_h = c_ref[0, h]
        n_h = n_ref[0, h:h + 1, :]
        yield
        num = w_inter * _dot(qb, c_h.astype(BF16)) + _dot(s.astype(BF16), vh.astype(BF16))
        den = w_inter * jnp.sum(q * n_h, axis=-1, keepdims=True) + jnp.sum(s, axis=-1, keepdims=True)
        hcell = num / jnp.maximum(jnp.abs(den), jnp.exp(-m_q))
        yield

        g_col = b_end - b_col + i_col
        m_new = jnp.maximum(b_end + m_prev, jnp.max(g_col, axis=0, keepdims=True))
        a_st = jnp.exp(b_end + m_prev - m_new)
        wkk = jnp.exp(g_col - m_new) * kx
        c_ref[0, h] = a_st * c_h + _dot_tn(wkk.astype(BF16), vh.astype(BF16))
        n_ref[0, h:h + 1, :] = a_st * n_h + jnp.sum(wkk, axis=0, keepdims=True)
        m_all = jnp.where(lane == h, m_new, m_all)

        mu = jnp.mean(hcell, axis=-1, keepdims=True)
        hc = hcell - mu
        var = jnp.mean(hc * hc, axis=-1, keepdims=True)
        yh = hc * lax.rsqrt(var + ML_LN_EPS) * nw_ref[:, sl] * og[:, sl]
        y_ref[:, sl] = yh[L - rows:, :].astype(y_ref.dtype)
        yield

    m_ref[0] = m_all


N_COLS, N_STATE, N_RW_PRM, N_ML_PRM = 6, 7, 15, 5


def _mixers_kernel(*refs, n_seq, aliased, rows, lead, width, heads, hd):
    pos = 0

    def take(n):
        nonlocal pos
        pos += n
        return refs[pos - n:pos]

    rkv_ref, misc_ref, qk_ref, v_ref, o_ref, misc2_ref = take(N_COLS)
    sh_rkv_ref, sh_misc_ref, s0_ref, conv0_ref, c0_ref, n0_ref, m0_ref = take(N_STATE)
    rw_prm, ml_prm = take(N_RW_PRM), take(N_ML_PRM)
    take(2 if aliased else 0)
    y_rw_ref, y_ml_ref, s_ref, c_ref, n_ref, m_ref = take(6)
    carry_rkv, carry_misc, r_s, k_s, v_s, a_s, lw_s, g_s, ext = take(9)
    c = pl.program_id(1)

    @pl.when(c == 0)
    def _():
        carry_rkv[...] = sh_rkv_ref[:, 0]
        carry_misc[...] = sh_misc_ref[:, 0]
        s_ref[...] = s0_ref[...]
        c_ref[...] = c0_ref[...]
        n_ref[...] = n0_ref[...]
        m_ref[...] = m0_ref[...]
        ext[:, 0:SUBLANE, :] = jnp.zeros((n_seq, SUBLANE, ext.shape[2]), F32)

    parts = [
        _mlstm_steps(c, tuple(ref.at[q] for ref in (qk_ref, v_ref, o_ref, misc2_ref, conv0_ref)) + tuple(ml_prm)
                     + tuple(ref.at[q] for ref in (y_ml_ref, c_ref, n_ref, m_ref, ext)),
                     rows=rows, lead=lead, heads=heads, hd=hd)
        for q in range(n_seq)
    ]
    parts.append(_rwkv_steps(
        c, [tuple(ref.at[q] for ref in (rkv_ref, misc_ref, y_rw_ref, s_ref, carry_rkv, carry_misc,
                                        r_s, k_s, v_s, a_s, lw_s, g_s)) for q in range(n_seq)],
        rw_prm, rows=rows, lead=lead, width=width))
    while parts:
        for part in list(parts):
            if next(part, StopIteration) is StopIteration:
                parts.remove(part)


def _mixers(proj, proj_misc, y_prev, rw_state, ml_state, rw_prm, ml_prm, *,
            n_seq, batch, n_chunks, rows, seq_row, lead, width, heads, hd, d_model):
    g_rows = proj.shape[1]
    n_pairs = width // LANE
    per = batch // n_seq
    rkv_blk = (2 * d_model + 2 * width) // (3 * width)
    qk_blk = (2 * d_model) // (2 * width)
    v_blk = (2 * d_model + 2 * width + 3 * width) // width
    aliased = y_prev is not None
    kern = functools.partial(_mixers_kernel, n_seq=n_seq, aliased=aliased, rows=rows, lead=lead, width=width,
                             heads=heads, hd=hd)
    full = lambda shape: pl.BlockSpec(shape, lambda p, c: (0,) * len(shape))
    state = lambda *shape: pl.BlockSpec((n_seq, 1) + shape, lambda p, c: (0, p) + (0,) * len(shape))
    cols = lambda w_, blk: pl.BlockSpec((n_seq, rows, w_), lambda p, c: (0, seq_row(p, c), blk))
    col_specs = [cols(3 * width, rkv_blk), cols(MISC_COLS, 0), cols(2 * width, qk_blk),
                 cols(width, v_blk), cols(width, v_blk + 1), cols(MISC_COLS, 0)]
    s_spec, c_spec, n_spec, m_spec = state(n_pairs, LANE, LANE), state(heads, hd, hd), state(heads, hd), state(1, LANE)
    state_specs = [state(1, 3 * width), state(1, 3 * LANE), s_spec, state(SUBLANE, 2 * width), c_spec, n_spec, m_spec]
    rw_specs = [
        full((1, 3 * width)), full((1, 3 * LANE)),
        full((1, width)), full((LANE, width)), full((1, width)), full((LANE, width)), full((2 * LANE, width)),
        full((1, width)), full((1, width)), full((1, width)), full((1, width)), full((1, width)),
        full((CHUNK, CHUNK)), full((LANE, LANE)), full((2 * CHUNK, 2 * CHUNK)),
    ]
    ml_specs = [full(ml_prm[0].shape), full((1, 2 * width)), full((1, LANE)), full((1, width)), full((CHUNK, CHUNK))]
    assert (len(col_specs), len(state_specs), len(rw_specs), len(ml_specs)) == (N_COLS, N_STATE, N_RW_PRM, N_ML_PRM)
    assert lead >= SUBLANE and batch % n_seq == 0
    in_specs = col_specs + state_specs + rw_specs + ml_specs + [pl.BlockSpec(memory_space=pl.ANY)] * (2 * aliased)
    y_spec = cols(width, 0)
    out_specs = [y_spec, y_spec, s_spec, c_spec, n_spec, m_spec]
    sds = jax.ShapeDtypeStruct
    st = lambda *shape: sds((n_seq, per) + shape, F32)
    out_shape = [sds((n_seq, g_rows, width), BF16)] * 2 + [st(n_pairs, LANE, LANE), st(heads, hd, hd),
                                                           st(heads, hd), st(1, LANE)]
    vmem = pltpu.VMEM
    scratch = [vmem((n_seq, 1, 3 * width), F32), vmem((n_seq, 1, 3 * LANE), F32)]
    scratch += [vmem((n_seq, CHUNK, width), F32) for _ in range(6)]
    scratch += [vmem((n_seq, CHUNK + SUBLANE, 2 * width), F32)]
    n_in = len(in_specs)
    shift_rkv, shift_misc, s0 = rw_state
    conv0, c0, n0, m0 = ml_state
    outs = pl.pallas_call(
        kern, grid=(per, n_chunks), in_specs=in_specs, out_specs=out_specs, out_shape=out_shape,
        scratch_shapes=scratch, input_output_aliases={n_in - 2: 0, n_in - 1: 1} if aliased else {},
        compiler_params=_params("arbitrary", "arbitrary"), name="mixers",
    )(proj, proj_misc, proj, proj, proj, proj_misc, shift_rkv, shift_misc, s0, conv0, c0, n0, m0,
      *rw_prm, *ml_prm, *(y_prev or ()))
    return outs[:2], outs[2:]


def _pad_rows(w, rows):
    return jnp.pad(w, ((0, rows - w.shape[0]), (0, 0)))


def _swiglu_weights(w_gate, w_up, w_down):
    f = w_gate.shape[1]
    fp = -(-f // FF_TILE) * FF_TILE
    pad_cols = lambda w: jnp.pad(w.astype(BF16), ((0, 0), (0, fp - f)))
    return pad_cols(w_gate), pad_cols(w_up), _pad_rows(w_down.astype(BF16), fp)


def kernel(x_prompt, x_sample, state_rwkv_shift, state_rwkv_wkv, state_mlstm_conv, state_mlstm_C,
           state_mlstm_n, state_mlstm_m, meta_tokens, ffn1_norm, ffn1_w_gate, ffn1_w_up, ffn1_w_down,
           mix_norm, w_in, rw_mu, rw_w0, rw_w2, rw_a0, rw_a2, rw_g2, rw_kk, rw_ka, rw_rk, rw_ln_w, rw_ln_b,
           ml_conv_w, ml_conv_b, ml_i_b, ml_f_b, ml_norm_w, w_br_rw, w_br_ml, w_out,
           ffn2_norm, ffn2_w_gate, ffn2_w_up, ffn2_w_down, final_norm):
    assert ffn1_norm.shape[0] == 1, "single-layer trunk"
    B, T, D = x_prompt.shape
    Bs, Ts, _ = x_sample.shape
    n_meta = meta_tokens.shape[0]
    W = rw_w0.shape[-1]
    dl, al, gl = rw_w2.shape[1], rw_a2.shape[1], rw_g2.shape[1]
    rw_heads, rw_hd = rw_rk.shape[1], rw_rk.shape[2]
    Wm = ml_norm_w.shape[-1]
    H = ml_i_b.shape[-1]
    hd = Wm // H
    K = ml_conv_w.shape[1]
    L = CHUNK
    assert rw_hd == LANE // 2 and dl == LANE // 2 and al == LANE // 2 and gl <= 2 * LANE
    assert K == 4 and hd % LANE == 0 and 2 * H <= LANE and W == Wm and 2 * W == D
    assert Ts <= L and Ts % (2 * SUBLANE) == 0 and (11 * D // 2) % PROJ_COL_TILE == 0

    S = SEQS_PER_STEP
    lead = (-n_meta) % L
    head_rows = lead + n_meta
    n_head = head_rows // L
    n_chunks = n_head + T // L
    lead_s = L - Ts
    pb, ps = B // S, Bs // S
    main_rows, extra_rows = pb * T, pb * head_rows + ps * Ts
    G = main_rows + extra_rows
    N = S * G
    assert B % S == 0 and Bs % S == 0 and T % L == 0 and T % ROW_TILE == 0
    assert extra_rows % ROW_TILE == 0 and N % PROJ_ROW_TILE == 0 and N % MERGE_ROW_TILE == 0
    head = jnp.concatenate([jnp.zeros((lead, D), F32), meta_tokens.astype(F32)], axis=0)
    x_extra = jnp.concatenate([jnp.broadcast_to(head[None, None], (S, pb, head_rows, D)).reshape(S, -1, D),
                               x_sample.reshape(S, ps * Ts, D)], axis=1).reshape(S * extra_rows, D)

    ffn1_w = _swiglu_weights(ffn1_w_gate[0], ffn1_w_up[0], ffn1_w_down[0])
    ffn2_w = _swiglu_weights(ffn2_w_gate[0], ffn2_w_up[0], ffn2_w_down[0])
    wi = w_in[0]
    o_lora = 3 * W
    o_qk = o_lora + dl + al + gl
    o_v = o_qk + 2 * Wm
    o_o = o_v + Wm
    o_i = o_o + Wm
    o_gate = o_i + 2 * H
    zc = lambda n: jnp.zeros((D, n), wi.dtype)
    misc_w = jnp.concatenate([wi[:, o_lora:o_qk], zc(MISC_IF - (dl + al + gl)),
                              wi[:, o_i:o_gate], zc(MISC_COLS - MISC_IF - 2 * H)], axis=1)
    w_proj = jnp.concatenate([wi[:, o_gate:], wi[:, o_qk:o_v], wi[:, :o_lora], wi[:, o_v:o_i]],
                             axis=1).astype(BF16)

    def misc_vec(v_lora, fill=0.0):
        return jnp.pad(v_lora, [(0, 0)] * (v_lora.ndim - 1) + [(0, 3 * LANE - v_lora.shape[-1])],
                       constant_values=fill)

    row = lambda v: v.reshape(1, -1).astype(F32)
    rw_prm = (
        row(rw_mu[0, :o_lora]), misc_vec(row(rw_mu[0, o_lora:])),
        row(rw_w0[0]), _pad_rows(rw_w2[0], LANE).astype(BF16),
        row(rw_a0[0]), jnp.concatenate([jnp.zeros((dl, W), F32), rw_a2[0]], axis=0).astype(BF16),
        _pad_rows(rw_g2[0], 2 * LANE).astype(BF16),
        row(rw_kk[0]), row(rw_ka[0]), row(rw_rk[0]), row(rw_ln_w[0]), row(rw_ln_b[0]),
    )
    ifb = jnp.pad(jnp.concatenate([ml_i_b[0], ml_f_b[0]]).reshape(1, 2 * H), ((0, 0), (0, LANE - 2 * H)))
    tri = jnp.tril(jnp.ones((L, L), BF16))
    same_head = jnp.kron(jnp.eye(LANE // rw_hd, dtype=BF16), jnp.ones((rw_hd, rw_hd), BF16))
    rw_prm = rw_prm + (tri, same_head, jnp.eye(2 * L, dtype=F32))
    ml_prm = (ml_conv_w[0].astype(F32), row(ml_conv_b[0]), ifb.astype(F32), row(ml_norm_w[0]), tri)

    def rw_state_in(shift, wkv):
        b = shift.shape[0]
        s = wkv.reshape(b, rw_heads // 2, 2, rw_hd, rw_hd)
        z = jnp.zeros_like(s[:, :, 0])
        bd = jnp.concatenate([jnp.concatenate([s[:, :, 0], z], axis=-1),
                              jnp.concatenate([z, s[:, :, 1]], axis=-1)], axis=-2)
        return shift[:, None, :o_lora], misc_vec(shift[:, None, o_lora:]), bd

    def rw_state_out(bd):
        h = rw_hd
        return jnp.stack([bd[:, :, :h, :h], bd[:, :, h:, h:]], axis=2).reshape(bd.shape[0], rw_heads, h, h)

    def conv_in(buf):
        return jnp.pad(buf, ((0, 0), (SUBLANE - (K - 1), 0), (0, 0)))

    def m_in(m):
        return jnp.pad(m, ((0, 0), (0, LANE - H)))[:, None, :]

    x1 = _ffn((x_prompt, x_extra), row(ffn1_norm[0]), *ffn1_w, row(final_norm),
              groups=S, split_out=None, final_norm=False)
    proj, proj_misc = _proj(x1, row(mix_norm[0]), w_proj, misc_w.astype(BF16))
    proj_g, proj_misc_g = proj.reshape(S, G, -1), proj_misc.reshape(S, G, -1)

    zeros = lambda *s: jnp.zeros(s, F32)
    main_chunks = T // L

    def prompt_row(p, c):
        return jnp.where(c < n_head, main_rows // L + p * n_head + c, p * main_chunks + c - n_head)

    sample_base = main_rows + pb * head_rows
    seqs = (
        dict(batch=B, n_chunks=n_chunks, rows=L, seq_row=prompt_row, lead=lead),
        dict(batch=Bs, n_chunks=1, rows=Ts, seq_row=lambda p, c: sample_base // Ts + p, lead=lead_s),
    )
    grouped = lambda a: a.reshape((S, a.shape[0] // S) + a.shape[1:])
    flat = lambda a: a.reshape((a.shape[0] * a.shape[1],) + a.shape[2:])
    rw_states = (
        rw_state_in(zeros(B, o_qk), zeros(B, rw_heads, rw_hd, rw_hd)),
        rw_state_in(state_rwkv_shift[0], state_rwkv_wkv[0]),
    )
    ml_states = (
        (conv_in(zeros(B, K - 1, 2 * Wm)), zeros(B, H, hd, hd), zeros(B, H, hd), m_in(zeros(B, H))),
        (conv_in(state_mlstm_conv[0]), state_mlstm_C[0], state_mlstm_n[0], m_in(state_mlstm_m[0])),
    )
    ys = None
    rw_out, ml_out = [], []
    for seq, rws, mls in zip(seqs, rw_states, ml_states):
        ys, (s_end, c_end, n_end, m_end) = _mixers(
            proj_g, proj_misc_g, ys, tuple(map(grouped, rws)), tuple(map(grouped, mls)), rw_prm, ml_prm,
            n_seq=S, width=W, heads=H, hd=hd, d_model=D, **seq)
        rw_out.append(rw_state_out(flat(s_end)))
        ml_out.append((flat(c_end), flat(n_end), flat(m_end)[:, 0, :H]))

    x2 = _merge(x1, ys[0].reshape(N, W), ys[1].reshape(N, Wm), proj, w_br_rw[0].astype(BF16),
                w_br_ml[0].astype(BF16), w_out[0].astype(BF16))
    y_prompt, y_extra = _ffn((x2,), row(ffn2_norm[0]), *ffn2_w, row(final_norm),
                             groups=S, split_out=(B, T), final_norm=True)

    c_rkv = 2 * D + 2 * Wm

    def seq_states(base, per_group, t_len):
        seq = jnp.arange(S * per_group)
        end = (seq // per_group) * G + base + (seq % per_group + 1) * t_len
        last = end[:, None] - (K - 1) + jnp.arange(K - 1)[None, :]
        tail = proj[last.reshape(-1)].reshape(S * per_group, K - 1, -1)
        tail_misc = proj_misc[last.reshape(-1)].reshape(S * per_group, K - 1, -1)
        shift = jnp.concatenate([tail[:, -1, c_rkv:c_rkv + 3 * W], tail_misc[:, -1, :dl + al + gl]], axis=-1)
        return shift, tail[:, :, 2 * D:2 * D + 2 * Wm]

    p_shift, p_conv = seq_states(0, pb, T)
    s_shift, s_conv = seq_states(sample_base, ps, Ts)
    y_sample = y_extra.reshape(S, extra_rows, D)[:, pb * head_rows:].reshape(Bs, Ts, D)
    d1 = lambda a: a[None]
    return (y_prompt, y_sample,
            d1(p_shift), d1(rw_out[0]), d1(p_conv), d1(ml_out[0][0]), d1(ml_out[0][1]), d1(ml_out[0][2]),
            d1(s_shift), d1(rw_out[1]), d1(s_conv), d1(ml_out[1][0]), d1(ml_out[1][1]), d1(ml_out[1][2]))
```

```python
import functools

import jax
import jax.numpy as jnp
from jax import lax
from jax.experimental import pallas as pl
from jax.experimental.pallas import tpu as pltpu

F32 = jnp.float32
BF16 = jnp.bfloat16

LANE = 128
SUBLANE = 8
VMEM_LIMIT_BYTES = 56 * 1024 * 1024
CHUNK = 64
ROW_TILE = 512
PROJ_ROW_TILE = 1024
FF_TILE = 512
PROJ_COL_TILE = 1024
MERGE_ROW_TILE = 256
SEQS_PER_STEP = 2
MISC_COLS = 512
MISC_IF = 384
RMS_EPS = 1e-6
RW_GN_EPS = 64e-5
ML_LN_EPS = 1e-5
KK_NORM_FLOOR = 1e-12
NEG = -1e30
DECAY_SCALE = 0.6065306597126334


def _dot(a, b):
    return jnp.dot(a, b, preferred_element_type=F32)


def _dot_nt(a, b):
    return lax.dot_general(a, b, (((1,), (1,)), ((), ())), preferred_element_type=F32)


def _dot_tn(a, b):
    return lax.dot_general(a, b, (((0,), (0,)), ((), ())), preferred_element_type=F32)


def _split3(x):
    h1 = x.astype(BF16)
    r1 = x - h1.astype(F32)
    h2 = r1.astype(BF16)
    r2 = r1 - h2.astype(F32)
    return h1, h2, r2.astype(BF16)


def _cumsum_rows(tri, x):
    n = x.shape[1]
    y = _dot(tri, jnp.concatenate(_split3(x), axis=1))
    return y[:, :n] + y[:, n:2 * n] + y[:, 2 * n:]


def _segsum(x, seg):
    rows = x.shape[0]
    h1, h2, _ = _split3(x)
    y = _dot(jnp.concatenate([h1, h2], axis=0), seg)
    return y[:rows] + y[rows:]


def _sigmoid(z):
    return jax.nn.sigmoid(z)


def _rmsnorm(x, g):
    return x * lax.rsqrt(jnp.mean(x * x, axis=-1, keepdims=True) + RMS_EPS) * g


def _iota(shape, dim):
    return lax.broadcasted_iota(jnp.int32, shape, dim)


def _params(*semantics):
    return pltpu.CompilerParams(dimension_semantics=semantics, vmem_limit_bytes=VMEM_LIMIT_BYTES)


def _ffn_kernel(*refs, n_f, tiles, split_in, split_out, final_norm):
    n_x = 2 if split_in else 1
    n_o = 2 if split_out else 1
    x_refs = refs[:n_x]
    g_ref, wg_ref, wu_ref, wd_ref, fg_ref = refs[n_x:n_x + 5]
    o_refs = refs[n_x + 5:n_x + 5 + n_o]
    xn_ref, acc_ref = refs[n_x + 5 + n_o:]
    in_main = pl.program_id(1) < tiles[1]
    j = pl.program_id(2)

    def with_x(fn):
        if split_in:
            pl.when(in_main)(lambda: fn(x_refs[0][...]))
            pl.when(jnp.logical_not(in_main))(lambda: fn(x_refs[1][...]))
        else:
            fn(x_refs[0][...])

    def start(x):
        xn_ref[...] = _rmsnorm(x, g_ref[...]).astype(BF16)

    @pl.when(j == 0)
    def _():
        acc_ref[...] = jnp.zeros_like(acc_ref)
        with_x(start)

    xn = xn_ref[...]
    gate = _dot(xn, wg_ref[...])
    h = (gate * _sigmoid(gate) * _dot(xn, wu_ref[...])).astype(BF16)
    acc_ref[...] += _dot(h, wd_ref[...])

    def finish(o_ref, x):
        y = x + 0.5 * acc_ref[...]
        if final_norm:
            y = _rmsnorm(y, fg_ref[...])
        o_ref[...] = y

    last = j == n_f - 1
    if split_out:
        pl.when(last & in_main)(lambda: finish(o_refs[0], x_refs[0][...]))
        pl.when(last & jnp.logical_not(in_main))(lambda: finish(o_refs[1], x_refs[0][...]))
    else:
        pl.when(last)(lambda: with_x(lambda x: finish(o_refs[0], x)))


def _ffn(xs, norm_g, wg, wu, wd, final_g, *, groups, split_out, final_norm):
    split_in = len(xs) == 2
    d = xs[-1].shape[-1]
    if split_in:
        b, t = xs[0].shape[:2]
        n = b * t + xs[1].shape[0]
    else:
        n = xs[0].shape[0]
        b, t = split_out if split_out else (groups, n // groups)
    n_f = wd.shape[0] // FF_TILE
    tpb = t // ROW_TILE
    mg = (b // groups) * tpb
    tg = n // ROW_TILE // groups
    eg = tg - mg
    kern = functools.partial(_ffn_kernel, n_f=n_f, tiles=(tg, mg), split_in=split_in,
                             split_out=bool(split_out), final_norm=final_norm)
    whole = pl.BlockSpec((ROW_TILE, d), lambda g, r, j: (g * tg + r, 0))

    def main_index(g, r, j):
        m = g * mg + jnp.minimum(r, mg - 1)
        return m // tpb, m % tpb, 0

    def extra_index(g, r, j):
        return g * eg + jnp.maximum(r - mg, 0), 0

    main = pl.BlockSpec((None, ROW_TILE, d), main_index)
    extra = pl.BlockSpec((ROW_TILE, d), extra_index)
    return pl.pallas_call(
        kern,
        grid=(groups, tg, n_f),
        in_specs=([main, extra] if split_in else [whole]) + [
            pl.BlockSpec((1, d), lambda g, r, j: (0, 0)),
            pl.BlockSpec((d, FF_TILE), lambda g, r, j: (0, j)),
            pl.BlockSpec((d, FF_TILE), lambda g, r, j: (0, j)),
            pl.BlockSpec((FF_TILE, d), lambda g, r, j: (j, 0)),
            pl.BlockSpec((1, d), lambda g, r, j: (0, 0)),
        ],
        out_specs=[main, extra] if split_out else whole,
        out_shape=([jax.ShapeDtypeStruct((b, t, d), F32), jax.ShapeDtypeStruct((n - b * t, d), F32)]
                   if split_out else jax.ShapeDtypeStruct((n, d), F32)),
        scratch_shapes=[pltpu.VMEM((ROW_TILE, d), BF16), pltpu.VMEM((ROW_TILE, d), F32)],
        compiler_params=_params("arbitrary", "arbitrary", "arbitrary"),
        name="ffn_final" if final_norm else "ffn",
    )(*xs, norm_g, wg, wu, wd, final_g)


def _proj_kernel(x_ref, g_ref, w_ref, wm_ref, o_ref, om_ref, xn_ref):
    @pl.when(pl.program_id(1) == 0)
    def _():
        xn_ref[...] = _rmsnorm(x_ref[...], g_ref[...]).astype(BF16)
        om_ref[...] = _dot(xn_ref[...], wm_ref[...])

    o_ref[...] = _dot(xn_ref[...], w_ref[...])


def _proj(x, norm_g, w, w_misc):
    n, d = x.shape
    cols = w.shape[1]
    tm, tn = PROJ_ROW_TILE, PROJ_COL_TILE
    return pl.pallas_call(
        _proj_kernel,
        grid=(n // tm, cols // tn),
        in_specs=[
            pl.BlockSpec((tm, d), lambda i, j: (i, 0)),
            pl.BlockSpec((1, d), lambda i, j: (0, 0)),
            pl.BlockSpec((d, tn), lambda i, j: (0, j)),
            pl.BlockSpec((d, MISC_COLS), lambda i, j: (0, 0)),
        ],
        out_specs=[pl.BlockSpec((tm, tn), lambda i, j: (i, j)),
                   pl.BlockSpec((tm, MISC_COLS), lambda i, j: (i, 0))],
        out_shape=[jax.ShapeDtypeStruct((n, cols), F32), jax.ShapeDtypeStruct((n, MISC_COLS), F32)],
        scratch_shapes=[pltpu.VMEM((tm, d), BF16)],
        compiler_params=_params("parallel", "arbitrary"),
        name="in_proj",
    )(x, norm_g, w, w_misc)


def _merge_kernel(x_ref, yrw_ref, yml_ref, ga_ref, gb_ref, wrw_ref, wml_ref, wo_ref, o_ref):
    merged = (_sigmoid(ga_ref[...]) * _dot(yrw_ref[...], wrw_ref[...])
              + _sigmoid(gb_ref[...]) * _dot(yml_ref[...], wml_ref[...]))
    o_ref[...] = x_ref[...] + _dot(merged.astype(BF16), wo_ref[...])


def _merge(x, y_rw, y_ml, proj, w_rw, w_ml, w_out):
    n, d = x.shape
    w = y_rw.shape[1]
    tm = MERGE_ROW_TILE
    rows = lambda width, blk: pl.BlockSpec((tm, width), lambda i: (i, blk))
    resident = lambda shape: pl.BlockSpec(shape, lambda i: (0, 0), pipeline_mode=pl.Buffered(1))
    return pl.pallas_call(
        _merge_kernel,
        grid=(n // tm,),
        in_specs=[rows(d, 0), rows(w, 0), rows(w, 0), rows(d, 0), rows(d, 1),
                  resident((w, d)), resident((w, d)), resident((d, d))],
        out_specs=rows(d, 0),
        out_shape=jax.ShapeDtypeStruct((n, d), F32),
        compiler_params=_params("parallel"),
        name="merge",
    )(x, y_rw, y_ml, proj, proj, w_rw, w_ml, w_out)


def _each(f, *lists):
    return [f(*xs) for xs in zip(*lists)]


def _rwkv_pairs(r, k, v, a, lw, g, s0, kkw, kaw, rkw, lnw, lnb, cst, result):
    L = r[0].shape[0]
    tri, seg, head0, strict, incl, col_head0, eye2, blockdiag = cst
    bf = lambda x: x.astype(BF16)

    kk_ = _each(lambda k_, w_: k_ * w_, k, kkw)
    kn = _each(lambda x: _segsum(x * x, seg), kk_)
    cum = _each(lambda x: _cumsum_rows(tri, x), lw)
    yield
    kk = _each(lambda x, n_: x / jnp.maximum(jnp.sqrt(n_), KK_NORM_FLOOR), kk_, kn)
    k2 = _each(lambda k_, a_, w_: k_ * (1.0 + (a_ - 1.0) * w_), k, a, kaw)
    kka = _each(lambda x, a_: x * a_, kk, a)
    c_end = [c[L - 1:L, :] for c in cum]
    e_inv = [jnp.exp(-c) for c in cum]
    e_rem = _each(lambda c, ce: jnp.exp(ce - c), cum, c_end)
    at = _each(lambda x, c, l_: bf(-x * jnp.exp(c - l_)), kk, cum, lw)
    rt = _each(lambda x, c: bf(x * jnp.exp(c)), r, cum)
    bt = _each(lambda x, e: bf(x * e), kka, e_inv)
    kt = _each(lambda x, e: bf(x * e), k2, e_inv)
    vb = [bf(x) for x in v]
    zero = jnp.zeros_like(vb[0])

    def by_head(x):
        return jnp.concatenate([jnp.where(head0, x, zero), jnp.where(head0, zero, x)], axis=0)

    lhs = _each(lambda x, y_: jnp.concatenate([x, y_], axis=0), at, rt)
    ab = _each(lambda l_, x: _dot_nt(l_, by_head(x)), lhs, bt)
    ak = _each(lambda l_, x: _dot_nt(l_, by_head(x)), lhs, kt)
    yield
    a_rb = [jnp.where(incl, x[L:], 0.0) for x in ab]
    a_ak = [jnp.where(strict, x[:L], 0.0) for x in ak]
    a_rk = [jnp.where(incl, x[L:], 0.0) for x in ak]

    def blockdiag2(x):
        n_cat = jnp.where(strict, x[:L], 0.0)
        return jnp.concatenate([jnp.where(col_head0, n_cat, 0.0), jnp.where(col_head0, 0.0, n_cat)], axis=0)

    p = [blockdiag2(x) for x in ab]
    t_inv = [eye2 + x for x in p]
    sb = [bf(x) for x in s0]
    vs = [by_head(x) for x in vb]
    x0 = _each(lambda a_, s_, m_, v_: _dot_nt(a_, s_) + _dot(bf(m_), v_), at, sb, a_ak, vs)
    for _ in range(L.bit_length() - 2):
        p = [_dot(bf(x), bf(x)) for x in p]
        yield
        t_inv = _each(lambda t_, x: t_ + _dot(bf(t_), bf(x)), t_inv, p)
    yield
    us = _each(lambda t_, x: _dot(bf(t_), by_head(bf(x))), t_inv, x0)
    yield
    y = _each(lambda r_, s_, m1, m2, u_, v_: _dot_nt(r_, s_) + _dot(
        bf(jnp.concatenate([m1, m2], axis=1)), jnp.concatenate([bf(u_), v_], axis=0)),
        rt, sb, a_rb, a_rk, us, vs)
    upd = _each(lambda u_, v_, x1, x2, e: _dot_tn(
        jnp.concatenate([bf(u_[:L] + u_[L:]), v_], axis=0),
        jnp.concatenate([bf(x1 * e), bf(x2 * e)], axis=0)), us, vb, kka, k2, e_rem)
    yield
    s_new = _each(lambda s_, ce, u_: s_ * jnp.exp(ce) + jnp.where(blockdiag, u_, 0.0), s0, c_end, upd)

    hd = float(LANE // 2)
    mean = [_segsum(x, seg) / hd for x in y]
    bsum = _each(lambda r_, k_, w_: _segsum(r_ * k_ * w_, seg), r, k2, rkw)
    yield
    yc = _each(lambda x, m_: x - m_, y, mean)
    var = [_segsum(x * x, seg) / hd for x in yc]
    yield
    out = _each(lambda x, v_, w_, b_, bs, vv, g_: (x * lax.rsqrt(v_ + RW_GN_EPS) * w_ + b_ + bs * vv) * g_,
                yc, var, lnw, lnb, bsum, v, g)
    result.extend([out, s_new])


def _rwkv_steps(c, seqs, prm, *, rows, lead, width):
    (mu_rkv_ref, mu_misc_ref, w0_ref, w2_ref, a0_ref, a2_ref, g2_ref,
     kk_ref, ka_ref, rk_ref, lnw_ref, lnb_ref, tri_ref, seg_ref, eye_ref) = prm
    L = CHUNK
    row = _iota((L, 1), 0)
    first = jnp.where(c == 0, lead, 0)
    valid = row >= first
    for (rkv_ref, misc_ref, _, _, carry_rkv, carry_misc, r_s, k_s, v_s, a_s, lw_s, g_s) in seqs:
        p = rkv_ref[...]
        pm = misc_ref[:, :3 * LANE]
        if rows < L:
            p = jnp.concatenate([jnp.zeros((L - rows, p.shape[1]), F32), p], axis=0)
            pm = jnp.concatenate([jnp.zeros((L - rows, pm.shape[1]), F32), pm], axis=0)
        p = jnp.where(valid, p, 0.0)
        pm = jnp.where(valid, pm, 0.0)

        def shift_mix(cur, carry_ref, mu):
            prev = jnp.where(row == first, carry_ref[...], pltpu.roll(cur, 1, axis=0))
            carry_ref[...] = cur[L - 1:L, :]
            return jnp.where(valid, cur + mu * (prev - cur), 0.0)

        u = shift_mix(p, carry_rkv, mu_rkv_ref[...])
        um = shift_mix(pm, carry_misc, mu_misc_ref[...])

        lora = um[:, :LANE]
        wl = w0_ref[...] + _dot(jnp.tanh(lora).astype(BF16), w2_ref[...])
        lw_s[...] = jnp.where(valid, -DECAY_SCALE * _sigmoid(wl), 0.0)
        a_s[...] = _sigmoid(a0_ref[...] + _dot(lora.astype(BF16), a2_ref[...]))
        g_s[...] = _dot(_sigmoid(um[:, LANE:]).astype(BF16), g2_ref[...])
        r_s[...] = u[:, :width]
        k_s[...] = u[:, width:2 * width]
        v_s[...] = u[:, 2 * width:]
        yield

    lane = _iota((1, LANE), 1)
    half = LANE // 2
    col2 = _iota((L, 2 * L), 1)
    t_i = _iota((L, 2 * L), 0)
    s_i = col2 & (L - 1)
    cst = (
        tri_ref[...], seg_ref[...],
        lane < half,
        s_i < t_i, s_i <= t_i,
        col2 < L,
        eye_ref[...],
        (_iota((LANE, LANE), 0) < half) == (_iota((LANE, LANE), 1) < half),
    )
    sls = [slice(j * LANE, (j + 1) * LANE) for j in range(width // LANE)]
    units = [(seq, j, sl) for seq in seqs for j, sl in enumerate(sls)]
    from_seq = lambda k: [seq[k][:, sl] for seq, _, sl in units]
    from_prm = lambda ref: [ref[:, sl] for _, _, sl in units]
    result = []
    yield from _rwkv_pairs(from_seq(6), from_seq(7), from_seq(8), from_seq(9), from_seq(10), from_seq(11),
                           [seq[3][0, j] for seq, j, _ in units],
                           from_prm(kk_ref), from_prm(ka_ref), from_prm(rk_ref), from_prm(lnw_ref),
                           from_prm(lnb_ref), cst, result)
    outs, s_new = result
    for (seq, j, sl), out, s_end in zip(units, outs, s_new):
        seq[3][0, j] = s_end
        seq[2][:, sl] = out[L - rows:, :].astype(seq[2].dtype)


def _mlstm_steps(c, refs, *, rows, lead, heads, hd):
    (qk_ref, v_ref, o_ref, misc_ref, conv0_ref, cw_ref, cb_ref, ifb_ref, nw_ref, tri_ref,
     y_ref, c_ref, n_ref, m_ref, ext) = refs
    L = CHUNK
    width = heads * hd

    def chunk_rows(x):
        if rows < L:
            return jnp.concatenate([jnp.zeros((L - rows, x.shape[1]), F32), x], axis=0)
        return x

    row = _iota((L, 1), 0)
    valid = row >= jnp.where(c == 0, lead, 0)

    carried = jnp.concatenate([jnp.zeros((lead - SUBLANE, 2 * width), F32), conv0_ref[0],
                               jnp.zeros((L - lead, 2 * width), F32)], axis=0)
    use_carried = (c == 0) & (row >= lead - SUBLANE) & (row < lead)
    ext[SUBLANE:SUBLANE + L, :] = jnp.where(valid, chunk_rows(qk_ref[...]), jnp.where(use_carried, carried, 0.0))
    conv = cb_ref[...]
    taps = cw_ref.shape[0]
    for j in range(taps):
        back = taps - 1 - j
        conv = conv + cw_ref[j:j + 1, :] * ext[SUBLANE - back:SUBLANE - back + L, :]
    ext[0:SUBLANE, :] = ext[L:L + SUBLANE, :]
    qk = jnp.where(valid, conv * _sigmoid(conv), 0.0)
    v = jnp.where(valid, chunk_rows(v_ref[...]), 0.0)
    og = _sigmoid(chunk_rows(o_ref[...]))
    yield

    lane = _iota((1, LANE), 1)
    is_i = lane < heads
    z = chunk_rows(misc_ref[:, MISC_IF:MISC_IF + LANE]) + ifb_ref[...]
    logsig = jnp.minimum(z, 0.0) - jnp.log1p(jnp.exp(-jnp.abs(z)))
    gates = jnp.where(valid, jnp.where(is_i, z, logsig), jnp.where(is_i, NEG, 0.0))
    cum = _cumsum_rows(tri_ref[...], jnp.where(is_i, 0.0, gates))
    pad = jnp.zeros((LANE - L, LANE), F32)
    gates_t = jnp.concatenate([gates, pad], axis=0).T
    cum_t = jnp.concatenate([cum, pad], axis=0).T
    causal = _iota((L, L), 1) <= _iota((L, L), 0)
    m_all = m_ref[0]
    yield

    for h in range(heads):
        sl = slice(h * hd, (h + 1) * hd)
        q = qk[:, sl]
        kx = qk[:, width + h * hd:width + (h + 1) * hd] * (hd ** -0.5)
        vh = v[:, sl]
        b_col = cum[:, heads + h:heads + h + 1]
        i_col = gates[:, h:h + 1]
        b_row = cum_t[heads + h:heads + h + 1, :L]
        i_row = gates_t[h:h + 1, :L]
        b_end = b_col[L - 1:L, :]
        m_prev = jnp.sum(jnp.where(lane == h, m_all, 0.0), axis=1, keepdims=True)

        log_inter = b_col + m_prev
        dmat = jnp.where(causal, b_col - b_row + i_row, NEG)
        m_q = jnp.maximum(log_inter, jnp.max(dmat, axis=-1, keepdims=True))
        w_inter = jnp.exp(log_inter - m_q)
        qb = q.astype(BF16)
        s = _dot_nt(qb, kx.astype(BF16)) * jnp.exp(dmat - m_q)
        c_h = c_ref[0, h]
        n_h = n_ref[0, h:h + 1, :]
        yield
        num = w_inter * _dot(qb, c_h.astype(BF16)) + _dot(s.astype(BF16), vh.astype(BF16))
        den = w_inter * jnp.sum(q * n_h, axis=-1, keepdims=True) + jnp.sum(s, axis=-1, keepdims=True)
        hcell = num / jnp.maximum(jnp.abs(den), jnp.exp(-m_q))
        yield

        g_col = b_end - b_col + i_col
        m_new = jnp.maximum(b_end + m_prev, jnp.max(g_col, axis=0, keepdims=True))
        a_st = jnp.exp(b_end + m_prev - m_new)
        wkk = jnp.exp(g_col - m_new) * kx
        c_ref[0, h] = a_st * c_h + _dot_tn(wkk.astype(BF16), vh.astype(BF16))
        n_ref[0, h:h + 1, :] = a_st * n_h + jnp.sum(wkk, axis=0, keepdims=True)
        m_all = jnp.where(lane == h, m_new, m_all)

        mu = jnp.mean(hcell, axis=-1, keepdims=True)
        hc = hcell - mu
        var = jnp.mean(hc * hc, axis=-1, keepdims=True)
        yh = hc * lax.rsqrt(var + ML_LN_EPS) * nw_ref[:, sl] * og[:, sl]
        y_ref[:, sl] = yh[L - rows:, :].astype(y_ref.dtype)
        yield

    m_ref[0] = m_all


N_COLS, N_STATE, N_RW_PRM, N_ML_PRM = 6, 7, 15, 5


def _mixers_kernel(*refs, n_seq, aliased, rows, lead, width, heads, hd):
    pos = 0

    def take(n):
        nonlocal pos
        pos += n
        return refs[pos - n:pos]

    rkv_ref, misc_ref, qk_ref, v_ref, o_ref, misc2_ref = take(N_COLS)
    sh_rkv_ref, sh_misc_ref, s0_ref, conv0_ref, c0_ref, n0_ref, m0_ref = take(N_STATE)
    rw_prm, ml_prm = take(N_RW_PRM), take(N_ML_PRM)
    take(2 if aliased else 0)
    y_rw_ref, y_ml_ref, s_ref, c_ref, n_ref, m_ref = take(6)
    carry_rkv, carry_misc, r_s, k_s, v_s, a_s, lw_s, g_s, ext = take(9)
    c = pl.program_id(1)

    @pl.when(c == 0)
    def _():
        carry_rkv[...] = sh_rkv_ref[:, 0]
        carry_misc[...] = sh_misc_ref[:, 0]
        s_ref[...] = s0_ref[...]
        c_ref[...] = c0_ref[...]
        n_ref[...] = n0_ref[...]
        m_ref[...] = m0_ref[...]
        ext[:, 0:SUBLANE, :] = jnp.zeros((n_seq, SUBLANE, ext.shape[2]), F32)

    parts = [
        _mlstm_steps(c, tuple(ref.at[q] for ref in (qk_ref, v_ref, o_ref, misc2_ref, conv0_ref)) + tuple(ml_prm)
                     + tuple(ref.at[q] for ref in (y_ml_ref, c_ref, n_ref, m_ref, ext)),
                     rows=rows, lead=lead, heads=heads, hd=hd)
        for q in range(n_seq)
    ]
    parts.append(_rwkv_steps(
        c, [tuple(ref.at[q] for ref in (rkv_ref, misc_ref, y_rw_ref, s_ref, carry_rkv, carry_misc,
                                        r_s, k_s, v_s, a_s, lw_s, g_s)) for q in range(n_seq)],
        rw_prm, rows=rows, lead=lead, width=width))
    while parts:
        for part in list(parts):
            if next(part, StopIteration) is StopIteration:
                parts.remove(part)


def _mixers(proj, proj_misc, y_prev, rw_state, ml_state, rw_prm, ml_prm, *,
            n_seq, batch, n_chunks, rows, seq_row, lead, width, heads, hd, d_model):
    g_rows = proj.shape[1]
    n_pairs = width // LANE
    per = batch // n_seq
    rkv_blk = (2 * d_model + 2 * width) // (3 * width)
    qk_blk = (2 * d_model) // (2 * width)
    v_blk = (2 * d_model + 2 * width + 3 * width) // width
    aliased = y_prev is not None
    kern = functools.partial(_mixers_kernel, n_seq=n_seq, aliased=aliased, rows=rows, lead=lead, width=width,
                             heads=heads, hd=hd)
    full = lambda shape: pl.BlockSpec(shape, lambda p, c: (0,) * len(shape))
    state = lambda *shape: pl.BlockSpec((n_seq, 1) + shape, lambda p, c: (0, p) + (0,) * len(shape))
    cols = lambda w_, blk: pl.BlockSpec((n_seq, rows, w_), lambda p, c: (0, seq_row(p, c), blk))
    col_specs = [cols(3 * width, rkv_blk), cols(MISC_COLS, 0), cols(2 * width, qk_blk),
                 cols(width, v_blk), cols(width, v_blk + 1), cols(MISC_COLS, 0)]
    s_spec, c_spec, n_spec, m_spec = state(n_pairs, LANE, LANE), state(heads, hd, hd), state(heads, hd), state(1, LANE)
    state_specs = [state(1, 3 * width), state(1, 3 * LANE), s_spec, state(SUBLANE, 2 * width), c_spec, n_spec, m_spec]
    rw_specs = [
        full((1, 3 * width)), full((1, 3 * LANE)),
        full((1, width)), full((LANE, width)), full((1, width)), full((LANE, width)), full((2 * LANE, width)),
        full((1, width)), full((1, width)), full((1, width)), full((1, width)), full((1, width)),
        full((CHUNK, CHUNK)), full((LANE, LANE)), full((2 * CHUNK, 2 * CHUNK)),
    ]
    ml_specs = [full(ml_prm[0].shape), full((1, 2 * width)), full((1, LANE)), full((1, width)), full((CHUNK, CHUNK))]
    assert (len(col_specs), len(state_specs), len(rw_specs), len(ml_specs)) == (N_COLS, N_STATE, N_RW_PRM, N_ML_PRM)
    assert lead >= SUBLANE and batch % n_seq == 0
    in_specs = col_specs + state_specs + rw_specs + ml_specs + [pl.BlockSpec(memory_space=pl.ANY)] * (2 * aliased)
    y_spec = cols(width, 0)
    out_specs = [y_spec, y_spec, s_spec, c_spec, n_spec, m_spec]
    sds = jax.ShapeDtypeStruct
    st = lambda *shape: sds((n_seq, per) + shape, F32)
    out_shape = [sds((n_seq, g_rows, width), BF16)] * 2 + [st(n_pairs, LANE, LANE), st(heads, hd, hd),
                                                           st(heads, hd), st(1, LANE)]
    vmem = pltpu.VMEM
    scratch = [vmem((n_seq, 1, 3 * width), F32), vmem((n_seq, 1, 3 * LANE), F32)]
    scratch += [vmem((n_seq, CHUNK, width), F32) for _ in range(6)]
    scratch += [vmem((n_seq, CHUNK + SUBLANE, 2 * width), F32)]
    n_in = len(in_specs)
    shift_rkv, shift_misc, s0 = rw_state
    conv0, c0, n0, m0 = ml_state
    outs = pl.pallas_call(
        kern, grid=(per, n_chunks), in_specs=in_specs, out_specs=out_specs, out_shape=out_shape,
        scratch_shapes=scratch, input_output_aliases={n_in - 2: 0, n_in - 1: 1} if aliased else {},
        compiler_params=_params("arbitrary", "arbitrary"), name="mixers",
    )(proj, proj_misc, proj, proj, proj, proj_misc, shift_rkv, shift_misc, s0, conv0, c0, n0, m0,
      *rw_prm, *ml_prm, *(y_prev or ()))
    return outs[:2], outs[2:]


def _pad_rows(w, rows):
    return jnp.pad(w, ((0, rows - w.shape[0]), (0, 0)))


def _swiglu_weights(w_gate, w_up, w_down):
    f = w_gate.shape[1]
    fp = -(-f // FF_TILE) * FF_TILE
    pad_cols = lambda w: jnp.pad(w.astype(BF16), ((0, 0), (0, fp - f)))
    return pad_cols(w_gate), pad_cols(w_up), _pad_rows(w_down.astype(BF16), fp)


def kernel(x_prompt, x_sample, state_rwkv_shift, state_rwkv_wkv, state_mlstm_conv, state_mlstm_C,
           state_mlstm_n, state_mlstm_m, meta_tokens, ffn1_norm, ffn1_w_gate, ffn1_w_up, ffn1_w_down,
           mix_norm, w_in, rw_mu, rw_w0, rw_w2, rw_a0, rw_a2, rw_g2, rw_kk, rw_ka, rw_rk, rw_ln_w, rw_ln_b,
           ml_conv_w, ml_conv_b, ml_i_b, ml_f_b, ml_norm_w, w_br_rw, w_br_ml, w_out,
           ffn2_norm, ffn2_w_gate, ffn2_w_up, ffn2_w_down, final_norm):
    assert ffn1_norm.shape[0] == 1, "single-layer trunk"
    B, T, D = x_prompt.shape
    Bs, Ts, _ = x_sample.shape
    n_meta = meta_tokens.shape[0]
    W = rw_w0.shape[-1]
    dl, al, gl = rw_w2.shape[1], rw_a2.shape[1], rw_g2.shape[1]
    rw_heads, rw_hd = rw_rk.shape[1], rw_rk.shape[2]
    Wm = ml_norm_w.shape[-1]
    H = ml_i_b.shape[-1]
    hd = Wm // H
    K = ml_conv_w.shape[1]
    L = CHUNK
    assert rw_hd == LANE // 2 and dl == LANE // 2 and al == LANE // 2 and gl <= 2 * LANE
    assert K == 4 and hd % LANE == 0 and 2 * H <= LANE and W == Wm and 2 * W == D
    assert Ts <= L and Ts % (2 * SUBLANE) == 0 and (11 * D // 2) % PROJ_COL_TILE == 0

    S = SEQS_PER_STEP
    lead = (-n_meta) % L
    head_rows = lead + n_meta
    n_head = head_rows // L
    n_chunks = n_head + T // L
    lead_s = L - Ts
    pb, ps = B // S, Bs // S
    main_rows, extra_rows = pb * T, pb * head_rows + ps * Ts
    G = main_rows + extra_rows
    N = S * G
    assert B % S == 0 and Bs % S == 0 and T % L == 0 and T % ROW_TILE == 0
    assert extra_rows % ROW_TILE == 0 and N % PROJ_ROW_TILE == 0 and N % MERGE_ROW_TILE == 0
    head = jnp.concatenate([jnp.zeros((lead, D), F32), meta_tokens.astype(F32)], axis=0)
    x_extra = jnp.concatenate([jnp.broadcast_to(head[None, None], (S, pb, head_rows, D)).reshape(S, -1, D),
                               x_sample.reshape(S, ps * Ts, D)], axis=1).reshape(S * extra_rows, D)

    ffn1_w = _swiglu_weights(ffn1_w_gate[0], ffn1_w_up[0], ffn1_w_down[0])
    ffn2_w = _swiglu_weights(ffn2_w_gate[0], ffn2_w_up[0], ffn2_w_down[0])
    wi = w_in[0]
    o_lora = 3 * W
    o_qk = o_lora + dl + al + gl
    o_v = o_qk + 2 * Wm
    o_o = o_v + Wm
    o_i = o_o + Wm
    o_gate = o_i + 2 * H
    zc = lambda n: jnp.zeros((D, n), wi.dtype)
    misc_w = jnp.concatenate([wi[:, o_lora:o_qk], zc(MISC_IF - (dl + al + gl)),
                              wi[:, o_i:o_gate], zc(MISC_COLS - MISC_IF - 2 * H)], axis=1)
    w_proj = jnp.concatenate([wi[:, o_gate:], wi[:, o_qk:o_v], wi[:, :o_lora], wi[:, o_v:o_i]],
                             axis=1).astype(BF16)

    def misc_vec(v_lora, fill=0.0):
        return jnp.pad(v_lora, [(0, 0)] * (v_lora.ndim - 1) + [(0, 3 * LANE - v_lora.shape[-1])],
                       constant_values=fill)

    row = lambda v: v.reshape(1, -1).astype(F32)
    rw_prm = (
        row(rw_mu[0, :o_lora]), misc_vec(row(rw_mu[0, o_lora:])),
        row(rw_w0[0]), _pad_rows(rw_w2[0], LANE).astype(BF16),
        row(rw_a0[0]), jnp.concatenate([jnp.zeros((dl, W), F32), rw_a2[0]], axis=0).astype(BF16),
        _pad_rows(rw_g2[0], 2 * LANE).astype(BF16),
        row(rw_kk[0]), row(rw_ka[0]), row(rw_rk[0]), row(rw_ln_w[0]), row(rw_ln_b[0]),
    )
    ifb = jnp.pad(jnp.concatenate([ml_i_b[0], ml_f_b[0]]).reshape(1, 2 * H), ((0, 0), (0, LANE - 2 * H)))
    tri = jnp.tril(jnp.ones((L, L), BF16))
    same_head = jnp.kron(jnp.eye(LANE // rw_hd, dtype=BF16), jnp.ones((rw_hd, rw_hd), BF16))
    rw_prm = rw_prm + (tri, same_head, jnp.eye(2 * L, dtype=F32))
    ml_prm = (ml_conv_w[0].astype(F32), row(ml_conv_b[0]), ifb.astype(F32), row(ml_norm_w[0]), tri)

    def rw_state_in(shift, wkv):
        b = shift.shape[0]
        s = wkv.reshape(b, rw_heads // 2, 2, rw_hd, rw_hd)
        z = jnp.zeros_like(s[:, :, 0])
        bd = jnp.concatenate([jnp.concatenate([s[:, :, 0], z], axis=-1),
                              jnp.concatenate([z, s[:, :, 1]], axis=-1)], axis=-2)
        return shift[:, None, :o_lora], misc_vec(shift[:, None, o_lora:]), bd

    def rw_state_out(bd):
        h = rw_hd
        return jnp.stack([bd[:, :, :h, :h], bd[:, :, h:, h:]], axis=2).reshape(bd.shape[0], rw_heads, h, h)

    def conv_in(buf):
        return jnp.pad(buf, ((0, 0), (SUBLANE - (K - 1), 0), (0, 0)))

    def m_in(m):
        return jnp.pad(m, ((0, 0), (0, LANE - H)))[:, None, :]

    x1 = _ffn((x_prompt, x_extra), row(ffn1_norm[0]), *ffn1_w, row(final_norm),
              groups=S, split_out=None, final_norm=False)
    proj, proj_misc = _proj(x1, row(mix_norm[0]), w_proj, misc_w.astype(BF16))
    proj_g, proj_misc_g = proj.reshape(S, G, -1), proj_misc.reshape(S, G, -1)

    zeros = lambda *s: jnp.zeros(s, F32)
    main_chunks = T // L

    def prompt_row(p, c):
        return jnp.where(c < n_head, main_rows // L + p * n_head + c, p * main_chunks + c - n_head)

    sample_base = main_rows + pb * head_rows
    seqs = (
        dict(batch=B, n_chunks=n_chunks, rows=L, seq_row=prompt_row, lead=lead),
        dict(batch=Bs, n_chunks=1, rows=Ts, seq_row=lambda p, c: sample_base // Ts + p, lead=lead_s),
    )
    grouped = lambda a: a.reshape((S, a.shape[0] // S) + a.shape[1:])
    flat = lambda a: a.reshape((a.shape[0] * a.shape[1],) + a.shape[2:])
    rw_states = (
        rw_state_in(zeros(B, o_qk), zeros(B, rw_heads, rw_hd, rw_hd)),
        rw_state_in(state_rwkv_shift[0], state_rwkv_wkv[0]),
    )
    ml_states = (
        (conv_in(zeros(B, K - 1, 2 * Wm)), zeros(B, H, hd, hd), zeros(B, H, hd), m_in(zeros(B, H))),
        (conv_in(state_mlstm_conv[0]), state_mlstm_C[0], state_mlstm_n[0], m_in(state_mlstm_m[0])),
    )
    ys = None
    rw_out, ml_out = [], []
    for seq, rws, mls in zip(seqs, rw_states, ml_states):
        ys, (s_end, c_end, n_end, m_end) = _mixers(
            proj_g, proj_misc_g, ys, tuple(map(grouped, rws)), tuple(map(grouped, mls)), rw_prm, ml_prm,
            n_seq=S, width=W, heads=H, hd=hd, d_model=D, **seq)
        rw_out.append(rw_state_out(flat(s_end)))
        ml_out.append((flat(c_end), flat(n_end), flat(m_end)[:, 0, :H]))

    x2 = _merge(x1, ys[0].reshape(N, W), ys[1].reshape(N, Wm), proj, w_br_rw[0].astype(BF16),
                w_br_ml[0].astype(BF16), w_out[0].astype(BF16))
    y_prompt, y_extra = _ffn((x2,), row(ffn2_norm[0]), *ffn2_w, row(final_norm),
                             groups=S, split_out=(B, T), final_norm=True)

    c_rkv = 2 * D + 2 * Wm

    def seq_states(base, per_group, t_len):
        seq = jnp.arange(S * per_group)
        end = (seq // per_group) * G + base + (seq % per_group + 1) * t_len
        last = end[:, None] - (K - 1) + jnp.arange(K - 1)[None, :]
        tail = proj[last.reshape(-1)].reshape(S * per_group, K - 1, -1)
        tail_misc = proj_misc[last.reshape(-1)].reshape(S * per_group, K - 1, -1)
        shift = jnp.concatenate([tail[:, -1, c_rkv:c_rkv + 3 * W], tail_misc[:, -1, :dl + al + gl]], axis=-1)
        return shift, tail[:, :, 2 * D:2 * D + 2 * Wm]

    p_shift, p_conv = seq_states(0, pb, T)
    s_shift, s_conv = seq_states(sample_base, ps, Ts)
    y_sample = y_extra.reshape(S, extra_rows, D)[:, pb * head_rows:].reshape(Bs, Ts, D)
    d1 = lambda a: a[None]
    return (y_prompt, y_sample,
            d1(p_shift), d1(rw_out[0]), d1(p_conv), d1(ml_out[0][0]), d1(ml_out[0][1]), d1(ml_out[0][2]),
            d1(s_shift), d1(rw_out[1]), d1(s_conv), d1(ml_out[1][0]), d1(ml_out[1][1]), d1(ml_out[1][2]))
```
